```python
import jax, jax.numpy as jnp
from jax import lax
import numpy as np

D_MODEL = 2048
BATCH = 4
SEQ = 2048
DEPTH = 1

HEAD_DIM = 128
MIX_WIDTH = D_MODEL
N_HEADS_NA = 8
N_HEADS_DIL = MIX_WIDTH // HEAD_DIM - N_HEADS_NA
WIDTH_NA = N_HEADS_NA * HEAD_DIM
WIDTH_DIL = N_HEADS_DIL * HEAD_DIM
GRID_W = 64
NA_ROWS = 8
NA_COLS = 16
DIL_PATTERNS = ((128, 1), (512, 4), (2048, 16))
DIL_QBLOCK = 64
T5_BUCKETS = 32
T5_MAX_DIST = 1024
N_EXPERTS = 32
TOP_K = 4
D_FF = D_MODEL
SWIGLU_LIMIT = 7.0
SWIGLU_ALPHA = 1.702
MOE_BLOCK = 256
EPS = 1e-6
NEG = -1e30

kernel_name = 'hybrid_na_dilated_moe_block'


def rms_norm(x, g):
    xf = x.astype(jnp.float32)
    y = xf * lax.rsqrt(jnp.mean(xf * xf, axis=-1, keepdims=True) + EPS)
    return (y * g.astype(jnp.float32)).astype(x.dtype)


def _t5_bucket(rel):
    nb = T5_BUCKETS // 2
    max_exact = nb // 2
    ret = (rel > 0).astype(np.int32) * nb
    n = np.abs(rel)
    large = max_exact + (np.log(np.maximum(n, 1) / max_exact) / np.log(T5_MAX_DIST / max_exact)
                         * (nb - max_exact)).astype(np.int32)
    large = np.minimum(large, nb - 1)
    return (ret + np.where(n < max_exact, n, large)).astype(np.int32)


def neighbourhood_attention(q, k, v, rpb):
    b, s, h, dh = q.shape
    rows = s // GRID_W
    kr = min(NA_ROWS, rows)
    r = np.arange(rows)
    row_start = np.clip(r - kr // 2, 0, rows - kr)
    row_idx = row_start[:, None] + np.arange(kr)[None, :]
    cidx = np.arange(GRID_W)
    col_start = np.clip(cidx - NA_COLS // 2, 0, GRID_W - NA_COLS)
    col_ok = (cidx[None, :] >= col_start[:, None]) & (cidx[None, :] < col_start[:, None] + NA_COLS)
    dr = row_idx - r[:, None] + NA_ROWS - 1
    dc = np.clip(cidx[None, :] - cidx[:, None] + NA_COLS - 1, 0, 2 * NA_COLS - 2)

    def to_grid(t):
        return t.reshape(b, rows, GRID_W, h, dh).transpose(0, 3, 1, 2, 4)

    qg, kg, vg = to_grid(q), to_grid(k), to_grid(v)
    k_rows = kg[:, :, row_idx]
    v_rows = vg[:, :, row_idx]
    logits = jnp.einsum('bhrqd,bhrkcd->bhrqkc', qg, k_rows).astype(jnp.float32) * (dh ** -0.5)
    bias = rpb[:, dr[:, None, :, None], dc[None, :, None, :]]
    logits = logits + bias.astype(jnp.float32)[None]
    logits = jnp.where(col_ok[None, None, None, :, None, :], logits, NEG)
    probs = jax.nn.softmax(logits.reshape(b, h, rows, GRID_W, kr * GRID_W), axis=-1)
    probs = probs.reshape(logits.shape).astype(v.dtype)
    out = jnp.einsum('bhrqkc,bhrkcd->bhrqd', probs, v_rows)
    return out.transpose(0, 2, 3, 1, 4).reshape(b, s, h * dh)


def _to_sub(t, sub_len, dil):
    b, h, s, dh = t.shape
    return t.reshape(b, h, sub_len, dil, dh).transpose(0, 1, 3, 2, 4)


def dilated_attention(q, k, v, t5_table):
    b, s, h, dh = q.shape
    qh, kh, vh = (t.transpose(0, 2, 1, 3) for t in (q, k, v))
    outs, lses = [], []
    for window, dil in DIL_PATTERNS:
        half = window // (2 * dil)
        sub_len = s // dil
        nblk = -(-sub_len // DIL_QBLOCK)
        lp = nblk * DIL_QBLOCK
        kw = DIL_QBLOCK + 2 * half
        qs = jnp.pad(_to_sub(qh, sub_len, dil), ((0, 0),) * 3 + ((0, lp - sub_len), (0, 0)))
        qs = qs.reshape(b, h, dil, nblk, DIL_QBLOCK, dh)
        pad_k = ((0, 0),) * 3 + ((half, lp - sub_len + half), (0, 0))
        ks = jnp.pad(_to_sub(kh, sub_len, dil), pad_k)
        vs = jnp.pad(_to_sub(vh, sub_len, dil), pad_k)
        kidx = np.arange(nblk)[:, None] * DIL_QBLOCK + np.arange(kw)[None, :]
        kb = ks[:, :, :, kidx]
        vb = vs[:, :, :, kidx]
        qi = np.arange(DIL_QBLOCK)
        kj = np.arange(kw)
        delta = kj[None, :] - half - qi[:, None]
        key_l = np.arange(nblk)[:, None] * DIL_QBLOCK + kj[None, :] - half
        valid = (np.abs(delta) <= half)[None] & ((key_l >= 0) & (key_l < sub_len))[:, None, :]
        bias = t5_table[_t5_bucket(delta * dil)].transpose(2, 0, 1).astype(jnp.float32)
        logits = jnp.einsum('bhrnqd,bhrnkd->bhrnqk', qs, kb).astype(jnp.float32) * (dh ** -0.5)
        logits = logits + bias[None, :, None, None]
        logits = jnp.where(valid[None, None, None], logits, NEG)
        m = jnp.max(logits, axis=-1, keepdims=True)
        p = jnp.exp(logits - m)
        den = jnp.sum(p, axis=-1, keepdims=True)
        o = jnp.einsum('bhrnqk,bhrnkd->bhrnqd', (p / den).astype(v.dtype), vb)
        lse = (m + jnp.log(den))[..., 0]
        o = o.reshape(b, h, dil, lp, dh)[:, :, :, :sub_len].transpose(0, 1, 3, 2, 4).reshape(b, h, s, dh)
        lse = lse.reshape(b, h, dil, lp)[..., :sub_len].transpose(0, 1, 3, 2).reshape(b, h, s)
        outs.append(o)
        lses.append(lse)
    wts = jax.nn.softmax(jnp.stack(lses, axis=0), axis=0)
    out = jnp.sum(wts[..., None] * jnp.stack(outs, axis=0).astype(jnp.float32), axis=0)
    return out.astype(q.dtype).transpose(0, 2, 1, 3).reshape(b, s, h * dh)


def moe_ffn(hid, w_router, b_router, w_e_in, b_e_in, w_e_out, b_e_out):
    t, d = hid.shape
    logits = hid.astype(jnp.float32) @ w_router.astype(jnp.float32) + b_router.astype(jnp.float32)
    top_val, top_idx = lax.top_k(logits, TOP_K)
    gates = jax.nn.softmax(top_val, axis=-1)
    n_assign = t * TOP_K
    e_flat = top_idx.reshape(n_assign).astype(jnp.int32)
    tok_flat = jnp.broadcast_to(jnp.arange(t, dtype=jnp.int32)[:, None], (t, TOP_K)).reshape(n_assign)
    g_flat = gates.reshape(n_assign)
    order = jnp.argsort(e_flat)
    e_sorted = e_flat[order]
    counts = jax.ops.segment_sum(jnp.ones((n_assign,), jnp.int32), e_flat, num_segments=N_EXPERTS)
    padded = (counts + MOE_BLOCK - 1) // MOE_BLOCK * MOE_BLOCK
    start = jnp.cumsum(counts) - counts
    pend = jnp.cumsum(padded)
    pstart = pend - padded
    dest = pstart[e_sorted] + jnp.arange(n_assign, dtype=jnp.int32) - start[e_sorted]
    n_slots = n_assign + N_EXPERTS * MOE_BLOCK
    nblk = n_slots // MOE_BLOCK
    slot_tok = jnp.full((n_slots,), t, jnp.int32).at[dest].set(tok_flat[order])
    slot_gate = jnp.zeros((n_slots,), jnp.float32).at[dest].set(g_flat[order])
    block_e = jnp.minimum(jnp.searchsorted(pend, jnp.arange(nblk, dtype=jnp.int32) * MOE_BLOCK, side='right'),
                          N_EXPERTS - 1)
    hid_pad = jnp.concatenate([hid, jnp.zeros((1, d), hid.dtype)], axis=0)

    def expert_block(args):
        tok, e = args
        z = hid_pad[tok] @ w_e_in[e] + b_e_in[e]
        gate = jnp.minimum(z[:, 0::2], SWIGLU_LIMIT)
        up = jnp.clip(z[:, 1::2], -SWIGLU_LIMIT, SWIGLU_LIMIT)
        glu = gate * jax.nn.sigmoid(SWIGLU_ALPHA * gate)
        return ((up + 1.0) * glu) @ w_e_out[e] + b_e_out[e]

    ys = lax.map(expert_block, (slot_tok.reshape(nblk, MOE_BLOCK), block_e))
    ys = ys.reshape(n_slots, d).astype(jnp.float32) * slot_gate[:, None]
    out = jax.ops.segment_sum(ys, slot_tok, num_segments=t + 1)[:t]
    return out.astype(hid.dtype)


def setup_inputs(seed: int = 0) -> dict:
    key = jax.random.key(seed)
    ks = jax.random.split(key, 21)
    f32 = jnp.float32
    d = D_MODEL

    def nrm(k, shape, scale):
        return jax.random.normal(k, shape, f32) * scale

    return {
        'x': nrm(ks[0], (BATCH, SEQ, d), 1.0),
        'c': nrm(ks[1], (BATCH, d), 1.0),
        'w_ada': nrm(ks[2], (DEPTH, d, 6 * d), 0.5 * d ** -0.5),
        'b_ada': nrm(ks[3], (DEPTH, 6 * d), 0.02),
        'g_pre_mix': 1.0 + nrm(ks[4], (DEPTH, d), 0.02),
        'g_post_mix': 1.0 + nrm(ks[5], (DEPTH, d), 0.02),
        'w_in': nrm(ks[6], (DEPTH, d, 3 * MIX_WIDTH), d ** -0.5),
        'rpb_na': nrm(ks[7], (DEPTH, N_HEADS_NA, 2 * NA_ROWS - 1, 2 * NA_COLS - 1), 0.1),
        't5_table': nrm(ks[8], (T5_BUCKETS, N_HEADS_DIL), 0.1),
        'g_out_na': 1.0 + nrm(ks[9], (DEPTH, WIDTH_NA), 0.02),
        'g_out_dil': 1.0 + nrm(ks[10], (DEPTH, WIDTH_DIL), 0.02),
        'w_o': nrm(ks[11], (DEPTH, MIX_WIDTH, d), MIX_WIDTH ** -0.5),
        'g_pre_ffn': 1.0 + nrm(ks[12], (DEPTH, d), 0.02),
        'g_post_ffn': 1.0 + nrm(ks[13], (DEPTH, d), 0.02),
        'w_router': nrm(ks[14], (DEPTH, d, N_EXPERTS), d ** -0.5),
        'b_router': nrm(ks[15], (DEPTH, N_EXPERTS), 0.01),
        'w_e_in': nrm(ks[16], (DEPTH, N_EXPERTS, d, 2 * D_FF), d ** -0.5),
        'b_e_in': nrm(ks[17], (DEPTH, N_EXPERTS, 2 * D_FF), 0.02),
        'w_e_out': nrm(ks[18], (DEPTH, N_EXPERTS, D_FF, d), D_FF ** -0.5),
        'b_e_out': nrm(ks[19], (DEPTH, N_EXPERTS, d), 0.02),
    }


def reference(x, c, w_ada, b_ada, g_pre_mix, g_post_mix, w_in, rpb_na, t5_table, g_out_na, g_out_dil,
              w_o, g_pre_ffn, g_post_ffn, w_router, b_router, w_e_in, b_e_in, w_e_out, b_e_out):
    b, s, d = x.shape
    for l in range(DEPTH):
        mod = jax.nn.silu(c) @ w_ada[l] + b_ada[l]
        sh1, sc1, gt1, sh2, sc2, gt2 = (m[:, None, :] for m in jnp.split(mod, 6, axis=-1))

        hmix = rms_norm(x, g_pre_mix[l]) * (1.0 + sc1) + sh1
        proj = hmix @ w_in[l]
        qa, ka, va = (t.reshape(b, s, N_HEADS_NA, HEAD_DIM)
                      for t in jnp.split(proj[..., :3 * WIDTH_NA], 3, axis=-1))
        qb, kb, vb = (t.reshape(b, s, N_HEADS_DIL, HEAD_DIM)
                      for t in jnp.split(proj[..., 3 * WIDTH_NA:], 3, axis=-1))
        out_na = neighbourhood_attention(qa, ka, va, rpb_na[l])
        out_dil = dilated_attention(qb, kb, vb, t5_table)
        mixed = jnp.concatenate([rms_norm(out_na, g_out_na[l]), rms_norm(out_dil, g_out_dil[l])], axis=-1) @ w_o[l]
        x = x + gt1 * rms_norm(mixed, g_post_mix[l])

        hffn = rms_norm(x, g_pre_ffn[l]) * (1.0 + sc2) + sh2
        y = moe_ffn(hffn.reshape(b * s, d), w_router[l], b_router[l], w_e_in[l], b_e_in[l],
                    w_e_out[l], b_e_out[l]).reshape(b, s, d)
        x = x + gt2 * rms_norm(y, g_post_ffn[l])
    return x
```

```python
import functools

import numpy as np
import jax
import jax.numpy as jnp
from jax import lax
from jax.experimental import pallas as pl
from jax.experimental.pallas import tpu as pltpu

F32 = jnp.float32
BF16 = jnp.bfloat16
U32 = jnp.uint32
I32 = jnp.int32

HEAD_DIM = 128
N_HEADS_NA = 8
N_HEADS_DIL = 8
GRID_W = 64
NA_ROWS = 8
NA_COLS = 16
DIL_PATTERNS = ((128, 1), (512, 4), (2048, 16))
T5_BUCKETS = 32
T5_MAX_DIST = 1024
N_EXPERTS = 32
TOP_K = 4
SWIGLU_LIMIT = 7.0
SWIGLU_ALPHA = 1.702
EPS = 1e-6
NEG = -1e30
SCALE = HEAD_DIM ** -0.5

LANES = 128
QBLK = 128
ROW_BLK = 256
SUPER_BLKS = 5
SUPER_ROWS = ROW_BLK * SUPER_BLKS
F_CHUNK = 256
VMEM_LIMIT = 56 * 1024 * 1024


def _cparams(sem, vmem=VMEM_LIMIT):
    return pltpu.CompilerParams(dimension_semantics=sem, vmem_limit_bytes=vmem)


def _rms(x, g):
    return x * lax.rsqrt(jnp.mean(x * x, axis=-1, keepdims=True) + EPS) * g


def _ada_kernel(c_ref, w_ref, b_ref, o_ref):
    c = c_ref[...]
    s = c * jax.nn.sigmoid(c)
    o_ref[...] = jnp.dot(s.astype(BF16), w_ref[...].astype(BF16),
                         preferred_element_type=F32) + b_ref[...]


def _ada_mod(c, w, b):
    bsz, d = c.shape
    n = w.shape[1]
    tn = 1024
    cp = jnp.zeros((8, d), F32).at[:bsz].set(c)
    out = pl.pallas_call(
        _ada_kernel,
        out_shape=jax.ShapeDtypeStruct((8, n), F32),
        grid=(n // tn,),
        in_specs=[pl.BlockSpec((8, d), lambda j: (0, 0)),
                  pl.BlockSpec((d, tn), lambda j: (0, j)),
                  pl.BlockSpec((1, tn), lambda j: (0, j))],
        out_specs=pl.BlockSpec((8, tn), lambda j: (0, j)),
        compiler_params=_cparams(("arbitrary",)),
        name="ada_mod",
    )(cp, w, b.reshape(1, n))
    return out[:bsz]


def _qkv_kernel(x_ref, g_ref, sc_ref, sh_ref, w_ref, o_ref, h_scr):
    @pl.when(pl.program_id(1) == 0)
    def _():
        h = _rms(x_ref[...], g_ref[...]) * (1.0 + sc_ref[...]) + sh_ref[...]
        h_scr[...] = h.astype(BF16)

    acc = jnp.dot(h_scr[...], w_ref[...], preferred_element_type=F32)
    for u in range(o_ref.shape[0]):
        o_ref[u] = acc[:, u * LANES:(u + 1) * LANES].astype(BF16)


def _qkv_proj(x2, g, sc, sh, w_bf, seq):
    t, d = x2.shape
    n = w_bf.shape[1]
    tm, tn = 1024, 1024
    return pl.pallas_call(
        _qkv_kernel,
        out_shape=jax.ShapeDtypeStruct((n // LANES, t, LANES), BF16),
        grid=(t // tm, n // tn),
        in_specs=[pl.BlockSpec((tm, d), lambda i, j: (i, 0)),
                  pl.BlockSpec((1, d), lambda i, j: (0, 0)),
                  pl.BlockSpec((None, 1, d), lambda i, j: (i * tm // seq, 0, 0)),
                  pl.BlockSpec((None, 1, d), lambda i, j: (i * tm // seq, 0, 0)),
                  pl.BlockSpec((d, tn), lambda i, j: (0, j))],
        out_specs=pl.BlockSpec((tn // LANES, tm, LANES), lambda i, j: (j, i, 0)),
        scratch_shapes=[pltpu.VMEM((tm, d), BF16)],
        compiler_params=_cparams(("arbitrary", "arbitrary")),
        name="qkv_proj",
    )(x2, g.reshape(1, d), sc, sh, w_bf)


def _na_bias_table(rpb):
    cidx = np.arange(GRID_W)
    col_start = np.clip(cidx - NA_COLS // 2, 0, GRID_W - NA_COLS)
    col_ok = (cidx[None, :] >= col_start[:, None]) & (cidx[None, :] < col_start[:, None] + NA_COLS)
    dc = np.clip(cidx[None, :] - cidx[:, None] + NA_COLS - 1, 0, 2 * NA_COLS - 2)
    dr = np.arange(NA_ROWS)[:, None] + np.arange(NA_ROWS)[None, :]
    tab = rpb[:, dr[:, None, :, None], dc[None, :, None, :]]
    tab = jnp.where(col_ok[None, None, :, None, :], tab.astype(F32), NEG)
    return tab.reshape(rpb.shape[0], NA_ROWS, GRID_W, NA_ROWS * GRID_W)


def _na_kernel(q_ref, k_ref, v_ref, bias_ref, o_ref, *, rows):
    nkeys = NA_ROWS * GRID_W

    def body(r, carry):
        rs = jnp.clip(r - NA_ROWS // 2, 0, rows - NA_ROWS)
        var = rs - r + (NA_ROWS - 1)
        q0 = pl.multiple_of(r * GRID_W, GRID_W)
        k0 = pl.multiple_of(rs * GRID_W, GRID_W)
        q = q_ref[pl.ds(q0, GRID_W), :]
        k = k_ref[pl.ds(k0, nkeys), :]
        v = v_ref[pl.ds(k0, nkeys), :]
        s = lax.dot_general(q, k, (((1,), (1,)), ((), ())), preferred_element_type=F32) * SCALE
        s = s + bias_ref[var]
        m = jnp.max(s, axis=-1, keepdims=True)
        p = jnp.exp(s - m)
        l = jnp.sum(p, axis=-1, keepdims=True)
        o = jnp.dot(p.astype(BF16), v, preferred_element_type=F32)
        o_ref[pl.ds(q0, GRID_W), :] = o / l
        return carry

    lax.fori_loop(0, rows, body, 0)


def _na_attn(proj, bias, bsz, seq):
    nh = N_HEADS_NA
    t = bsz * seq
    blk = lambda off: pl.BlockSpec((None, seq, LANES), lambda h, b: (h + off, b, 0))
    return pl.pallas_call(
        functools.partial(_na_kernel, rows=seq // GRID_W),
        out_shape=jax.ShapeDtypeStruct((nh, t, LANES), F32),
        grid=(nh, bsz),
        in_specs=[blk(0), blk(nh), blk(2 * nh),
                  pl.BlockSpec((None, NA_ROWS, GRID_W, NA_ROWS * GRID_W), lambda h, b: (h, 0, 0, 0))],
        out_specs=pl.BlockSpec((None, seq, LANES), lambda h, b: (h, b, 0)),
        compiler_params=_cparams(("arbitrary", "arbitrary")),
        name="na_attn",
    )(proj, proj, proj, bias)


def _t5_bucket(rel):
    nb = T5_BUCKETS // 2
    max_exact = nb // 2
    ret = (rel > 0).astype(np.int32) * nb
    n = np.abs(rel)
    large = max_exact + (np.log(np.maximum(n, 1) / max_exact) / np.log(T5_MAX_DIST / max_exact)
                         * (nb - max_exact)).astype(np.int32)
    large = np.minimum(large, nb - 1)
    return (ret + np.where(n < max_exact, n, large)).astype(np.int32)


def _dil_geometry(sub_len):
    half = DIL_PATTERNS[0][0] // 2
    width = min(sub_len, QBLK + 2 * half)
    nblk = sub_len // QBLK
    starts = [min(max(QBLK * n - half, 0), sub_len - width) for n in range(nblk)]
    offs = sorted({ws - QBLK * n for n, ws in enumerate(starts)}, reverse=True)
    var = [offs.index(ws - QBLK * n) for n, ws in enumerate(starts)]
    return width, starts, offs, var


def _dil_bias_table(t5_table, dil, width, offs):
    half = DIL_PATTERNS[0][0] // 2
    i = np.arange(QBLK)[:, None]
    jj = np.arange(width)[None, :]
    tabs = []
    for off in offs:
        delta = jj - i + off
        b = t5_table[_t5_bucket(delta * dil)].astype(F32)
        tabs.append(jnp.where((np.abs(delta) <= half)[..., None], b, NEG))
    return jnp.stack(tabs).transpose(3, 0, 1, 2)


def _window_attn(q, k, v, bias):
    s = lax.dot_general(q, k, (((1,), (1,)), ((), ())), preferred_element_type=F32) * SCALE + bias
    m = jnp.max(s, axis=-1, keepdims=True)
    p = jnp.exp(s - m)
    l = jnp.sum(p, axis=-1, keepdims=True)
    o = jnp.dot(p.astype(BF16), v, preferred_element_type=F32) / l
    return o, m + jnp.log(l)


def _dil_sub_kernel(q_ref, k_ref, v_ref, bias_ref, o_ref, *, width, starts, var):
    for n, ws in enumerate(starts):
        rows = slice(QBLK * n, QBLK * (n + 1))
        o, lse = _window_attn(q_ref[rows, :], k_ref[ws:ws + width, :], v_ref[ws:ws + width, :],
                              bias_ref[var[n]])
        o_ref[rows, 0:LANES] = o
        o_ref[rows, LANES:2 * LANES] = jnp.broadcast_to(lse, (QBLK, LANES))


def _dil_sub(proj, t5_table, dil, bsz, seq):
    nh = N_HEADS_DIL
    sub_len = seq // dil
    width, starts, offs, var = _dil_geometry(sub_len)
    bias = _dil_bias_table(t5_table, dil, width, offs)
    view = proj.reshape(proj.shape[0], bsz, sub_len, dil * LANES)
    first = 3 * N_HEADS_NA
    blk = lambda off: pl.BlockSpec((None, None, sub_len, LANES), lambda h, b, r: (first + off + h, b, 0, r))
    out = pl.pallas_call(
        functools.partial(_dil_sub_kernel, width=width, starts=starts, var=var),
        out_shape=jax.ShapeDtypeStruct((nh, bsz, sub_len, dil * 2 * LANES), F32),
        grid=(nh, bsz, dil),
        in_specs=[blk(0), blk(nh), blk(2 * nh),
                  pl.BlockSpec((None, len(offs), QBLK, width), lambda h, b, r: (h, 0, 0, 0))],
        out_specs=pl.BlockSpec((None, None, sub_len, 2 * LANES), lambda h, b, r: (h, b, 0, r)),
        compiler_params=_cparams(("arbitrary",) * 3),
        name=f"dil_sub{dil}",
    )(view, view, view, bias)
    return out.reshape(nh, bsz * seq, 2 * LANES)


def _dil_main_kernel(q_ref, k_ref, v_ref, bias_ref, a4_ref, a16_ref, o_ref, *, seq, width, nblk, half):
    def body(n, carry):
        q0 = pl.multiple_of(n * QBLK, QBLK)
        ws = pl.multiple_of(jnp.clip(n * QBLK - half, 0, seq - width), half)
        var = jnp.where(n == 0, 0, jnp.where(n == nblk - 1, 2, 1))
        o1, lse1 = _window_attn(q_ref[pl.ds(q0, QBLK), :], k_ref[pl.ds(ws, width), :],
                                v_ref[pl.ds(ws, width), :], bias_ref[var])
        a4 = a4_ref[pl.ds(q0, QBLK), :]
        a16 = a16_ref[pl.ds(q0, QBLK), :]
        o4, lse4 = a4[:, :LANES], a4[:, LANES:]
        o16, lse16 = a16[:, :LANES], a16[:, LANES:]
        mx = jnp.maximum(jnp.maximum(lse4, lse16), lse1)
        e1 = jnp.exp(lse1 - mx)
        e4 = jnp.exp(lse4 - mx)
        e16 = jnp.exp(lse16 - mx)
        o_ref[pl.ds(q0, QBLK), :] = (e1 * o1 + e4 * o4 + e16 * o16) / (e1 + e4 + e16)
        return carry

    lax.fori_loop(0, nblk, body, 0)


def _dil_main(proj, t5_table, aug4, aug16, bsz, seq):
    nh = N_HEADS_DIL
    width, starts, offs, var = _dil_geometry(seq)
    assert offs == [0, -(DIL_PATTERNS[0][0] // 2), -DIL_PATTERNS[0][0]] and var[0] == 0 and var[-1] == 2
    bias = _dil_bias_table(t5_table, 1, width, offs)
    first = 3 * N_HEADS_NA
    blk = lambda off: pl.BlockSpec((None, seq, LANES), lambda h, b: (first + off + h, b, 0))
    aug = pl.BlockSpec((None, seq, 2 * LANES), lambda h, b: (h, b, 0))
    return pl.pallas_call(
        functools.partial(_dil_main_kernel, seq=seq, width=width, nblk=len(starts),
                          half=DIL_PATTERNS[0][0] // 2),
        out_shape=jax.ShapeDtypeStruct((nh, bsz * seq, LANES), F32),
        grid=(nh, bsz),
        in_specs=[blk(0), blk(nh), blk(2 * nh),
                  pl.BlockSpec((None, len(offs), QBLK, width), lambda h, b: (h, 0, 0, 0)),
                  aug, aug],
        out_specs=pl.BlockSpec((None, seq, LANES), lambda h, b: (h, b, 0)),
        compiler_params=_cparams(("arbitrary", "arbitrary")),
        name="dil_main",
    )(proj, proj, proj, bias, aug4, aug16)


def _mix_kernel(na_ref, dl_ref, gna_ref, gdl_ref, wo_ref, x_ref, gt1_ref, sc2_ref, sh2_ref,
                gpost_ref, gpre_ref, wr_ref, br_ref, x1_ref, hp_ref, lg_ref):
    nh = na_ref.shape[0]
    na = jnp.concatenate([na_ref[h] for h in range(nh)], axis=1)
    dl = jnp.concatenate([dl_ref[h] for h in range(nh)], axis=1)
    lhs = jnp.concatenate([_rms(na, gna_ref[...]), _rms(dl, gdl_ref[...])], axis=1).astype(BF16)
    mixed = jnp.dot(lhs, wo_ref[...], preferred_element_type=F32)
    x1 = x_ref[...] + gt1_ref[...] * _rms(mixed, gpost_ref[...])
    x1_ref[...] = x1
    hf = _rms(x1, gpre_ref[...]) * (1.0 + sc2_ref[...]) + sh2_ref[...]
    hb = hf.astype(BF16)
    lg_ref[...] = jnp.dot(hb, wr_ref[...], preferred_element_type=F32) + br_ref[...]
    half = hb.shape[1] // 2
    lo = lax.bitcast_convert_type(hb[:, :half].astype(F32), U32)
    hi = lax.bitcast_convert_type(hb[:, half:].astype(F32), U32)
    hp_ref[...] = (hi & U32(0xFFFF0000)) | lax.shift_right_logical(lo, U32(16))


def _mix_out(out_na, out_dil, g_na, g_dil, wo_bf, x2, gt1, sc2, sh2, g_post, g_pre, wr_bf, br, seq):
    t, d = x2.shape
    nh = out_na.shape[0]
    tm = 256
    wna = g_na.shape[-1]
    row = lambda n: pl.BlockSpec((1, n), lambda i: (0, 0))
    per_b = pl.BlockSpec((None, 1, d), lambda i: (i * tm // seq, 0, 0))
    heads = pl.BlockSpec((nh, tm, LANES), lambda i: (0, i, 0))
    return pl.pallas_call(
        _mix_kernel,
        out_shape=(jax.ShapeDtypeStruct((t, d), F32),
                   jax.ShapeDtypeStruct((t, d // 2), U32),
                   jax.ShapeDtypeStruct((t, LANES), F32)),
        grid=(t // tm,),
        in_specs=[heads, heads, row(wna), row(wna),
                  pl.BlockSpec((d, d), lambda i: (0, 0)),
                  pl.BlockSpec((tm, d), lambda i: (i, 0)),
                  per_b, per_b, per_b, row(d), row(d),
                  pl.BlockSpec((d, LANES), lambda i: (0, 0)), row(LANES)],
        out_specs=(pl.BlockSpec((tm, d), lambda i: (i, 0)),
                   pl.BlockSpec((tm, d // 2), lambda i: (i, 0)),
                   pl.BlockSpec((tm, LANES), lambda i: (i, 0))),
        compiler_params=_cparams(("arbitrary",)),
        name="mix_out",
    )(out_na, out_dil, g_na.reshape(1, wna), g_dil.reshape(1, wna), wo_bf, x2, gt1, sc2, sh2,
      g_post.reshape(1, d), g_pre.reshape(1, d), wr_bf, br)


def _route_kernel(lg_ref, idx_ref, rank_ref, gate_ref, cnt_ref, carry):
    step = pl.program_id(0)

    @pl.when(step == 0)
    def _():
        carry[...] = jnp.zeros_like(carry)

    v = lg_ref[...]
    ch = v.shape[0]
    lane = lax.broadcasted_iota(I32, v.shape, 1).astype(F32)
    vals, idxs = [], []
    for _ in range(TOP_K):
        m = jnp.max(v, axis=1, keepdims=True)
        ik = jnp.min(jnp.where(v == m, lane, float(LANES)), axis=1, keepdims=True)
        vals.append(m)
        idxs.append(ik)
        v = jnp.where(lane == ik, -jnp.inf, v)
    es = [jnp.exp(val - vals[0]) for val in vals]
    den = es[0] + es[1] + es[2] + es[3]
    sel = jnp.zeros(v.shape, F32)
    for ik in idxs:
        sel = jnp.where(lane == ik, 1.0, sel)
    ti = lax.broadcasted_iota(I32, (ch, ch), 0)
    tj = lax.broadcasted_iota(I32, (ch, ch), 1)
    lower = jnp.where(tj < ti, 1.0, 0.0).astype(BF16)
    cum = jnp.dot(lower, sel.astype(BF16), preferred_element_type=F32) + carry[0:1, :]
    idx_o = jnp.zeros(v.shape, F32)
    rank_o = jnp.zeros(v.shape, F32)
    gate_o = jnp.zeros(v.shape, F32)
    for k in range(TOP_K):
        rk = jnp.sum(jnp.where(lane == idxs[k], cum, 0.0), axis=1, keepdims=True)
        idx_o = jnp.where(lane == float(k), idxs[k], idx_o)
        rank_o = jnp.where(lane == float(k), rk, rank_o)
        gate_o = jnp.where(lane == float(k), es[k] / den, gate_o)
    idx_ref[...] = idx_o.astype(I32)
    rank_ref[...] = rank_o.astype(I32)
    gate_ref[...] = gate_o
    total = carry[...] + jnp.sum(sel, axis=0, keepdims=True)
    carry[...] = total
    cnt_ref[...] = total.astype(I32)


def _route(logits):
    t = logits.shape[0]
    ch = 512
    blk = pl.BlockSpec((ch, LANES), lambda i: (i, 0))
    return pl.pallas_call(
        _route_kernel,
        out_shape=(jax.ShapeDtypeStruct((t, LANES), I32),
                   jax.ShapeDtypeStruct((t, LANES), I32),
                   jax.ShapeDtypeStruct((t, LANES), F32),
                   jax.ShapeDtypeStruct((8, LANES), I32)),
        grid=(t // ch,),
        in_specs=[blk],
        out_specs=(blk, blk, blk, pl.BlockSpec((8, LANES), lambda i: (0, 0))),
        scratch_shapes=[pltpu.VMEM((8, LANES), F32)],
        compiler_params=_cparams(("arbitrary",)),
        name="route",
    )(logits)


def _schedule(counts, n_assign):
    max_sb = n_assign // SUPER_ROWS + N_EXPERTS
    nsb_e = (counts + SUPER_ROWS - 1) // SUPER_ROWS
    sb_end = jnp.cumsum(nsb_e)
    sb_start = sb_end - nsb_e
    row_start = (sb_start * SUPER_ROWS).astype(I32)
    nsb = sb_end[-1].astype(I32)
    s = jnp.arange(max_sb, dtype=I32)
    sb_e = jnp.minimum(jnp.searchsorted(sb_end, s, side="right"), N_EXPERTS - 1).astype(I32)
    rem = counts[sb_e] - (s - sb_start[sb_e]) * SUPER_ROWS
    sb_nb = jnp.where(s < nsb, (jnp.clip(rem, 0, SUPER_ROWS) + ROW_BLK - 1) // ROW_BLK, 0).astype(I32)
    zero_start = (s * SUPER_ROWS + jnp.maximum(sb_nb - 1, 0) * ROW_BLK).astype(I32)
    return max_sb, row_start, nsb.reshape(1), sb_e, sb_nb, zero_start


def _dispatch_kernel(eidx_s, rank_s, rs_s, zs_s, nsb_s, hp_ref, xs_hbm, zbuf, zsem, sem):
    step = pl.program_id(0)
    tt = hp_ref.shape[0]

    def zero_copy(s):
        return pltpu.make_async_copy(zbuf, xs_hbm.at[pl.ds(pl.multiple_of(zs_s[s], ROW_BLK), ROW_BLK)], zsem)

    @pl.when(step == 0)
    def _():
        zbuf[...] = jnp.zeros_like(zbuf)

        def zstart(s, c):
            zero_copy(s).start()
            return c

        def zwait(s, c):
            zero_copy(s).wait()
            return c

        lax.fori_loop(0, nsb_s[0], zstart, 0)
        lax.fori_loop(0, nsb_s[0], zwait, 0)

    base = step * tt

    def row_copy(j, k):
        a = (base + j) * TOP_K + k
        d = rs_s[eidx_s[a]] + rank_s[a]
        return pltpu.make_async_copy(hp_ref.at[pl.ds(j, 1)], xs_hbm.at[pl.ds(d, 1)], sem)

    def issue(j, c):
        for k in range(TOP_K):
            row_copy(j, k).start()
        return c

    def drain(j, c):
        for k in range(TOP_K):
            row_copy(j, k).wait()
        return c

    lax.fori_loop(0, tt, issue, 0)
    lax.fori_loop(0, tt, drain, 0)


def _dispatch(eidx, rank, row_start, zero_start, nsb, hp, n_rows):
    t, w = hp.shape
    tt = 256
    return pl.pallas_call(
        _dispatch_kernel,
        out_shape=jax.ShapeDtypeStruct((n_rows, w), U32),
        grid_spec=pltpu.PrefetchScalarGridSpec(
            num_scalar_prefetch=5,
            grid=(t // tt,),
            in_specs=[pl.BlockSpec((tt, w), lambda i, *_: (i, 0))],
            out_specs=pl.BlockSpec(memory_space=pl.ANY),
            scratch_shapes=[pltpu.VMEM((ROW_BLK, w), U32),
                            pltpu.SemaphoreType.DMA(()), pltpu.SemaphoreType.DMA(())]),
        compiler_params=_cparams(("arbitrary",)),
        name="dispatch",
    )(eidx, rank, row_start, zero_start, nsb, hp)


def _moe_kernel(sbe_s, sbn_s, nsb_s, x_ref, win_ref, bin_ref, wout_ref, bout_ref, o_ref, winb, woutb):
    s = pl.program_id(0)
    f = pl.program_id(1)
    half = x_ref.shape[1]
    nsub = win_ref.shape[1] // (2 * LANES)

    @pl.when(s < nsb_s[0])
    def _():
        winb[...] = win_ref[...].astype(BF16)
        woutb[...] = wout_ref[...].astype(BF16)
        nb = sbn_s[s]
        lane = lax.broadcasted_iota(I32, (ROW_BLK, LANES), 1)
        even = (2 * lane) & (LANES - 1)
        odd = even + 1
        first_half = lane < LANES // 2

        def block(r, carry):
            rows = pl.ds(pl.multiple_of(r * ROW_BLK, ROW_BLK), ROW_BLK)
            xu = x_ref[rows, :]
            xa = lax.bitcast_convert_type(lax.shift_left(xu, U32(16)), F32).astype(BF16)
            xb = lax.bitcast_convert_type(xu & U32(0xFFFF0000), F32).astype(BF16)
            z = (jnp.dot(xa, winb[0:half, :], preferred_element_type=F32)
                 + jnp.dot(xb, winb[half:2 * half, :], preferred_element_type=F32) + bin_ref[...])
            hs = []
            for u in range(nsub):
                za = z[:, (2 * u) * LANES:(2 * u + 1) * LANES]
                zb = z[:, (2 * u + 1) * LANES:(2 * u + 2) * LANES]
                gate = jnp.where(first_half, jnp.take_along_axis(za, even, axis=1),
                                 jnp.take_along_axis(zb, even, axis=1))
                up = jnp.where(first_half, jnp.take_along_axis(za, odd, axis=1),
                               jnp.take_along_axis(zb, odd, axis=1))
                gate = jnp.minimum(gate, SWIGLU_LIMIT)
                up = jnp.clip(up, -SWIGLU_LIMIT, SWIGLU_LIMIT)
                glu = gate * jax.nn.sigmoid(SWIGLU_ALPHA * gate)
                hs.append(((up + 1.0) * glu).astype(BF16))
            h = jnp.concatenate(hs, axis=1)
            y = jnp.dot(h, woutb[...], preferred_element_type=F32)

            @pl.when(f == 0)
            def _():
                o_ref[rows, :] = y + bout_ref[...]

            @pl.when(f > 0)
            def _():
                o_ref[rows, :] += y

            return carry

        lax.fori_loop(0, nb, block, 0)

        @pl.when(f == 0)
        def _():
            def clear(r, carry):
                o_ref[pl.ds(pl.multiple_of(r * ROW_BLK, ROW_BLK), ROW_BLK), :] = jnp.zeros(
                    (ROW_BLK, o_ref.shape[1]), F32)
                return carry

            lax.fori_loop(nb, SUPER_BLKS, clear, 0)


def _moe_ffn(sb_e, sb_nb, nsb, xs, w_in, b_in, w_out, b_out, max_sb):
    ne, d, f2 = w_in.shape
    ff = w_out.shape[1]
    nf = ff // F_CHUNK
    half = xs.shape[1]

    def live(s, f, nsb_ref):
        ok = s < nsb_ref[0]
        return jnp.where(ok, s, nsb_ref[0] - 1), jnp.where(ok, f, nf - 1)

    def x_map(s, f, sbe, sbn, nsb_ref):
        return live(s, f, nsb_ref)[0], 0

    def win_map(s, f, sbe, sbn, nsb_ref):
        se, fe = live(s, f, nsb_ref)
        return sbe[se], 0, fe

    def wout_map(s, f, sbe, sbn, nsb_ref):
        se, fe = live(s, f, nsb_ref)
        return sbe[se], fe, 0

    def bout_map(s, f, sbe, sbn, nsb_ref):
        return sbe[live(s, f, nsb_ref)[0]], 0, 0

    return pl.pallas_call(
        _moe_kernel,
        out_shape=jax.ShapeDtypeStruct((max_sb * SUPER_ROWS, d), F32),
        grid_spec=pltpu.PrefetchScalarGridSpec(
            num_scalar_prefetch=3,
            grid=(max_sb, nf),
            in_specs=[pl.BlockSpec((SUPER_ROWS, half), x_map),
                      pl.BlockSpec((None, d, 2 * F_CHUNK), win_map),
                      pl.BlockSpec((None, 1, 2 * F_CHUNK), win_map),
                      pl.BlockSpec((None, F_CHUNK, d), wout_map),
                      pl.BlockSpec((None, 1, d), bout_map)],
            out_specs=pl.BlockSpec((SUPER_ROWS, d), x_map),
            scratch_shapes=[pltpu.VMEM((d, 2 * F_CHUNK), BF16), pltpu.VMEM((F_CHUNK, d), BF16)]),
        compiler_params=_cparams(("arbitrary", "arbitrary")),
        name="moe_ffn",
    )(sb_e, sb_nb, nsb, xs, w_in, b_in.reshape(ne, 1, f2), w_out, b_out.reshape(ne, 1, d))


def _combine_kernel(eidx_s, rank_s, rs_s, gate_ref, x1_ref, gt2_ref, g_ref, ys_hbm, o_ref, buf, sem):
    step = pl.program_id(0)
    nstep = pl.num_programs(0)
    tt = x1_ref.shape[0]

    def row_copy(tile, slot, j, k):
        a = (tile * tt + j) * TOP_K + k
        d = rs_s[eidx_s[a]] + rank_s[a]
        return pltpu.make_async_copy(ys_hbm.at[pl.ds(d, 1)], buf.at[slot, k, pl.ds(j, 1)], sem.at[slot])

    def fetch(tile, slot):
        def issue(j, c):
            for k in range(TOP_K):
                row_copy(tile, slot, j, k).start()
            return c

        lax.fori_loop(0, tt, issue, 0)

    @pl.when(step == 0)
    def _():
        fetch(0, 0)

    @pl.when(step + 1 < nstep)
    def _():
        fetch(step + 1, (step + 1) % 2)

    slot = step % 2

    def drain(j, c):
        for k in range(TOP_K):
            row_copy(step, slot, j, k).wait()
        return c

    lax.fori_loop(0, tt, drain, 0)

    g = gate_ref[...]
    y = buf[slot, 0] * g[:, 0:1]
    for k in range(1, TOP_K):
        y = y + buf[slot, k] * g[:, k:k + 1]
    o_ref[...] = x1_ref[...] + gt2_ref[...] * _rms(y, g_ref[...])


def _combine(eidx, rank, row_start, gates, x1, gt2, g_post, ys, seq):
    t, d = x1.shape
    tt = 128
    return pl.pallas_call(
        _combine_kernel,
        out_shape=jax.ShapeDtypeStruct((t, d), F32),
        grid_spec=pltpu.PrefetchScalarGridSpec(
            num_scalar_prefetch=3,
            grid=(t // tt,),
            in_specs=[pl.BlockSpec((tt, LANES), lambda i, *_: (i, 0)),
                      pl.BlockSpec((tt, d), lambda i, *_: (i, 0)),
                      pl.BlockSpec((None, 1, d), lambda i, *_: (i * tt // seq, 0, 0)),
                      pl.BlockSpec((1, d), lambda i, *_: (0, 0)),
                      pl.BlockSpec(memory_space=pl.ANY)],
            out_specs=pl.BlockSpec((tt, d), lambda i, *_: (i, 0)),
            scratch_shapes=[pltpu.VMEM((2, TOP_K, tt, d), F32), pltpu.SemaphoreType.DMA((2,))]),
        compiler_params=_cparams(("arbitrary",)),
        name="combine",
    )(eidx, rank, row_start, gates, x1, gt2, g_post.reshape(1, d), ys)


def kernel(x, c, w_ada, b_ada, g_pre_mix, g_post_mix, w_in, rpb_na, t5_table, g_out_na, g_out_dil, w_o,
           g_pre_ffn, g_post_ffn, w_router, b_router, w_e_in, b_e_in, w_e_out, b_e_out):
    bsz, seq, d = x.shape
    t = bsz * seq
    for l in range(w_ada.shape[0]):
        mod = _ada_mod(c, w_ada[l], b_ada[l])
        sh1, sc1, gt1, sh2, sc2, gt2 = (m.reshape(bsz, 1, d) for m in jnp.split(mod, 6, axis=-1))
        x2 = x.reshape(t, d)

        proj = _qkv_proj(x2, g_pre_mix[l], sc1, sh1, w_in[l].astype(BF16), seq)
        out_na = _na_attn(proj, _na_bias_table(rpb_na[l]), bsz, seq)
        aug16 = _dil_sub(proj, t5_table, DIL_PATTERNS[2][1], bsz, seq)
        aug4 = _dil_sub(proj, t5_table, DIL_PATTERNS[1][1], bsz, seq)
        out_dil = _dil_main(proj, t5_table, aug4, aug16, bsz, seq)

        ne = w_router.shape[-1]
        wr = jnp.zeros((d, LANES), BF16).at[:, :ne].set(w_router[l].astype(BF16))
        br = jnp.full((1, LANES), NEG, F32).at[0, :ne].set(b_router[l])
        x1, hp, logits = _mix_out(out_na, out_dil, g_out_na[l], g_out_dil[l], w_o[l].astype(BF16), x2,
                                  gt1, sc2, sh2, g_post_mix[l], g_pre_ffn[l], wr, br, seq)

        idx, rank, gates, cnt = _route(logits)
        eidx = idx[:, :TOP_K].reshape(-1)
        rank = rank[:, :TOP_K].reshape(-1)
        max_sb, row_start, nsb, sb_e, sb_nb, zero_start = _schedule(cnt[0, :ne], t * TOP_K)
        xs = _dispatch(eidx, rank, row_start, zero_start, nsb, hp, max_sb * SUPER_ROWS)
        ys = _moe_ffn(sb_e, sb_nb, nsb, xs, w_e_in[l], b_e_in[l], w_e_out[l], b_e_out[l], max_sb)
        x = _combine(eidx, rank, row_start, gates, x1, gt2, g_post_ffn[l], ys, seq).reshape(bsz, seq, d)
    return x
```

```python
import functools

import numpy as np
import jax
import jax.numpy as jnp
from jax import lax
from jax.experimental import pallas as pl
from jax.experimental.pallas import tpu as pltpu

F32 = jnp.float32
BF16 = jnp.bfloat16
U32 = jnp.uint32
I32 = jnp.int32

HEAD_DIM = 128
N_HEADS_NA = 8
N_HEADS_DIL = 8
GRID_W = 64
NA_ROWS = 8
NA_COLS = 16
DIL_PATTERNS = ((128, 1), (512, 4), (2048, 16))
T5_BUCKETS = 32
T5_MAX_DIST = 1024
N_EXPERTS = 32
TOP_K = 4
SWIGLU_LIMIT = 7.0
SWIGLU_ALPHA = 1.702
EPS = 1e-6
NEG = -1e30
SCALE = HEAD_DIM ** -0.5

LANES = 128
QBLK = 128
ROW_BLK = 256
SUPER_BLKS = 5
SUPER_ROWS = ROW_BLK * SUPER_BLKS
F_CHUNK = 256
VMEM_LIMIT = 56 * 1024 * 1024


def _cparams(sem, vmem=VMEM_LIMIT):
    return pltpu.CompilerParams(dimension_semantics=sem, vmem_limit_bytes=vmem)


def _rms(x, g):
    return x * lax.rsqrt(jnp.mean(x * x, axis=-1, keepdims=True) + EPS) * g


def _ada_kernel(c_ref, w_ref, b_ref, o_ref):
    c = c_ref[...]
    s = c * jax.nn.sigmoid(c)
    o_ref[...] = jnp.dot(s.astype(BF16), w_ref[...].astype(BF16),
                         preferred_element_type=F32) + b_ref[...]


def _ada_mod(c, w, b):
    bsz, d = c.shape
    n = w.shape[1]
    tn = 1024
    cp = jnp.zeros((8, d), F32).at[:bsz].set(c)
    out = pl.pallas_call(
        _ada_kernel,
        out_shape=jax.ShapeDtypeStruct((8, n), F32),
        grid=(n // tn,),
        in_specs=[pl.BlockSpec((8, d), lambda j: (0, 0)),
                  pl.BlockSpec((d, tn), lambda j: (0, j)),
                  pl.BlockSpec((1, tn), lambda j: (0, j))],
        out_specs=pl.BlockSpec((8, tn), lambda j: (0, j)),
        compiler_params=_cparams(("arbitrary",)),
        name="ada_mod",
    )(cp, w, b.reshape(1, n))
    return out[:bsz]


def _qkv_kernel(x_ref, g_ref, sc_ref, sh_ref, w_ref, o_ref, h_scr):
    @pl.when(pl.program_id(1) == 0)
    def _():
        h = _rms(x_ref[...], g_ref[...]) * (1.0 + sc_ref[...]) + sh_ref[...]
        h_scr[...] = h.astype(BF16)

    acc = jnp.dot(h_scr[...], w_ref[...], preferred_element_type=F32)
    for u in range(o_ref.shape[0]):
        o_ref[u] = acc[:, u * LANES:(u + 1) * LANES].astype(BF16)


def _qkv_proj(x2, g, sc, sh, w_bf, seq):
    t, d = x2.shape
    n = w_bf.shape[1]
    tm, tn = 1024, 1024
    return pl.pallas_call(
        _qkv_kernel,
        out_shape=jax.ShapeDtypeStruct((n // LANES, t, LANES), BF16),
        grid=(t // tm, n // tn),
        in_specs=[pl.BlockSpec((tm, d), lambda i, j: (i, 0)),
                  pl.BlockSpec((1, d), lambda i, j: (0, 0)),
                  pl.BlockSpec((None, 1, d), lambda i, j: (i * tm // seq, 0, 0)),
                  pl.BlockSpec((None, 1, d), lambda i, j: (i * tm // seq, 0, 0)),
                  pl.BlockSpec((d, tn), lambda i, j: (0, j))],
        out_specs=pl.BlockSpec((tn // LANES, tm, LANES), lambda i, j: (j, i, 0)),
        scratch_shapes=[pltpu.VMEM((tm, d), BF16)],
        compiler_params=_cparams(("arbitrary", "arbitrary")),
        name="qkv_proj",
    )(x2, g.reshape(1, d), sc, sh, w_bf)


def _toeplitz(vec, rows, cols):
    n = rows + cols - 1
    assert vec.shape[-1] == n
    lead = vec.shape[:-1]
    ext = jnp.concatenate([vec, jnp.zeros(lead + (1,), vec.dtype)], axis=-1)
    flat = jnp.broadcast_to(ext[..., None, :], lead + (rows, n + 1)).reshape(lead + (rows * (n + 1),))
    skew = flat[..., :rows * n].reshape(lead + (rows, n))
    return skew[..., rows - 1:rows - 1 + cols]


def _na_bias_table(rpb):
    cidx = np.arange(GRID_W)
    col_start = np.clip(cidx - NA_COLS // 2, 0, GRID_W - NA_COLS)
    col_ok = (cidx[None, :] >= col_start[:, None]) & (cidx[None, :] < col_start[:, None] + NA_COLS)
    pad = GRID_W - NA_COLS
    vec = jnp.pad(rpb.astype(F32), ((0, 0), (0, 0), (pad, pad)))
    tab = jnp.where(col_ok, _toeplitz(vec, GRID_W, GRID_W), NEG)
    nh = rpb.shape[0]
    per_var = [tab[:, v:v + NA_ROWS].transpose(0, 2, 1, 3).reshape(nh, GRID_W, NA_ROWS * GRID_W)
               for v in range(NA_ROWS)]
    return jnp.stack(per_var, axis=1)


def _na_kernel(q_ref, k_ref, v_ref, bias_ref, o_ref, *, rows):
    nkeys = NA_ROWS * GRID_W

    def body(r, carry):
        rs = jnp.clip(r - NA_ROWS // 2, 0, rows - NA_ROWS)
        var = rs - r + (NA_ROWS - 1)
        q0 = pl.multiple_of(r * GRID_W, GRID_W)
        k0 = pl.multiple_of(rs * GRID_W, GRID_W)
        q = q_ref[pl.ds(q0, GRID_W), :]
        k = k_ref[pl.ds(k0, nkeys), :]
        v = v_ref[pl.ds(k0, nkeys), :]
        s = lax.dot_general(q, k, (((1,), (1,)), ((), ())), preferred_element_type=F32) * SCALE
        s = s + bias_ref[var]
        m = jnp.max(s, axis=-1, keepdims=True)
        p = jnp.exp(s - m)
        l = jnp.sum(p, axis=-1, keepdims=True)
        o = jnp.dot(p.astype(BF16), v, preferred_element_type=F32)
        o_ref[pl.ds(q0, GRID_W), :] = o / l
        return carry

    lax.fori_loop(0, rows, body, 0)


def _na_attn(proj, bias, bsz, seq):
    nh = N_HEADS_NA
    t = bsz * seq
    blk = lambda off: pl.BlockSpec((None, seq, LANES), lambda h, b: (h + off, b, 0))
    return pl.pallas_call(
        functools.partial(_na_kernel, rows=seq // GRID_W),
        out_shape=jax.ShapeDtypeStruct((nh, t, LANES), F32),
        grid=(nh, bsz),
        in_specs=[blk(0), blk(nh), blk(2 * nh),
                  pl.BlockSpec((None, NA_ROWS, GRID_W, NA_ROWS * GRID_W), lambda h, b: (h, 0, 0, 0))],
        out_specs=pl.BlockSpec((None, seq, LANES), lambda h, b: (h, b, 0)),
        compiler_params=_cparams(("arbitrary", "arbitrary")),
        name="na_attn",
    )(proj, proj, proj, bias)


def _t5_bucket(rel):
    nb = T5_BUCKETS // 2
    max_exact = nb // 2
    ret = (rel > 0).astype(np.int32) * nb
    n = np.abs(rel)
    large = max_exact + (np.log(np.maximum(n, 1) / max_exact) / np.log(T5_MAX_DIST / max_exact)
                         * (nb - max_exact)).astype(np.int32)
    large = np.minimum(large, nb - 1)
    return (ret + np.where(n < max_exact, n, large)).astype(np.int32)


def _dil_geometry(sub_len):
    half = DIL_PATTERNS[0][0] // 2
    width = min(sub_len, QBLK + 2 * half)
    nblk = sub_len // QBLK
    starts = [min(max(QBLK * n - half, 0), sub_len - width) for n in range(nblk)]
    offs = sorted({ws - QBLK * n for n, ws in enumerate(starts)}, reverse=True)
    var = [offs.index(ws - QBLK * n) for n, ws in enumerate(starts)]
    return width, starts, offs, var


def _dil_bias_table(t5_table, dil, width, offs):
    half = DIL_PATTERNS[0][0] // 2
    tabs = []
    for off in offs:
        delta = np.arange(QBLK + width - 1) - (QBLK - 1) + off
        onehot = np.eye(T5_BUCKETS, dtype=np.float32)[_t5_bucket(delta * dil)]
        vals = jnp.dot(jnp.asarray(onehot), t5_table.astype(F32), precision=lax.Precision.HIGHEST)
        vec = jnp.where((np.abs(delta) <= half)[:, None], vals, NEG).T
        tabs.append(_toeplitz(vec, QBLK, width))
    return jnp.stack(tabs, axis=1)


def _window_attn(q, k, v, bias):
    s = lax.dot_general(q, k, (((1,), (1,)), ((), ())), preferred_element_type=F32) * SCALE + bias
    m = jnp.max(s, axis=-1, keepdims=True)
    p = jnp.exp(s - m)
    l = jnp.sum(p, axis=-1, keepdims=True)
    o = jnp.dot(p.astype(BF16), v, preferred_element_type=F32) / l
    return o, m + jnp.log(l)


def _dil_sub_kernel(q_ref, k_ref, v_ref, bias_ref, o_ref, *, width, starts, var):
    for n, ws in enumerate(starts):
        rows = slice(QBLK * n, QBLK * (n + 1))
        o, lse = _window_attn(q_ref[rows, :], k_ref[ws:ws + width, :], v_ref[ws:ws + width, :],
                              bias_ref[var[n]])
        o_ref[rows, 0:LANES] = o
        o_ref[rows, LANES:2 * LANES] = jnp.broadcast_to(lse, (QBLK, LANES))


def _dil_sub(proj, t5_table, dil, bsz, seq):
    nh = N_HEADS_DIL
    sub_len = seq // dil
    width, starts, offs, var = _dil_geometry(sub_len)
    bias = _dil_bias_table(t5_table, dil, width, offs)
    view = proj.reshape(proj.shape[0], bsz, sub_len, dil * LANES)
    first = 3 * N_HEADS_NA
    blk = lambda off: pl.BlockSpec((None, None, sub_len, LANES), lambda h, b, r: (first + off + h, b, 0, r))
    out = pl.pallas_call(
        functools.partial(_dil_sub_kernel, width=width, starts=starts, var=var),
        out_shape=jax.ShapeDtypeStruct((nh, bsz, sub_len, dil * 2 * LANES), F32),
        grid=(nh, bsz, dil),
        in_specs=[blk(0), blk(nh), blk(2 * nh),
                  pl.BlockSpec((None, len(offs), QBLK, width), lambda h, b, r: (h, 0, 0, 0))],
        out_specs=pl.BlockSpec((None, None, sub_len, 2 * LANES), lambda h, b, r: (h, b, 0, r)),
        compiler_params=_cparams(("arbitrary",) * 3),
        name=f"dil_sub{dil}",
    )(view, view, view, bias)
    return out.reshape(nh, bsz * seq, 2 * LANES)


def _dil_main_kernel(q_ref, k_ref, v_ref, bias_ref, a4_ref, a16_ref, o_ref, *, seq, width, nblk, half):
    def body(n, carry):
        q0 = pl.multiple_of(n * QBLK, QBLK)
        ws = pl.multiple_of(jnp.clip(n * QBLK - half, 0, seq - width), half)
        var = jnp.where(n == 0, 0, jnp.where(n == nblk - 1, 2, 1))
        o1, lse1 = _window_attn(q_ref[pl.ds(q0, QBLK), :], k_ref[pl.ds(ws, width), :],
                                v_ref[pl.ds(ws, width), :], bias_ref[var])
        a4 = a4_ref[pl.ds(q0, QBLK), :]
        a16 = a16_ref[pl.ds(q0, QBLK), :]
        o4, lse4 = a4[:, :LANES], a4[:, LANES:]
        o16, lse16 = a16[:, :LANES], a16[:, LANES:]
        mx = jnp.maximum(jnp.maximum(lse4, lse16), lse1)
        e1 = jnp.exp(lse1 - mx)
        e4 = jnp.exp(lse4 - mx)
        e16 = jnp.exp(lse16 - mx)
        o_ref[pl.ds(q0, QBLK), :] = (e1 * o1 + e4 * o4 + e16 * o16) / (e1 + e4 + e16)
        return carry

    lax.fori_loop(0, nblk, body, 0)


def _dil_main(proj, t5_table, aug4, aug16, bsz, seq):
    nh = N_HEADS_DIL
    width, starts, offs, var = _dil_geometry(seq)
    assert offs == [0, -(DIL_PATTERNS[0][0] // 2), -DIL_PATTERNS[0][0]] and var[0] == 0 and var[-1] == 2
    bias = _dil_bias_table(t5_table, 1, width, offs)
    first = 3 * N_HEADS_NA
    blk = lambda off: pl.BlockSpec((None, seq, LANES), lambda h, b: (first + off + h, b, 0))
    aug = pl.BlockSpec((None, seq, 2 * LANES), lambda h, b: (h, b, 0))
    return pl.pallas_call(
        functools.partial(_dil_main_kernel, seq=seq, width=width, nblk=len(starts),
                          half=DIL_PATTERNS[0][0] // 2),
        out_shape=jax.ShapeDtypeStruct((nh, bsz * seq, LANES), F32),
        grid=(nh, bsz),
        in_specs=[blk(0), blk(nh), blk(2 * nh),
                  pl.BlockSpec((None, len(offs), QBLK, width), lambda h, b: (h, 0, 0, 0)),
                  aug, aug],
        out_specs=pl.BlockSpec((None, seq, LANES), lambda h, b: (h, b, 0)),
        compiler_params=_cparams(("arbitrary", "arbitrary")),
        name="dil_main",
    )(proj, proj, proj, bias, aug4, aug16)


def _mix_kernel(na_ref, dl_ref, gna_ref, gdl_ref, wo_ref, x_ref, gt1_ref, sc2_ref, sh2_ref,
                gpost_ref, gpre_ref, wr_ref, br_ref, x1_ref, hp_ref, lg_ref):
    nh = na_ref.shape[0]
    na = jnp.concatenate([na_ref[h] for h in range(nh)], axis=1)
    dl = jnp.concatenate([dl_ref[h] for h in range(nh)], axis=1)
    lhs = jnp.concatenate([_rms(na, gna_ref[...]), _rms(dl, gdl_ref[...])], axis=1).astype(BF16)
    mixed = jnp.dot(lhs, wo_ref[...], preferred_element_type=F32)
    x1 = x_ref[...] + gt1_ref[...] * _rms(mixed, gpost_ref[...])
    x1_ref[...] = x1
    hf = _rms(x1, gpre_ref[...]) * (1.0 + sc2_ref[...]) + sh2_ref[...]
    hb = hf.astype(BF16)
    lg_ref[...] = jnp.dot(hb, wr_ref[...], preferred_element_type=F32) + br_ref[...]
    half = hb.shape[1] // 2
    lo = lax.bitcast_convert_type(hb[:, :half].astype(F32), U32)
    hi = lax.bitcast_convert_type(hb[:, half:].astype(F32), U32)
    hp_ref[...] = (hi & U32(0xFFFF0000)) | lax.shift_right_logical(lo, U32(16))


def _mix_out(out_na, out_dil, g_na, g_dil, wo_bf, x2, gt1, sc2, sh2, g_post, g_pre, wr_bf, br, seq):
    t, d = x2.shape
    nh = out_na.shape[0]
    tm = 256
    wna = g_na.shape[-1]
    row = lambda n: pl.BlockSpec((1, n), lambda i: (0, 0))
    per_b = pl.BlockSpec((None, 1, d), lambda i: (i * tm // seq, 0, 0))
    heads = pl.BlockSpec((nh, tm, LANES), lambda i: (0, i, 0))
    return pl.pallas_call(
        _mix_kernel,
        out_shape=(jax.ShapeDtypeStruct((t, d), F32),
                   jax.ShapeDtypeStruct((t, d // 2), U32),
                   jax.ShapeDtypeStruct((t, LANES), F32)),
        grid=(t // tm,),
        in_specs=[heads, heads, row(wna), row(wna),
                  pl.BlockSpec((d, d), lambda i: (0, 0)),
                  pl.BlockSpec((tm, d), lambda i: (i, 0)),
                  per_b, per_b, per_b, row(d), row(d),
                  pl.BlockSpec((d, LANES), lambda i: (0, 0)), row(LANES)],
        out_specs=(pl.BlockSpec((tm, d), lambda i: (i, 0)),
                   pl.BlockSpec((tm, d // 2), lambda i: (i, 0)),
                   pl.BlockSpec((tm, LANES), lambda i: (i, 0))),
        compiler_params=_cparams(("arbitrary",)),
        name="mix_out",
    )(out_na, out_dil, g_na.reshape(1, wna), g_dil.reshape(1, wna), wo_bf, x2, gt1, sc2, sh2,
      g_post.reshape(1, d), g_pre.reshape(1, d), wr_bf, br)


def _route_kernel(lg_ref, idx_ref, rank_ref, gate_ref, cnt_ref, carry):
    step = pl.program_id(0)

    @pl.when(step == 0)
    def _():
        carry[...] = jnp.zeros_like(carry)

    v = lg_ref[...]
    ch = v.shape[0]
    lane = lax.broadcasted_iota(I32, v.shape, 1).astype(F32)
    vals, idxs = [], []
    for _ in range(TOP_K):
        m = jnp.max(v, axis=1, keepdims=True)
        ik = jnp.min(jnp.where(v == m, lane, float(LANES)), axis=1, keepdims=True)
        vals.append(m)
        idxs.append(ik)
        v = jnp.where(lane == ik, -jnp.inf, v)
    es = [jnp.exp(val - vals[0]) for val in vals]
    den = es[0] + es[1] + es[2] + es[3]
    sel = jnp.zeros(v.shape, F32)
    for ik in idxs:
        sel = jnp.where(lane == ik, 1.0, sel)
    ti = lax.broadcasted_iota(I32, (ch, ch), 0)
    tj = lax.broadcasted_iota(I32, (ch, ch), 1)
    lower = jnp.where(tj < ti, 1.0, 0.0).astype(BF16)
    cum = jnp.dot(lower, sel.astype(BF16), preferred_element_type=F32) + carry[0:1, :]
    idx_o = jnp.zeros(v.shape, F32)
    rank_o = jnp.zeros(v.shape, F32)
    gate_o = jnp.zeros(v.shape, F32)
    for k in range(TOP_K):
        rk = jnp.sum(jnp.where(lane == idxs[k], cum, 0.0), axis=1, keepdims=True)
        idx_o = jnp.where(lane == float(k), idxs[k], idx_o)
        rank_o = jnp.where(lane == float(k), rk, rank_o)
        gate_o = jnp.where(lane == float(k), es[k] / den, gate_o)
    idx_ref[...] = idx_o.astype(I32)
    rank_ref[...] = rank_o.astype(I32)
    gate_ref[...] = gate_o
    total = carry[...] + jnp.sum(sel, axis=0, keepdims=True)
    carry[...] = total
    cnt_ref[...] = total.astype(I32)


def _route(logits):
    t = logits.shape[0]
    ch = 512
    blk = pl.BlockSpec((ch, LANES), lambda i: (i, 0))
    return pl.pallas_call(
        _route_kernel,
        out_shape=(jax.ShapeDtypeStruct((t, LANES), I32),
                   jax.ShapeDtypeStruct((t, LANES), I32),
                   jax.ShapeDtypeStruct((t, LANES), F32),
                   jax.ShapeDtypeStruct((8, LANES), I32)),
        grid=(t // ch,),
        in_specs=[blk],
        out_specs=(blk, blk, blk, pl.BlockSpec((8, LANES), lambda i: (0, 0))),
        scratch_shapes=[pltpu.VMEM((8, LANES), F32)],
        compiler_params=_cparams(("arbitrary",)),
        name="route",
    )(logits)


def _schedule(counts, n_assign):
    max_sb = n_assign // SUPER_ROWS + N_EXPERTS
    nsb_e = (counts + SUPER_ROWS - 1) // SUPER_ROWS
    sb_end = jnp.cumsum(nsb_e)
    sb_start = sb_end - nsb_e
    row_start = (sb_start * SUPER_ROWS).astype(I32)
    nsb = sb_end[-1].astype(I32)
    s = jnp.arange(max_sb, dtype=I32)
    sb_e = jnp.minimum(jnp.searchsorted(sb_end, s, side="right"), N_EXPERTS - 1).astype(I32)
    rem = counts[sb_e] - (s - sb_start[sb_e]) * SUPER_ROWS
    sb_nb = jnp.where(s < nsb, (jnp.clip(rem, 0, SUPER_ROWS) + ROW_BLK - 1) // ROW_BLK, 0).astype(I32)
    zero_start = (s * SUPER_ROWS + jnp.maximum(sb_nb - 1, 0) * ROW_BLK).astype(I32)
    return max_sb, row_start, nsb.reshape(1), sb_e, sb_nb, zero_start


def _dispatch_kernel(eidx_s, rank_s, rs_s, zs_s, nsb_s, hp_ref, xs_hbm, zbuf, zsem, sem):
    step = pl.program_id(0)
    tt = hp_ref.shape[0]

    def zero_copy(s):
        return pltpu.make_async_copy(zbuf, xs_hbm.at[pl.ds(pl.multiple_of(zs_s[s], ROW_BLK), ROW_BLK)], zsem)

    @pl.when(step == 0)
    def _():
        zbuf[...] = jnp.zeros_like(zbuf)

        def zstart(s, c):
            zero_copy(s).start()
            return c

        def zwait(s, c):
            zero_copy(s).wait()
            return c

        lax.fori_loop(0, nsb_s[0], zstart, 0)
        lax.fori_loop(0, nsb_s[0], zwait, 0)

    base = step * tt

    def row_copy(j, k):
        a = (base + j) * TOP_K + k
        d = rs_s[eidx_s[a]] + rank_s[a]
        return pltpu.make_async_copy(hp_ref.at[pl.ds(j, 1)], xs_hbm.at[pl.ds(d, 1)], sem)

    def issue(j, c):
        for k in range(TOP_K):
            row_copy(j, k).start()
        return c

    def drain(j, c):
        for k in range(TOP_K):
            row_copy(j, k).wait()
        return c

    lax.fori_loop(0, tt, issue, 0)
    lax.fori_loop(0, tt, drain, 0)


def _dispatch(eidx, rank, row_start, zero_start, nsb, hp, n_rows):
    t, w = hp.shape
    tt = 256
    return pl.pallas_call(
        _dispatch_kernel,
        out_shape=jax.ShapeDtypeStruct((n_rows, w), U32),
        grid_spec=pltpu.PrefetchScalarGridSpec(
            num_scalar_prefetch=5,
            grid=(t // tt,),
            in_specs=[pl.BlockSpec((tt, w), lambda i, *_: (i, 0))],
            out_specs=pl.BlockSpec(memory_space=pl.ANY),
            scratch_shapes=[pltpu.VMEM((ROW_BLK, w), U32),
                            pltpu.SemaphoreType.DMA(()), pltpu.SemaphoreType.DMA(())]),
        compiler_params=_cparams(("arbitrary",)),
        name="dispatch",
    )(eidx, rank, row_start, zero_start, nsb, hp)


def _moe_kernel(sbe_s, sbn_s, nsb_s, x_ref, win_ref, bin_ref, wout_ref, bout_ref, o_ref, winb, woutb):
    s = pl.program_id(0)
    f = pl.program_id(1)
    half = x_ref.shape[1]
    nsub = win_ref.shape[1] // (2 * LANES)

    @pl.when(s < nsb_s[0])
    def _():
        winb[...] = win_ref[...].astype(BF16)
        woutb[...] = wout_ref[...].astype(BF16)
        nb = sbn_s[s]
        lane = lax.broadcasted_iota(I32, (ROW_BLK, LANES), 1)
        even = (2 * lane) & (LANES - 1)
        odd = even + 1
        first_half = lane < LANES // 2

        def block(r, carry):
            rows = pl.ds(pl.multiple_of(r * ROW_BLK, ROW_BLK), ROW_BLK)
            xu = x_ref[rows, :]
            xa = lax.bitcast_convert_type(lax.shift_left(xu, U32(16)), F32).astype(BF16)
            xb = lax.bitcast_convert_type(xu & U32(0xFFFF0000), F32).astype(BF16)
            z = (jnp.dot(xa, winb[0:half, :], preferred_element_type=F32)
                 + jnp.dot(xb, winb[half:2 * half, :], preferred_element_type=F32) + bin_ref[...])
            hs = []
            for u in range(nsub):
                za = z[:, (2 * u) * LANES:(2 * u + 1) * LANES]
                zb = z[:, (2 * u + 1) * LANES:(2 * u + 2) * LANES]
                gate = jnp.where(first_half, jnp.take_along_axis(za, even, axis=1),
                                 jnp.take_along_axis(zb, even, axis=1))
                up = jnp.where(first_half, jnp.take_along_axis(za, odd, axis=1),
                               jnp.take_along_axis(zb, odd, axis=1))
                gate = jnp.minimum(gate, SWIGLU_LIMIT)
                up = jnp.clip(up, -SWIGLU_LIMIT, SWIGLU_LIMIT)
                glu = gate * jax.nn.sigmoid(SWIGLU_ALPHA * gate)
                hs.append(((up + 1.0) * glu).astype(BF16))
            h = jnp.concatenate(hs, axis=1)
            y = jnp.dot(h, woutb[...], preferred_element_type=F32)

            @pl.when(f == 0)
            def _():
                o_ref[rows, :] = y + bout_ref[...]

            @pl.when(f > 0)
            def _():
                o_ref[rows, :] += y

            return carry

        lax.fori_loop(0, nb, block, 0)

        @pl.when(f == 0)
        def _():
            def clear(r, carry):
                o_ref[pl.ds(pl.multiple_of(r * ROW_BLK, ROW_BLK), ROW_BLK), :] = jnp.zeros(
                    (ROW_BLK, o_ref.shape[1]), F32)
                return carry

            lax.fori_loop(nb, SUPER_BLKS, clear, 0)


def _moe_ffn(sb_e, sb_nb, nsb, xs, w_in, b_in, w_out, b_out, max_sb):
    ne, d, f2 = w_in.shape
    ff = w_out.shape[1]
    nf = ff // F_CHUNK
    half = xs.shape[1]

    def live(s, f, nsb_ref):
        ok = s < nsb_ref[0]
        return jnp.where(ok, s, nsb_ref[0] - 1), jnp.where(ok, f, nf - 1)

    def x_map(s, f, sbe, sbn, nsb_ref):
        return live(s, f, nsb_ref)[0], 0

    def win_map(s, f, sbe, sbn, nsb_ref):
        se, fe = live(s, f, nsb_ref)
        return sbe[se], 0, fe

    def wout_map(s, f, sbe, sbn, nsb_ref):
        se, fe = live(s, f, nsb_ref)
        return sbe[se], fe, 0

    def bout_map(s, f, sbe, sbn, nsb_ref):
        return sbe[live(s, f, nsb_ref)[0]], 0, 0

    return pl.pallas_call(
        _moe_kernel,
        out_shape=jax.ShapeDtypeStruct((max_sb * SUPER_ROWS, d), F32),
        grid_spec=pltpu.PrefetchScalarGridSpec(
            num_scalar_prefetch=3,
            grid=(max_sb, nf),
            in_specs=[pl.BlockSpec((SUPER_ROWS, half), x_map),
                      pl.BlockSpec((None, d, 2 * F_CHUNK), win_map),
                      pl.BlockSpec((None, 1, 2 * F_CHUNK), win_map),
                      pl.BlockSpec((None, F_CHUNK, d), wout_map),
                      pl.BlockSpec((None, 1, d), bout_map)],
            out_specs=pl.BlockSpec((SUPER_ROWS, d), x_map),
            scratch_shapes=[pltpu.VMEM((d, 2 * F_CHUNK), BF16), pltpu.VMEM((F_CHUNK, d), BF16)]),
        compiler_params=_cparams(("arbitrary", "arbitrary")),
        name="moe_ffn",
    )(sb_e, sb_nb, nsb, xs, w_in, b_in.reshape(ne, 1, f2), w_out, b_out.reshape(ne, 1, d))


def _combine_kernel(eidx_s, rank_s, rs_s, gate_ref, x1_ref, gt2_ref, g_ref, ys_hbm, o_ref, buf, sem):
    step = pl.program_id(0)
    nstep = pl.num_programs(0)
    tt = x1_ref.shape[0]

    def row_copy(tile, slot, j, k):
        a = (tile * tt + j) * TOP_K + k
        d = rs_s[eidx_s[a]] + rank_s[a]
        return pltpu.make_async_copy(ys_hbm.at[pl.ds(d, 1)], buf.at[slot, k, pl.ds(j, 1)], sem.at[slot])

    def fetch(tile, slot):
        def issue(j, c):
            for k in range(TOP_K):
                row_copy(tile, slot, j, k).start()
            return c

        lax.fori_loop(0, tt, issue, 0)

    @pl.when(step == 0)
    def _():
        fetch(0, 0)

    @pl.when(step + 1 < nstep)
    def _():
        fetch(step + 1, (step + 1) % 2)

    slot = step % 2

    def drain(j, c):
        for k in range(TOP_K):
            row_copy(step, slot, j, k).wait()
        return c

    lax.fori_loop(0, tt, drain, 0)

    g = gate_ref[...]
    y = buf[slot, 0] * g[:, 0:1]
    for k in range(1, TOP_K):
        y = y + buf[slot, k] * g[:, k:k + 1]
    o_ref[...] = x1_ref[...] + gt2_ref[...] * _rms(y, g_ref[...])


def _combine(eidx, rank, row_start, gates, x1, gt2, g_post, ys, seq):
    t, d = x1.shape
    tt = 128
    return pl.pallas_call(
        _combine_kernel,
        out_shape=jax.ShapeDtypeStruct((t, d), F32),
        grid_spec=pltpu.PrefetchScalarGridSpec(
            num_scalar_prefetch=3,
            grid=(t // tt,),
            in_specs=[pl.BlockSpec((tt, LANES), lambda i, *_: (i, 0)),
                      pl.BlockSpec((tt, d), lambda i, *_: (i, 0)),
                      pl.BlockSpec((None, 1, d), lambda i, *_: (i * tt // seq, 0, 0)),
                      pl.BlockSpec((1, d), lambda i, *_: (0, 0)),
                      pl.BlockSpec(memory_space=pl.ANY)],
            out_specs=pl.BlockSpec((tt, d), lambda i, *_: (i, 0)),
            scratch_shapes=[pltpu.VMEM((2, TOP_K, tt, d), F32), pltpu.SemaphoreType.DMA((2,))]),
        compiler_params=_cparams(("arbitrary",)),
        name="combine",
    )(eidx, rank, row_start, gates, x1, gt2, g_post.reshape(1, d), ys)


def kernel(x, c, w_ada, b_ada, g_pre_mix, g_post_mix, w_in, rpb_na, t5_table, g_out_na, g_out_dil, w_o,
           g_pre_ffn, g_post_ffn, w_router, b_router, w_e_in, b_e_in, w_e_out, b_e_out):
    bsz, seq, d = x.shape
    t = bsz * seq
    for l in range(w_ada.shape[0]):
        mod = _ada_mod(c, w_ada[l], b_ada[l])
        sh1, sc1, gt1, sh2, sc2, gt2 = (m.reshape(bsz, 1, d) for m in jnp.split(mod, 6, axis=-1))
        x2 = x.reshape(t, d)

        proj = _qkv_proj(x2, g_pre_mix[l], sc1, sh1, w_in[l].astype(BF16), seq)
        out_na = _na_attn(proj, _na_bias_table(rpb_na[l]), bsz, seq)
        aug16 = _dil_sub(proj, t5_table, DIL_PATTERNS[2][1], bsz, seq)
        aug4 = _dil_sub(proj, t5_table, DIL_PATTERNS[1][1], bsz, seq)
        out_dil = _dil_main(proj, t5_table, aug4, aug16, bsz, seq)

        ne = w_router.shape[-1]
        wr = jnp.zeros((d, LANES), BF16).at[:, :ne].set(w_router[l].astype(BF16))
        br = jnp.full((1, LANES), NEG, F32).at[0, :ne].set(b_router[l])
        x1, hp, logits = _mix_out(out_na, out_dil, g_out_na[l], g_out_dil[l], w_o[l].astype(BF16), x2,
                                  gt1, sc2, sh2, g_post_mix[l], g_pre_ffn[l], wr, br, seq)

        idx, rank, gates, cnt = _route(logits)
        eidx = idx[:, :TOP_K].reshape(-1)
        rank = rank[:, :TOP_K].reshape(-1)
        max_sb, row_start, nsb, sb_e, sb_nb, zero_start = _schedule(cnt[0, :ne], t * TOP_K)
        xs = _dispatch(eidx, rank, row_start, zero_start, nsb, hp, max_sb * SUPER_ROWS)
        ys = _moe_ffn(sb_e, sb_nb, nsb, xs, w_e_in[l], b_e_in[l], w_e_out[l], b_e_out[l], max_sb)
        x = _combine(eidx, rank, row_start, gates, x1, gt2, g_post_ffn[l], ys, seq).reshape(bsz, seq, d)
    return x
```

```python
import functools

import numpy as np
import jax
import jax.numpy as jnp
from jax import lax
from jax.experimental import pallas as pl
from jax.experimental.pallas import tpu as pltpu

F32 = jnp.float32
BF16 = jnp.bfloat16
U32 = jnp.uint32
I32 = jnp.int32

HEAD_DIM = 128
N_HEADS_NA = 8
N_HEADS_DIL = 8
GRID_W = 64
NA_ROWS = 8
NA_COLS = 16
DIL_PATTERNS = ((128, 1), (512, 4), (2048, 16))
T5_BUCKETS = 32
T5_MAX_DIST = 1024
N_EXPERTS = 32
TOP_K = 4
SWIGLU_LIMIT = 7.0
SWIGLU_ALPHA = 1.702
EPS = 1e-6
NEG = -1e30
SCALE = HEAD_DIM ** -0.5

LANES = 128
QBLK = 128
ROW_BLK = 256
SUPER_BLKS = 5
SUPER_ROWS = ROW_BLK * SUPER_BLKS
F_CHUNK = 256
VMEM_LIMIT = 56 * 1024 * 1024


def _cparams(sem, vmem=VMEM_LIMIT):
    return pltpu.CompilerParams(dimension_semantics=sem, vmem_limit_bytes=vmem)


def _rms(x, g):
    return x * lax.rsqrt(jnp.mean(x * x, axis=-1, keepdims=True) + EPS) * g


def _ada_kernel(c_ref, w_ref, b_ref, o_ref):
    c = c_ref[...]
    s = c * jax.nn.sigmoid(c)
    o_ref[...] = jnp.dot(s.astype(BF16), w_ref[...].astype(BF16),
                         preferred_element_type=F32) + b_ref[...]


def _ada_mod(c, w, b):
    bsz, d = c.shape
    n = w.shape[1]
    tn = 1024
    cp = jnp.zeros((8, d), F32).at[:bsz].set(c)
    out = pl.pallas_call(
        _ada_kernel,
        out_shape=jax.ShapeDtypeStruct((8, n), F32),
        grid=(n // tn,),
        in_specs=[pl.BlockSpec((8, d), lambda j: (0, 0)),
                  pl.BlockSpec((d, tn), lambda j: (0, j)),
                  pl.BlockSpec((1, tn), lambda j: (0, j))],
        out_specs=pl.BlockSpec((8, tn), lambda j: (0, j)),
        compiler_params=_cparams(("arbitrary",)),
        name="ada_mod",
    )(cp, w, b.reshape(1, n))
    return out[:bsz]


def _qkv_kernel(x_ref, g_ref, sc_ref, sh_ref, w_ref, o_ref, h_scr):
    @pl.when(pl.program_id(1) == 0)
    def _():
        h = _rms(x_ref[...], g_ref[...]) * (1.0 + sc_ref[...]) + sh_ref[...]
        h_scr[...] = h.astype(BF16)

    acc = jnp.dot(h_scr[...], w_ref[...], preferred_element_type=F32)
    for u in range(o_ref.shape[0]):
        o_ref[u] = acc[:, u * LANES:(u + 1) * LANES].astype(BF16)


def _qkv_proj(x2, g, sc, sh, w_bf, seq):
    t, d = x2.shape
    n = w_bf.shape[1]
    tm, tn = 1024, 1024
    return pl.pallas_call(
        _qkv_kernel,
        out_shape=jax.ShapeDtypeStruct((n // LANES, t, LANES), BF16),
        grid=(t // tm, n // tn),
        in_specs=[pl.BlockSpec((tm, d), lambda i, j: (i, 0)),
                  pl.BlockSpec((1, d), lambda i, j: (0, 0)),
                  pl.BlockSpec((None, 1, d), lambda i, j: (i * tm // seq, 0, 0)),
                  pl.BlockSpec((None, 1, d), lambda i, j: (i * tm // seq, 0, 0)),
                  pl.BlockSpec((d, tn), lambda i, j: (0, j))],
        out_specs=pl.BlockSpec((tn // LANES, tm, LANES), lambda i, j: (j, i, 0)),
        scratch_shapes=[pltpu.VMEM((tm, d), BF16)],
        compiler_params=_cparams(("arbitrary", "arbitrary")),
        name="qkv_proj",
    )(x2, g.reshape(1, d), sc, sh, w_bf)


def _toeplitz(vec, rows, cols):
    n = rows + cols - 1
    assert vec.shape[-1] == n
    lead = vec.shape[:-1]
    ext = jnp.concatenate([vec, jnp.zeros(lead + (1,), vec.dtype)], axis=-1)
    flat = jnp.broadcast_to(ext[..., None, :], lead + (rows, n + 1)).reshape(lead + (rows * (n + 1),))
    skew = flat[..., :rows * n].reshape(lead + (rows, n))
    return skew[..., rows - 1:rows - 1 + cols]


def _na_bias_table(rpb):
    cidx = np.arange(GRID_W)
    col_start = np.clip(cidx - NA_COLS // 2, 0, GRID_W - NA_COLS)
    col_ok = (cidx[None, :] >= col_start[:, None]) & (cidx[None, :] < col_start[:, None] + NA_COLS)
    pad = GRID_W - NA_COLS
    vec = jnp.pad(rpb.astype(F32), ((0, 0), (0, 0), (pad, pad)))
    tab = jnp.where(col_ok, _toeplitz(vec, GRID_W, GRID_W), NEG)
    nh = rpb.shape[0]
    per_var = [tab[:, v:v + NA_ROWS].transpose(0, 2, 1, 3).reshape(nh, GRID_W, NA_ROWS * GRID_W)
               for v in range(NA_ROWS)]
    return jnp.stack(per_var, axis=1)


def _attn_group(qs, ks, vs, biases):
    ss = [lax.dot_general(q, k, (((1,), (1,)), ((), ())), preferred_element_type=F32) * SCALE + b
          for q, k, b in zip(qs, ks, biases)]
    ms = [jnp.max(s, axis=-1, keepdims=True) for s in ss]
    ps = [jnp.exp(s - m) for s, m in zip(ss, ms)]
    ls = [jnp.sum(p, axis=-1, keepdims=True) for p in ps]
    os = [jnp.dot(p.astype(BF16), v, preferred_element_type=F32) / l for p, v, l in zip(ps, vs, ls)]
    return os, [m + jnp.log(l) for m, l in zip(ms, ls)]


NA_GROUP = 8


def _na_kernel(q_ref, k_ref, v_ref, bias_ref, o_ref, *, rows):
    nkeys = NA_ROWS * GRID_W

    def body(g, carry):
        qs, ks, vs, bs, q0s = [], [], [], [], []
        for i in range(NA_GROUP):
            r = g * NA_GROUP + i
            rs = jnp.clip(r - NA_ROWS // 2, 0, rows - NA_ROWS)
            q0 = pl.multiple_of(r * GRID_W, GRID_W)
            k0 = pl.multiple_of(rs * GRID_W, GRID_W)
            q0s.append(q0)
            qs.append(q_ref[pl.ds(q0, GRID_W), :])
            ks.append(k_ref[pl.ds(k0, nkeys), :])
            vs.append(v_ref[pl.ds(k0, nkeys), :])
            bs.append(bias_ref[rs - r + (NA_ROWS - 1)])
        os, _ = _attn_group(qs, ks, vs, bs)
        for q0, o in zip(q0s, os):
            o_ref[pl.ds(q0, GRID_W), :] = o
        return carry

    lax.fori_loop(0, rows // NA_GROUP, body, 0)


def _na_attn(proj, bias, bsz, seq):
    nh = N_HEADS_NA
    t = bsz * seq
    blk = lambda off: pl.BlockSpec((None, seq, LANES), lambda h, b: (h + off, b, 0))
    return pl.pallas_call(
        functools.partial(_na_kernel, rows=seq // GRID_W),
        out_shape=jax.ShapeDtypeStruct((nh, t, LANES), F32),
        grid=(nh, bsz),
        in_specs=[blk(0), blk(nh), blk(2 * nh),
                  pl.BlockSpec((None, NA_ROWS, GRID_W, NA_ROWS * GRID_W), lambda h, b: (h, 0, 0, 0))],
        out_specs=pl.BlockSpec((None, seq, LANES), lambda h, b: (h, b, 0)),
        compiler_params=_cparams(("arbitrary", "arbitrary")),
        name="na_attn",
    )(proj, proj, proj, bias)


def _t5_bucket(rel):
    nb = T5_BUCKETS // 2
    max_exact = nb // 2
    ret = (rel > 0).astype(np.int32) * nb
    n = np.abs(rel)
    large = max_exact + (np.log(np.maximum(n, 1) / max_exact) / np.log(T5_MAX_DIST / max_exact)
                         * (nb - max_exact)).astype(np.int32)
    large = np.minimum(large, nb - 1)
    return (ret + np.where(n < max_exact, n, large)).astype(np.int32)


def _dil_geometry(sub_len):
    half = DIL_PATTERNS[0][0] // 2
    width = min(sub_len, QBLK + 2 * half)
    nblk = sub_len // QBLK
    starts = [min(max(QBLK * n - half, 0), sub_len - width) for n in range(nblk)]
    offs = sorted({ws - QBLK * n for n, ws in enumerate(starts)}, reverse=True)
    var = [offs.index(ws - QBLK * n) for n, ws in enumerate(starts)]
    return width, starts, offs, var


def _dil_bias_table(t5_table, dil, width, offs):
    half = DIL_PATTERNS[0][0] // 2
    tabs = []
    for off in offs:
        delta = np.arange(QBLK + width - 1) - (QBLK - 1) + off
        onehot = np.eye(T5_BUCKETS, dtype=np.float32)[_t5_bucket(delta * dil)]
        vals = jnp.dot(jnp.asarray(onehot), t5_table.astype(F32), precision=lax.Precision.HIGHEST)
        vec = jnp.where((np.abs(delta) <= half)[:, None], vals, NEG).T
        tabs.append(_toeplitz(vec, QBLK, width))
    return jnp.stack(tabs, axis=1)


DIL_GROUP = 4


def _dil_kernel(q_ref, k_ref, v_ref, b1_ref, b4_ref, b16_ref, o_ref, qf, kf, vf, o4, l4, o16, l16, *, seq):
    qf[...] = q_ref[...].astype(F32)
    kf[...] = k_ref[...].astype(F32)
    vf[...] = v_ref[...].astype(F32)

    for dil, bias_ref, o_s, l_s in ((DIL_PATTERNS[2][1], b16_ref, o16, l16), (DIL_PATTERNS[1][1], b4_ref, o4, l4)):
        width, starts, _, var = _dil_geometry(seq // dil)
        blocks = [(rho, n, ws) for rho in range(dil) for n, ws in enumerate(starts)]
        for g in range(0, len(blocks), DIL_GROUP):
            grp = blocks[g:g + DIL_GROUP]
            qrows = [pl.ds(rho + dil * QBLK * n, QBLK, stride=dil) for rho, n, _ in grp]
            krows = [pl.ds(rho + dil * ws, width, stride=dil) for rho, _, ws in grp]
            os, lses = _attn_group([qf[r, :].astype(BF16) for r in qrows],
                                   [kf[r, :].astype(BF16) for r in krows],
                                   [vf[r, :].astype(BF16) for r in krows],
                                   [bias_ref[var[n]] for _, n, _ in grp])
            for r, o, lse in zip(qrows, os, lses):
                o_s[r, :] = o
                l_s[r, :] = jnp.broadcast_to(lse, (QBLK, LANES))

    width, starts, _, _ = _dil_geometry(seq)
    nblk = len(starts)
    half = DIL_PATTERNS[0][0] // 2

    def body(g, carry):
        rows, krows, bs = [], [], []
        for i in range(DIL_GROUP):
            n = g * DIL_GROUP + i
            ws = pl.multiple_of(jnp.clip(n * QBLK - half, 0, seq - width), half)
            rows.append(pl.ds(pl.multiple_of(n * QBLK, QBLK), QBLK))
            krows.append(pl.ds(ws, width))
            bs.append(b1_ref[jnp.where(n == 0, 0, jnp.where(n == nblk - 1, 2, 1))])
        os, lses = _attn_group([q_ref[r, :] for r in rows], [k_ref[r, :] for r in krows],
                               [v_ref[r, :] for r in krows], bs)
        for r, o1, lse1 in zip(rows, os, lses):
            lse4, lse16 = l4[r, :], l16[r, :]
            mx = jnp.maximum(jnp.maximum(lse4, lse16), lse1)
            e1 = jnp.exp(lse1 - mx)
            e4 = jnp.exp(lse4 - mx)
            e16 = jnp.exp(lse16 - mx)
            o_ref[r, :] = (e1 * o1 + e4 * o4[r, :] + e16 * o16[r, :]) / (e1 + e4 + e16)
        return carry

    lax.fori_loop(0, nblk // DIL_GROUP, body, 0)


def _dil_attn(proj, t5_table, bsz, seq):
    nh = N_HEADS_DIL
    tables = []
    for _, dil in DIL_PATTERNS:
        width, _, offs, var = _dil_geometry(seq // dil)
        tables.append(_dil_bias_table(t5_table, dil, width, offs))
    width, _, offs, var = _dil_geometry(seq)
    assert offs == [0, -(DIL_PATTERNS[0][0] // 2), -DIL_PATTERNS[0][0]] and var[0] == 0 and var[-1] == 2
    first = 3 * N_HEADS_NA
    blk = lambda off: pl.BlockSpec((None, seq, LANES), lambda h, b: (first + off + h, b, 0))
    tab = lambda t: pl.BlockSpec((None,) + t.shape[1:], lambda h, b: (h, 0, 0, 0))
    return pl.pallas_call(
        functools.partial(_dil_kernel, seq=seq),
        out_shape=jax.ShapeDtypeStruct((nh, bsz * seq, LANES), F32),
        grid=(nh, bsz),
        in_specs=[blk(0), blk(nh), blk(2 * nh)] + [tab(t) for t in tables],
        out_specs=pl.BlockSpec((None, seq, LANES), lambda h, b: (h, b, 0)),
        scratch_shapes=[pltpu.VMEM((seq, LANES), F32)] * 7,
        compiler_params=_cparams(("arbitrary", "arbitrary")),
        name="dil_attn",
    )(proj, proj, proj, *tables)


def _mix_kernel(na_ref, dl_ref, gna_ref, gdl_ref, wo_ref, x_ref, gt1_ref, sc2_ref, sh2_ref,
                gpost_ref, gpre_ref, wr_ref, br_ref, x1_ref, hp_ref, lg_ref):
    nh = na_ref.shape[0]
    na = jnp.concatenate([na_ref[h] for h in range(nh)], axis=1)
    dl = jnp.concatenate([dl_ref[h] for h in range(nh)], axis=1)
    lhs = jnp.concatenate([_rms(na, gna_ref[...]), _rms(dl, gdl_ref[...])], axis=1).astype(BF16)
    mixed = jnp.dot(lhs, wo_ref[...], preferred_element_type=F32)
    x1 = x_ref[...] + gt1_ref[...] * _rms(mixed, gpost_ref[...])
    x1_ref[...] = x1
    hf = _rms(x1, gpre_ref[...]) * (1.0 + sc2_ref[...]) + sh2_ref[...]
    hb = hf.astype(BF16)
    lg_ref[...] = jnp.dot(hb, wr_ref[...], preferred_element_type=F32) + br_ref[...]
    half = hb.shape[1] // 2
    lo = lax.bitcast_convert_type(hb[:, :half].astype(F32), U32)
    hi = lax.bitcast_convert_type(hb[:, half:].astype(F32), U32)
    hp_ref[...] = (hi & U32(0xFFFF0000)) | lax.shift_right_logical(lo, U32(16))


def _mix_out(out_na, out_dil, g_na, g_dil, wo_bf, x2, gt1, sc2, sh2, g_post, g_pre, wr_bf, br, seq):
    t, d = x2.shape
    nh = out_na.shape[0]
    tm = 256
    wna = g_na.shape[-1]
    row = lambda n: pl.BlockSpec((1, n), lambda i: (0, 0))
    per_b = pl.BlockSpec((None, 1, d), lambda i: (i * tm // seq, 0, 0))
    heads = pl.BlockSpec((nh, tm, LANES), lambda i: (0, i, 0))
    return pl.pallas_call(
        _mix_kernel,
        out_shape=(jax.ShapeDtypeStruct((t, d), F32),
                   jax.ShapeDtypeStruct((t, d // 2), U32),
                   jax.ShapeDtypeStruct((t, LANES), F32)),
        grid=(t // tm,),
        in_specs=[heads, heads, row(wna), row(wna),
                  pl.BlockSpec((d, d), lambda i: (0, 0)),
                  pl.BlockSpec((tm, d), lambda i: (i, 0)),
                  per_b, per_b, per_b, row(d), row(d),
                  pl.BlockSpec((d, LANES), lambda i: (0, 0)), row(LANES)],
        out_specs=(pl.BlockSpec((tm, d), lambda i: (i, 0)),
                   pl.BlockSpec((tm, d // 2), lambda i: (i, 0)),
                   pl.BlockSpec((tm, LANES), lambda i: (i, 0))),
        compiler_params=_cparams(("arbitrary",)),
        name="mix_out",
    )(out_na, out_dil, g_na.reshape(1, wna), g_dil.reshape(1, wna), wo_bf, x2, gt1, sc2, sh2,
      g_post.reshape(1, d), g_pre.reshape(1, d), wr_bf, br)


def _route_kernel(lg_ref, idx_ref, rank_ref, gate_ref, cnt_ref, carry):
    step = pl.program_id(0)

    @pl.when(step == 0)
    def _():
        carry[...] = jnp.zeros_like(carry)

    v = lg_ref[...]
    ch = v.shape[0]
    lane = lax.broadcasted_iota(I32, v.shape, 1).astype(F32)
    vals, idxs = [], []
    for _ in range(TOP_K):
        m = jnp.max(v, axis=1, keepdims=True)
        ik = jnp.min(jnp.where(v == m, lane, float(LANES)), axis=1, keepdims=True)
        vals.append(m)
        idxs.append(ik)
        v = jnp.where(lane == ik, -jnp.inf, v)
    es = [jnp.exp(val - vals[0]) for val in vals]
    den = es[0] + es[1] + es[2] + es[3]
    sel = jnp.zeros(v.shape, F32)
    for ik in idxs:
        sel = jnp.where(lane == ik, 1.0, sel)
    ti = lax.broadcasted_iota(I32, (ch, ch), 0)
    tj = lax.broadcasted_iota(I32, (ch, ch), 1)
    lower = jnp.where(tj < ti, 1.0, 0.0).astype(BF16)
    cum = jnp.dot(lower, sel.astype(BF16), preferred_element_type=F32) + carry[0:1, :]
    idx_o = jnp.zeros(v.shape, F32)
    rank_o = jnp.zeros(v.shape, F32)
    gate_o = jnp.zeros(v.shape, F32)
    for k in range(TOP_K):
        rk = jnp.sum(jnp.where(lane == idxs[k], cum, 0.0), axis=1, keepdims=True)
        idx_o = jnp.where(lane == float(k), idxs[k], idx_o)
        rank_o = jnp.where(lane == float(k), rk, rank_o)
        gate_o = jnp.where(lane == float(k), es[k] / den, gate_o)
    idx_ref[...] = idx_o.astype(I32)
    rank_ref[...] = rank_o.astype(I32)
    gate_ref[...] = gate_o
    total = carry[...] + jnp.sum(sel, axis=0, keepdims=True)
    carry[...] = total
    cnt_ref[...] = total.astype(I32)


def _route(logits):
    t = logits.shape[0]
    ch = 512
    blk = pl.BlockSpec((ch, LANES), lambda i: (i, 0))
    return pl.pallas_call(
        _route_kernel,
        out_shape=(jax.ShapeDtypeStruct((t, LANES), I32),
                   jax.ShapeDtypeStruct((t, LANES), I32),
                   jax.ShapeDtypeStruct((t, LANES), F32),
                   jax.ShapeDtypeStruct((8, LANES), I32)),
        grid=(t // ch,),
        in_specs=[blk],
        out_specs=(blk, blk, blk, pl.BlockSpec((8, LANES), lambda i: (0, 0))),
        scratch_shapes=[pltpu.VMEM((8, LANES), F32)],
        compiler_params=_cparams(("arbitrary",)),
        name="route",
    )(logits)


def _schedule(counts, n_assign):
    max_sb = n_assign // SUPER_ROWS + N_EXPERTS
    nsb_e = (counts + SUPER_ROWS - 1) // SUPER_ROWS
    sb_end = jnp.cumsum(nsb_e)
    sb_start = sb_end - nsb_e
    row_start = (sb_start * SUPER_ROWS).astype(I32)
    nsb = sb_end[-1].astype(I32)
    s = jnp.arange(max_sb, dtype=I32)
    sb_e = jnp.minimum(jnp.searchsorted(sb_end, s, side="right"), N_EXPERTS - 1).astype(I32)
    rem = counts[sb_e] - (s - sb_start[sb_e]) * SUPER_ROWS
    sb_nb = jnp.where(s < nsb, (jnp.clip(rem, 0, SUPER_ROWS) + ROW_BLK - 1) // ROW_BLK, 0).astype(I32)
    zero_start = (s * SUPER_ROWS + jnp.maximum(sb_nb - 1, 0) * ROW_BLK).astype(I32)
    return max_sb, row_start, nsb.reshape(1), sb_e, sb_nb, zero_start


def _dispatch_kernel(eidx_s, rank_s, rs_s, zs_s, nsb_s, hp_ref, xs_hbm, zbuf, zsem, sem):
    step = pl.program_id(0)
    tt = hp_ref.shape[0]

    def zero_copy(s):
        return pltpu.make_async_copy(zbuf, xs_hbm.at[pl.ds(pl.multiple_of(zs_s[s], ROW_BLK), ROW_BLK)], zsem)

    @pl.when(step == 0)
    def _():
        zbuf[...] = jnp.zeros_like(zbuf)

        def zstart(s, c):
            zero_copy(s).start()
            return c

        def zwait(s, c):
            zero_copy(s).wait()
            return c

        lax.fori_loop(0, nsb_s[0], zstart, 0)
        lax.fori_loop(0, nsb_s[0], zwait, 0)

    base = step * tt

    def row_copy(j, k):
        a = (base + j) * TOP_K + k
        d = rs_s[eidx_s[a]] + rank_s[a]
        return pltpu.make_async_copy(hp_ref.at[pl.ds(j, 1)], xs_hbm.at[pl.ds(d, 1)], sem)

    def issue(j, c):
        for k in range(TOP_K):
            row_copy(j, k).start()
        return c

    def drain(j, c):
        for k in range(TOP_K):
            row_copy(j, k).wait()
        return c

    lax.fori_loop(0, tt, issue, 0)
    lax.fori_loop(0, tt, drain, 0)


def _dispatch(eidx, rank, row_start, zero_start, nsb, hp, n_rows):
    t, w = hp.shape
    tt = 256
    return pl.pallas_call(
        _dispatch_kernel,
        out_shape=jax.ShapeDtypeStruct((n_rows, w), U32),
        grid_spec=pltpu.PrefetchScalarGridSpec(
            num_scalar_prefetch=5,
            grid=(t // tt,),
            in_specs=[pl.BlockSpec((tt, w), lambda i, *_: (i, 0))],
            out_specs=pl.BlockSpec(memory_space=pl.ANY),
            scratch_shapes=[pltpu.VMEM((ROW_BLK, w), U32),
                            pltpu.SemaphoreType.DMA(()), pltpu.SemaphoreType.DMA(())]),
        compiler_params=_cparams(("arbitrary",)),
        name="dispatch",
    )(eidx, rank, row_start, zero_start, nsb, hp)


def _moe_kernel(sbe_s, sbn_s, nsb_s, x_ref, win_ref, bin_ref, wout_ref, bout_ref, o_ref, winb, woutb, zbuf):
    s = pl.program_id(0)
    f = pl.program_id(1)
    half = x_ref.shape[1]
    d = o_ref.shape[1]
    nsub = win_ref.shape[1] // (2 * LANES)

    @pl.when(s < nsb_s[0])
    def _():
        winb[...] = win_ref[...].astype(BF16)
        woutb[...] = wout_ref[...].astype(BF16)
        nb = sbn_s[s]
        lane = lax.broadcasted_iota(I32, (ROW_BLK, LANES), 1)
        even = (2 * lane) & (LANES - 1)
        odd = even + 1
        first_half = lane < LANES // 2

        def rows_of(r):
            return pl.ds(pl.multiple_of(r * ROW_BLK, ROW_BLK), ROW_BLK)

        @pl.when(f == 0)
        def _():
            def init(r, carry):
                o_ref[rows_of(r), :] = jnp.broadcast_to(bout_ref[...], (ROW_BLK, d))
                return carry

            def clear(r, carry):
                o_ref[rows_of(r), :] = jnp.zeros((ROW_BLK, d), F32)
                return carry

            lax.fori_loop(0, nb, init, 0)
            lax.fori_loop(nb, SUPER_BLKS, clear, 0)

        def first_matmul(r):
            xu = x_ref[rows_of(r), :]
            xa = lax.bitcast_convert_type(lax.shift_left(xu, U32(16)), F32).astype(BF16)
            xb = lax.bitcast_convert_type(xu & U32(0xFFFF0000), F32).astype(BF16)
            zbuf[...] = (jnp.dot(xa, winb[0:half, :], preferred_element_type=F32)
                         + jnp.dot(xb, winb[half:2 * half, :], preferred_element_type=F32) + bin_ref[...])

        def activation():
            hs = []
            for u in range(nsub):
                za = zbuf[:, (2 * u) * LANES:(2 * u + 1) * LANES]
                zb = zbuf[:, (2 * u + 1) * LANES:(2 * u + 2) * LANES]
                gate = jnp.where(first_half, jnp.take_along_axis(za, even, axis=1),
                                 jnp.take_along_axis(zb, even, axis=1))
                up = jnp.where(first_half, jnp.take_along_axis(za, odd, axis=1),
                               jnp.take_along_axis(zb, odd, axis=1))
                gate = jnp.minimum(gate, SWIGLU_LIMIT)
                up = jnp.clip(up, -SWIGLU_LIMIT, SWIGLU_LIMIT)
                glu = gate * jax.nn.sigmoid(SWIGLU_ALPHA * gate)
                hs.append(((up + 1.0) * glu).astype(BF16))
            return jnp.concatenate(hs, axis=1)

        def second_matmul(r, h):
            o_ref[rows_of(r), :] += jnp.dot(h, woutb[...], preferred_element_type=F32)

        first_matmul(0)

        def step(r, carry):
            h = activation()
            first_matmul(r + 1)
            second_matmul(r, h)
            return carry

        lax.fori_loop(0, nb - 1, step, 0)
        second_matmul(nb - 1, activation())


def _moe_ffn(sb_e, sb_nb, nsb, xs, w_in, b_in, w_out, b_out, max_sb):
    ne, d, f2 = w_in.shape
    ff = w_out.shape[1]
    nf = ff // F_CHUNK
    half = xs.shape[1]

    def live(s, f, nsb_ref):
        ok = s < nsb_ref[0]
        return jnp.where(ok, s, nsb_ref[0] - 1), jnp.where(ok, f, nf - 1)

    def x_map(s, f, sbe, sbn, nsb_ref):
        return live(s, f, nsb_ref)[0], 0

    def win_map(s, f, sbe, sbn, nsb_ref):
        se, fe = live(s, f, nsb_ref)
        return sbe[se], 0, fe

    def wout_map(s, f, sbe, sbn, nsb_ref):
        se, fe = live(s, f, nsb_ref)
        return sbe[se], fe, 0

    def bout_map(s, f, sbe, sbn, nsb_ref):
        return sbe[live(s, f, nsb_ref)[0]], 0, 0

    return pl.pallas_call(
        _moe_kernel,
        out_shape=jax.ShapeDtypeStruct((max_sb * SUPER_ROWS, d), F32),
        grid_spec=pltpu.PrefetchScalarGridSpec(
            num_scalar_prefetch=3,
            grid=(max_sb, nf),
            in_specs=[pl.BlockSpec((SUPER_ROWS, half), x_map),
                      pl.BlockSpec((None, d, 2 * F_CHUNK), win_map),
                      pl.BlockSpec((None, 1, 2 * F_CHUNK), win_map),
                      pl.BlockSpec((None, F_CHUNK, d), wout_map),
                      pl.BlockSpec((None, 1, d), bout_map)],
            out_specs=pl.BlockSpec((SUPER_ROWS, d), x_map),
            scratch_shapes=[pltpu.VMEM((d, 2 * F_CHUNK), BF16), pltpu.VMEM((F_CHUNK, d), BF16),
                            pltpu.VMEM((ROW_BLK, 2 * F_CHUNK), F32)]),
        compiler_params=_cparams(("arbitrary", "arbitrary")),
        name="moe_ffn",
    )(sb_e, sb_nb, nsb, xs, w_in, b_in.reshape(ne, 1, f2), w_out, b_out.reshape(ne, 1, d))


def _combine_kernel(eidx_s, rank_s, rs_s, gate_ref, x1_ref, gt2_ref, g_ref, ys_hbm, o_ref, buf, sem):
    step = pl.program_id(0)
    nstep = pl.num_programs(0)
    tt = x1_ref.shape[0]

    def row_copy(tile, slot, j, k):
        a = (tile * tt + j) * TOP_K + k
        d = rs_s[eidx_s[a]] + rank_s[a]
        return pltpu.make_async_copy(ys_hbm.at[pl.ds(d, 1)], buf.at[slot, k, pl.ds(j, 1)], sem.at[slot])

    def fetch(tile, slot):
        def issue(j, c):
            for k in range(TOP_K):
                row_copy(tile, slot, j, k).start()
            return c

        lax.fori_loop(0, tt, issue, 0)

    @pl.when(step == 0)
    def _():
        fetch(0, 0)

    @pl.when(step + 1 < nstep)
    def _():
        fetch(step + 1, (step + 1) % 2)

    slot = step % 2

    def drain(j, c):
        for k in range(TOP_K):
            row_copy(step, slot, j, k).wait()
        return c

    lax.fori_loop(0, tt, drain, 0)

    g = gate_ref[...]
    y = buf[slot, 0] * g[:, 0:1]
    for k in range(1, TOP_K):
        y = y + buf[slot, k] * g[:, k:k + 1]
    o_ref[...] = x1_ref[...] + gt2_ref[...] * _rms(y, g_ref[...])


def _combine(eidx, rank, row_start, gates, x1, gt2, g_post, ys, seq):
    t, d = x1.shape
    tt = 128
    return pl.pallas_call(
        _combine_kernel,
        out_shape=jax.ShapeDtypeStruct((t, d), F32),
        grid_spec=pltpu.PrefetchScalarGridSpec(
            num_scalar_prefetch=3,
            grid=(t // tt,),
            in_specs=[pl.BlockSpec((tt, LANES), lambda i, *_: (i, 0)),
                      pl.BlockSpec((tt, d), lambda i, *_: (i, 0)),
                      pl.BlockSpec((None, 1, d), lambda i, *_: (i * tt // seq, 0, 0)),
                      pl.BlockSpec((1, d), lambda i, *_: (0, 0)),
                      pl.BlockSpec(memory_space=pl.ANY)],
            out_specs=pl.BlockSpec((tt, d), lambda i, *_: (i, 0)),
            scratch_shapes=[pltpu.VMEM((2, TOP_K, tt, d), F32), pltpu.SemaphoreType.DMA((2,))]),
        compiler_params=_cparams(("arbitrary",)),
        name="combine",
    )(eidx, rank, row_start, gates, x1, gt2, g_post.reshape(1, d), ys)


def kernel(x, c, w_ada, b_ada, g_pre_mix, g_post_mix, w_in, rpb_na, t5_table, g_out_na, g_out_dil, w_o,
           g_pre_ffn, g_post_ffn, w_router, b_router, w_e_in, b_e_in, w_e_out, b_e_out):
    bsz, seq, d = x.shape
    t = bsz * seq
    for l in range(w_ada.shape[0]):
        mod = _ada_mod(c, w_ada[l], b_ada[l])
        sh1, sc1, gt1, sh2, sc2, gt2 = (m.reshape(bsz, 1, d) for m in jnp.split(mod, 6, axis=-1))
        x2 = x.reshape(t, d)

        proj = _qkv_proj(x2, g_pre_mix[l], sc1, sh1, w_in[l].astype(BF16), seq)
        out_na = _na_attn(proj, _na_bias_table(rpb_na[l]), bsz, seq)
        out_dil = _dil_attn(proj, t5_table, bsz, seq)

        ne = w_router.shape[-1]
        wr = jnp.zeros((d, LANES), BF16).at[:, :ne].set(w_router[l].astype(BF16))
        br = jnp.full((1, LANES), NEG, F32).at[0, :ne].set(b_router[l])
        x1, hp, logits = _mix_out(out_na, out_dil, g_out_na[l], g_out_dil[l], w_o[l].astype(BF16), x2,
                                  gt1, sc2, sh2, g_post_mix[l], g_pre_ffn[l], wr, br, seq)

        idx, rank, gates, cnt = _route(logits)
        eidx = idx[:, :TOP_K].reshape(-1)
        rank = rank[:, :TOP_K].reshape(-1)
        max_sb, row_start, nsb, sb_e, sb_nb, zero_start = _schedule(cnt[0, :ne], t * TOP_K)
        xs = _dispatch(eidx, rank, row_start, zero_start, nsb, hp, max_sb * SUPER_ROWS)
        ys = _moe_ffn(sb_e, sb_nb, nsb, xs, w_e_in[l], b_e_in[l], w_e_out[l], b_e_out[l], max_sb)
        x = _combine(eidx, rank, row_start, gates, x1, gt2, g_post_ffn[l], ys, seq).reshape(bsz, seq, d)
    return x
```

```python
import functools

import numpy as np
import jax
import jax.numpy as jnp
from jax import lax
from jax.experimental import pallas as pl
from jax.experimental.pallas import tpu as pltpu

F32 = jnp.float32
BF16 = jnp.bfloat16
U32 = jnp.uint32
I32 = jnp.int32

HEAD_DIM = 128
N_HEADS_NA = 8
N_HEADS_DIL = 8
GRID_W = 64
NA_ROWS = 8
NA_COLS = 16
DIL_PATTERNS = ((128, 1), (512, 4), (2048, 16))
T5_BUCKETS = 32
T5_MAX_DIST = 1024
N_EXPERTS = 32
TOP_K = 4
SWIGLU_LIMIT = 7.0
SWIGLU_ALPHA = 1.702
EPS = 1e-6
NEG = -1e30
SCALE = HEAD_DIM ** -0.5

LANES = 128
QBLK = 128
ROW_BLK = 256
SUPER_BLKS = 5
SUPER_ROWS = ROW_BLK * SUPER_BLKS
F_CHUNK = 256
VMEM_LIMIT = 56 * 1024 * 1024


def _cparams(sem, vmem=VMEM_LIMIT):
    return pltpu.CompilerParams(dimension_semantics=sem, vmem_limit_bytes=vmem)


def _rms(x, g):
    return x * lax.rsqrt(jnp.mean(x * x, axis=-1, keepdims=True) + EPS) * g


def _ada_kernel(c_ref, w_ref, b_ref, o_ref):
    c = c_ref[...]
    s = c * jax.nn.sigmoid(c)
    o_ref[...] = jnp.dot(s.astype(BF16), w_ref[...].astype(BF16),
                         preferred_element_type=F32) + b_ref[...]


def _ada_mod(c, w, b):
    bsz, d = c.shape
    n = w.shape[1]
    tn = 1024
    cp = jnp.zeros((8, d), F32).at[:bsz].set(c)
    out = pl.pallas_call(
        _ada_kernel,
        out_shape=jax.ShapeDtypeStruct((8, n), F32),
        grid=(n // tn,),
        in_specs=[pl.BlockSpec((8, d), lambda j: (0, 0)),
                  pl.BlockSpec((d, tn), lambda j: (0, j)),
                  pl.BlockSpec((1, tn), lambda j: (0, j))],
        out_specs=pl.BlockSpec((8, tn), lambda j: (0, j)),
        compiler_params=_cparams(("arbitrary",)),
        name="ada_mod",
    )(cp, w, b.reshape(1, n))
    return out[:bsz]


def _qkv_kernel(x_ref, g_ref, sc_ref, sh_ref, w_ref, o_ref, h_scr):
    @pl.when(pl.program_id(1) == 0)
    def _():
        h = _rms(x_ref[...], g_ref[...]) * (1.0 + sc_ref[...]) + sh_ref[...]
        h_scr[...] = h.astype(BF16)

    acc = jnp.dot(h_scr[...], w_ref[...], preferred_element_type=F32)
    for u in range(o_ref.shape[0]):
        o_ref[u] = acc[:, u * LANES:(u + 1) * LANES].astype(BF16)


def _qkv_proj(x2, g, sc, sh, w_bf, seq):
    t, d = x2.shape
    n = w_bf.shape[1]
    tm, tn = 1024, 1024
    return pl.pallas_call(
        _qkv_kernel,
        out_shape=jax.ShapeDtypeStruct((n // LANES, t, LANES), BF16),
        grid=(t // tm, n // tn),
        in_specs=[pl.BlockSpec((tm, d), lambda i, j: (i, 0)),
                  pl.BlockSpec((1, d), lambda i, j: (0, 0)),
                  pl.BlockSpec((None, 1, d), lambda i, j: (i * tm // seq, 0, 0)),
                  pl.BlockSpec((None, 1, d), lambda i, j: (i * tm // seq, 0, 0)),
                  pl.BlockSpec((d, tn), lambda i, j: (0, j))],
        out_specs=pl.BlockSpec((tn // LANES, tm, LANES), lambda i, j: (j, i, 0)),
        scratch_shapes=[pltpu.VMEM((tm, d), BF16)],
        compiler_params=_cparams(("arbitrary", "arbitrary")),
        name="qkv_proj",
    )(x2, g.reshape(1, d), sc, sh, w_bf)


def _toeplitz(vec, rows, cols):
    n = rows + cols - 1
    assert vec.shape[-1] == n
    lead = vec.shape[:-1]
    ext = jnp.concatenate([vec, jnp.zeros(lead + (1,), vec.dtype)], axis=-1)
    flat = jnp.broadcast_to(ext[..., None, :], lead + (rows, n + 1)).reshape(lead + (rows * (n + 1),))
    skew = flat[..., :rows * n].reshape(lead + (rows, n))
    return skew[..., rows - 1:rows - 1 + cols]


def _na_bias_table(rpb):
    cidx = np.arange(GRID_W)
    col_start = np.clip(cidx - NA_COLS // 2, 0, GRID_W - NA_COLS)
    col_ok = (cidx[None, :] >= col_start[:, None]) & (cidx[None, :] < col_start[:, None] + NA_COLS)
    pad = GRID_W - NA_COLS
    vec = jnp.pad(rpb.astype(F32), ((0, 0), (0, 0), (pad, pad)))
    tab = jnp.where(col_ok, _toeplitz(vec, GRID_W, GRID_W), NEG)
    nh = rpb.shape[0]
    per_var = [tab[:, v:v + NA_ROWS].transpose(0, 2, 1, 3).reshape(nh, GRID_W, NA_ROWS * GRID_W)
               for v in range(NA_ROWS)]
    return jnp.stack(per_var, axis=1)


def _attn_group(qs, ks, vs, biases):
    ss = [lax.dot_general(q, k, (((1,), (1,)), ((), ())), preferred_element_type=F32) * SCALE + b
          for q, k, b in zip(qs, ks, biases)]
    ms = [jnp.max(s, axis=-1, keepdims=True) for s in ss]
    ps = [jnp.exp(s - m) for s, m in zip(ss, ms)]
    ls = [jnp.sum(p, axis=-1, keepdims=True) for p in ps]
    os = [jnp.dot(p.astype(BF16), v, preferred_element_type=F32) / l for p, v, l in zip(ps, vs, ls)]
    return os, [m + jnp.log(l) for m, l in zip(ms, ls)]


NA_GROUP = 8


def _na_kernel(q_ref, k_ref, v_ref, bias_ref, o_ref, *, rows):
    nkeys = NA_ROWS * GRID_W

    def body(g, carry):
        qs, ks, vs, bs, q0s = [], [], [], [], []
        for i in range(NA_GROUP):
            r = g * NA_GROUP + i
            rs = jnp.clip(r - NA_ROWS // 2, 0, rows - NA_ROWS)
            q0 = pl.multiple_of(r * GRID_W, GRID_W)
            k0 = pl.multiple_of(rs * GRID_W, GRID_W)
            q0s.append(q0)
            qs.append(q_ref[pl.ds(q0, GRID_W), :])
            ks.append(k_ref[pl.ds(k0, nkeys), :])
            vs.append(v_ref[pl.ds(k0, nkeys), :])
            bs.append(bias_ref[rs - r + (NA_ROWS - 1)])
        os, _ = _attn_group(qs, ks, vs, bs)
        for q0, o in zip(q0s, os):
            o_ref[pl.ds(q0, GRID_W), :] = o
        return carry

    lax.fori_loop(0, rows // NA_GROUP, body, 0)


def _na_attn(proj, bias, bsz, seq):
    nh = N_HEADS_NA
    t = bsz * seq
    blk = lambda off: pl.BlockSpec((None, seq, LANES), lambda h, b: (h + off, b, 0))
    return pl.pallas_call(
        functools.partial(_na_kernel, rows=seq // GRID_W),
        out_shape=jax.ShapeDtypeStruct((nh, t, LANES), F32),
        grid=(nh, bsz),
        in_specs=[blk(0), blk(nh), blk(2 * nh),
                  pl.BlockSpec((None, NA_ROWS, GRID_W, NA_ROWS * GRID_W), lambda h, b: (h, 0, 0, 0))],
        out_specs=pl.BlockSpec((None, seq, LANES), lambda h, b: (h, b, 0)),
        compiler_params=_cparams(("arbitrary", "arbitrary")),
        name="na_attn",
    )(proj, proj, proj, bias)


def _t5_bucket(rel):
    nb = T5_BUCKETS // 2
    max_exact = nb // 2
    ret = (rel > 0).astype(np.int32) * nb
    n = np.abs(rel)
    large = max_exact + (np.log(np.maximum(n, 1) / max_exact) / np.log(T5_MAX_DIST / max_exact)
                         * (nb - max_exact)).astype(np.int32)
    large = np.minimum(large, nb - 1)
    return (ret + np.where(n < max_exact, n, large)).astype(np.int32)


def _dil_geometry(sub_len):
    half = DIL_PATTERNS[0][0] // 2
    width = min(sub_len, QBLK + 2 * half)
    nblk = sub_len // QBLK
    starts = [min(max(QBLK * n - half, 0), sub_len - width) for n in range(nblk)]
    offs = sorted({ws - QBLK * n for n, ws in enumerate(starts)}, reverse=True)
    var = [offs.index(ws - QBLK * n) for n, ws in enumerate(starts)]
    return width, starts, offs, var


def _dil_bias_table(t5_table, dil, width, offs):
    half = DIL_PATTERNS[0][0] // 2
    tabs = []
    for off in offs:
        delta = np.arange(QBLK + width - 1) - (QBLK - 1) + off
        onehot = np.eye(T5_BUCKETS, dtype=np.float32)[_t5_bucket(delta * dil)]
        vals = jnp.dot(jnp.asarray(onehot), t5_table.astype(F32), precision=lax.Precision.HIGHEST)
        vec = jnp.where((np.abs(delta) <= half)[:, None], vals, NEG).T
        tabs.append(_toeplitz(vec, QBLK, width))
    return jnp.stack(tabs, axis=1)


DIL_GROUP = 8


def _dil_kernel(q_ref, k_ref, v_ref, b1_ref, b4_ref, b16_ref, o_ref, qf, kf, vf, o4, l4, o16, l16, *, seq):
    qf[...] = q_ref[...].astype(F32)
    kf[...] = k_ref[...].astype(F32)
    vf[...] = v_ref[...].astype(F32)

    for dil, bias_ref, o_s, l_s in ((DIL_PATTERNS[2][1], b16_ref, o16, l16), (DIL_PATTERNS[1][1], b4_ref, o4, l4)):
        width, starts, _, var = _dil_geometry(seq // dil)
        blocks = [(rho, n, ws) for rho in range(dil) for n, ws in enumerate(starts)]
        for g in range(0, len(blocks), DIL_GROUP):
            grp = blocks[g:g + DIL_GROUP]
            qrows = [pl.ds(rho + dil * QBLK * n, QBLK, stride=dil) for rho, n, _ in grp]
            krows = [pl.ds(rho + dil * ws, width, stride=dil) for rho, _, ws in grp]
            os, lses = _attn_group([qf[r, :].astype(BF16) for r in qrows],
                                   [kf[r, :].astype(BF16) for r in krows],
                                   [vf[r, :].astype(BF16) for r in krows],
                                   [bias_ref[var[n]] for _, n, _ in grp])
            for r, o, lse in zip(qrows, os, lses):
                o_s[r, :] = o
                l_s[r, :] = jnp.broadcast_to(lse, (QBLK, LANES))

    width, starts, _, _ = _dil_geometry(seq)
    nblk = len(starts)
    half = DIL_PATTERNS[0][0] // 2

    def body(g, carry):
        rows, krows, bs = [], [], []
        for i in range(DIL_GROUP):
            n = g * DIL_GROUP + i
            ws = pl.multiple_of(jnp.clip(n * QBLK - half, 0, seq - width), half)
            rows.append(pl.ds(pl.multiple_of(n * QBLK, QBLK), QBLK))
            krows.append(pl.ds(ws, width))
            bs.append(b1_ref[jnp.where(n == 0, 0, jnp.where(n == nblk - 1, 2, 1))])
        os, lses = _attn_group([q_ref[r, :] for r in rows], [k_ref[r, :] for r in krows],
                               [v_ref[r, :] for r in krows], bs)
        for r, o1, lse1 in zip(rows, os, lses):
            lse4, lse16 = l4[r, :], l16[r, :]
            mx = jnp.maximum(jnp.maximum(lse4, lse16), lse1)
            e1 = jnp.exp(lse1 - mx)
            e4 = jnp.exp(lse4 - mx)
            e16 = jnp.exp(lse16 - mx)
            o_ref[r, :] = (e1 * o1 + e4 * o4[r, :] + e16 * o16[r, :]) / (e1 + e4 + e16)
        return carry

    lax.fori_loop(0, nblk // DIL_GROUP, body, 0)


def _dil_attn(proj, t5_table, bsz, seq):
    nh = N_HEADS_DIL
    tables = []
    for _, dil in DIL_PATTERNS:
        width, _, offs, var = _dil_geometry(seq // dil)
        tables.append(_dil_bias_table(t5_table, dil, width, offs))
    width, _, offs, var = _dil_geometry(seq)
    assert offs == [0, -(DIL_PATTERNS[0][0] // 2), -DIL_PATTERNS[0][0]] and var[0] == 0 and var[-1] == 2
    first = 3 * N_HEADS_NA
    blk = lambda off: pl.BlockSpec((None, seq, LANES), lambda h, b: (first + off + h, b, 0))
    tab = lambda t: pl.BlockSpec((None,) + t.shape[1:], lambda h, b: (h, 0, 0, 0))
    return pl.pallas_call(
        functools.partial(_dil_kernel, seq=seq),
        out_shape=jax.ShapeDtypeStruct((nh, bsz * seq, LANES), F32),
        grid=(nh, bsz),
        in_specs=[blk(0), blk(nh), blk(2 * nh)] + [tab(t) for t in tables],
        out_specs=pl.BlockSpec((None, seq, LANES), lambda h, b: (h, b, 0)),
        scratch_shapes=[pltpu.VMEM((seq, LANES), F32)] * 7,
        compiler_params=_cparams(("arbitrary", "arbitrary")),
        name="dil_attn",
    )(proj, proj, proj, *tables)


def _mix_kernel(na_ref, dl_ref, gna_ref, gdl_ref, wo_ref, x_ref, gt1_ref, sc2_ref, sh2_ref,
                gpost_ref, gpre_ref, wr_ref, br_ref, x1_ref, hp_ref, lg_ref):
    nh = na_ref.shape[0]
    na = jnp.concatenate([na_ref[h] for h in range(nh)], axis=1)
    dl = jnp.concatenate([dl_ref[h] for h in range(nh)], axis=1)
    lhs = jnp.concatenate([_rms(na, gna_ref[...]), _rms(dl, gdl_ref[...])], axis=1).astype(BF16)
    mixed = jnp.dot(lhs, wo_ref[...], preferred_element_type=F32)
    x1 = x_ref[...] + gt1_ref[...] * _rms(mixed, gpost_ref[...])
    x1_ref[...] = x1
    hf = _rms(x1, gpre_ref[...]) * (1.0 + sc2_ref[...]) + sh2_ref[...]
    hb = hf.astype(BF16)
    lg_ref[...] = jnp.dot(hb, wr_ref[...], preferred_element_type=F32) + br_ref[...]
    half = hb.shape[1] // 2
    lo = lax.bitcast_convert_type(hb[:, :half].astype(F32), U32)
    hi = lax.bitcast_convert_type(hb[:, half:].astype(F32), U32)
    hp_ref[...] = (hi & U32(0xFFFF0000)) | lax.shift_right_logical(lo, U32(16))


def _mix_out(out_na, out_dil, g_na, g_dil, wo_bf, x2, gt1, sc2, sh2, g_post, g_pre, wr_bf, br, seq):
    t, d = x2.shape
    nh = out_na.shape[0]
    tm = 256
    wna = g_na.shape[-1]
    row = lambda n: pl.BlockSpec((1, n), lambda i: (0, 0))
    per_b = pl.BlockSpec((None, 1, d), lambda i: (i * tm // seq, 0, 0))
    heads = pl.BlockSpec((nh, tm, LANES), lambda i: (0, i, 0))
    return pl.pallas_call(
        _mix_kernel,
        out_shape=(jax.ShapeDtypeStruct((t, d), F32),
                   jax.ShapeDtypeStruct((t, d // 2), U32),
                   jax.ShapeDtypeStruct((t, LANES), F32)),
        grid=(t // tm,),
        in_specs=[heads, heads, row(wna), row(wna),
                  pl.BlockSpec((d, d), lambda i: (0, 0)),
                  pl.BlockSpec((tm, d), lambda i: (i, 0)),
                  per_b, per_b, per_b, row(d), row(d),
                  pl.BlockSpec((d, LANES), lambda i: (0, 0)), row(LANES)],
        out_specs=(pl.BlockSpec((tm, d), lambda i: (i, 0)),
                   pl.BlockSpec((tm, d // 2), lambda i: (i, 0)),
                   pl.BlockSpec((tm, LANES), lambda i: (i, 0))),
        compiler_params=_cparams(("arbitrary",)),
        name="mix_out",
    )(out_na, out_dil, g_na.reshape(1, wna), g_dil.reshape(1, wna), wo_bf, x2, gt1, sc2, sh2,
      g_post.reshape(1, d), g_pre.reshape(1, d), wr_bf, br)


def _route_kernel(lg_ref, idx_ref, rank_ref, gate_ref, cnt_ref, carry):
    step = pl.program_id(0)

    @pl.when(step == 0)
    def _():
        carry[...] = jnp.zeros_like(carry)

    v = lg_ref[...]
    ch = v.shape[0]
    lane = lax.broadcasted_iota(I32, v.shape, 1).astype(F32)
    vals, idxs = [], []
    for _ in range(TOP_K):
        m = jnp.max(v, axis=1, keepdims=True)
        ik = jnp.min(jnp.where(v == m, lane, float(LANES)), axis=1, keepdims=True)
        vals.append(m)
        idxs.append(ik)
        v = jnp.where(lane == ik, -jnp.inf, v)
    es = [jnp.exp(val - vals[0]) for val in vals]
    den = es[0] + es[1] + es[2] + es[3]
    sel = jnp.zeros(v.shape, F32)
    for ik in idxs:
        sel = jnp.where(lane == ik, 1.0, sel)
    ti = lax.broadcasted_iota(I32, (ch, ch), 0)
    tj = lax.broadcasted_iota(I32, (ch, ch), 1)
    lower = jnp.where(tj < ti, 1.0, 0.0).astype(BF16)
    cum = jnp.dot(lower, sel.astype(BF16), preferred_element_type=F32) + carry[0:1, :]
    idx_o = jnp.zeros(v.shape, F32)
    rank_o = jnp.zeros(v.shape, F32)
    gate_o = jnp.zeros(v.shape, F32)
    for k in range(TOP_K):
        rk = jnp.sum(jnp.where(lane == idxs[k], cum, 0.0), axis=1, keepdims=True)
        idx_o = jnp.where(lane == float(k), idxs[k], idx_o)
        rank_o = jnp.where(lane == float(k), rk, rank_o)
        gate_o = jnp.where(lane == float(k), es[k] / den, gate_o)
    idx_ref[...] = idx_o.astype(I32)
    rank_ref[...] = rank_o.astype(I32)
    gate_ref[...] = gate_o
    total = carry[...] + jnp.sum(sel, axis=0, keepdims=True)
    carry[...] = total
    cnt_ref[...] = total.astype(I32)


def _route(logits):
    t = logits.shape[0]
    ch = 512
    blk = pl.BlockSpec((ch, LANES), lambda i: (i, 0))
    return pl.pallas_call(
        _route_kernel,
        out_shape=(jax.ShapeDtypeStruct((t, LANES), I32),
                   jax.ShapeDtypeStruct((t, LANES), I32),
                   jax.ShapeDtypeStruct((t, LANES), F32),
                   jax.ShapeDtypeStruct((8, LANES), I32)),
        grid=(t // ch,),
        in_specs=[blk],
        out_specs=(blk, blk, blk, pl.BlockSpec((8, LANES), lambda i: (0, 0))),
        scratch_shapes=[pltpu.VMEM((8, LANES), F32)],
        compiler_params=_cparams(("arbitrary",)),
        name="route",
    )(logits)


def _schedule(counts, n_assign):
    max_sb = n_assign // SUPER_ROWS + N_EXPERTS
    nsb_e = (counts + SUPER_ROWS - 1) // SUPER_ROWS
    sb_end = jnp.cumsum(nsb_e)
    sb_start = sb_end - nsb_e
    row_start = (sb_start * SUPER_ROWS).astype(I32)
    nsb = sb_end[-1].astype(I32)
    s = jnp.arange(max_sb, dtype=I32)
    sb_e = jnp.minimum(jnp.searchsorted(sb_end, s, side="right"), N_EXPERTS - 1).astype(I32)
    rem = counts[sb_e] - (s - sb_start[sb_e]) * SUPER_ROWS
    sb_nb = jnp.where(s < nsb, (jnp.clip(rem, 0, SUPER_ROWS) + ROW_BLK - 1) // ROW_BLK, 0).astype(I32)
    zero_start = (s * SUPER_ROWS + jnp.maximum(sb_nb - 1, 0) * ROW_BLK).astype(I32)
    return max_sb, row_start, nsb.reshape(1), sb_e, sb_nb, zero_start


ROW_UNROLL = 8


def _dispatch_kernel(dest_s, zs_s, nsb_s, hp_ref, xs_hbm, zbuf, zsem, sem):
    step = pl.program_id(0)
    tt = hp_ref.shape[0]

    def zero_copy(s):
        return pltpu.make_async_copy(zbuf, xs_hbm.at[pl.ds(pl.multiple_of(zs_s[s], ROW_BLK), ROW_BLK)], zsem)

    @pl.when(step == 0)
    def _():
        zbuf[...] = jnp.zeros_like(zbuf)

        def zstart(s, c):
            zero_copy(s).start()
            return c

        def zwait(s, c):
            zero_copy(s).wait()
            return c

        lax.fori_loop(0, nsb_s[0], zstart, 0)
        lax.fori_loop(0, nsb_s[0], zwait, 0)

    base = step * tt * TOP_K

    def issue(jj, c):
        for u in range(ROW_UNROLL):
            j = jj * ROW_UNROLL + u
            for k in range(TOP_K):
                d = dest_s[base + j * TOP_K + k]
                pltpu.make_async_copy(hp_ref.at[pl.ds(j, 1)], xs_hbm.at[pl.ds(d, 1)], sem).start()
        return c

    lax.fori_loop(0, tt // ROW_UNROLL, issue, 0)
    for k in range(TOP_K):
        pltpu.make_async_copy(hp_ref, xs_hbm.at[pl.ds(0, tt)], sem).wait()


def _dispatch(dest, zero_start, nsb, hp, n_rows):
    t, w = hp.shape
    tt = 256
    return pl.pallas_call(
        _dispatch_kernel,
        out_shape=jax.ShapeDtypeStruct((n_rows, w), U32),
        grid_spec=pltpu.PrefetchScalarGridSpec(
            num_scalar_prefetch=3,
            grid=(t // tt,),
            in_specs=[pl.BlockSpec((tt, w), lambda i, *_: (i, 0))],
            out_specs=pl.BlockSpec(memory_space=pl.ANY),
            scratch_shapes=[pltpu.VMEM((ROW_BLK, w), U32),
                            pltpu.SemaphoreType.DMA(()), pltpu.SemaphoreType.DMA(())]),
        compiler_params=_cparams(("arbitrary",)),
        name="dispatch",
    )(dest, zero_start, nsb, hp)


def _moe_kernel(sbe_s, sbn_s, nsb_s, x_ref, win_ref, bin_ref, wout_ref, bout_ref, o_ref, winb, woutb, zbuf):
    s = pl.program_id(0)
    f = pl.program_id(1)
    half = x_ref.shape[1]
    d = o_ref.shape[1]
    nsub = win_ref.shape[1] // (2 * LANES)

    @pl.when(s < nsb_s[0])
    def _():
        nb = sbn_s[s]
        lane = lax.broadcasted_iota(I32, (ROW_BLK, LANES), 1)
        even = (2 * lane) & (LANES - 1)
        odd = even + 1
        first_half = lane < LANES // 2

        def rows_of(r):
            return pl.ds(pl.multiple_of(r * ROW_BLK, ROW_BLK), ROW_BLK)

        @pl.when(f == 0)
        def _():
            def init(r, carry):
                o_ref[rows_of(r), :] = jnp.broadcast_to(bout_ref[...], (ROW_BLK, d))
                return carry

            def clear(r, carry):
                o_ref[rows_of(r), :] = jnp.zeros((ROW_BLK, d), F32)
                return carry

            lax.fori_loop(0, nb, init, 0)
            lax.fori_loop(nb, SUPER_BLKS, clear, 0)

        def unpack(r):
            xu = x_ref[rows_of(r), :]
            xa = lax.bitcast_convert_type(lax.shift_left(xu, U32(16)), F32).astype(BF16)
            xb = lax.bitcast_convert_type(xu & U32(0xFFFF0000), F32).astype(BF16)
            return xa, xb

        def first_matmul(r):
            xa, xb = unpack(r)
            zbuf[...] = (jnp.dot(xa, winb[0:half, :], preferred_element_type=F32)
                         + jnp.dot(xb, winb[half:2 * half, :], preferred_element_type=F32) + bin_ref[...])

        def first_matmul_casting(r):
            xa, xb = unpack(r)
            kq = half // 2
            acc = bin_ref[...]
            for ks in range(4):
                wb = win_ref[ks * kq:(ks + 1) * kq, :].astype(BF16)
                winb[ks * kq:(ks + 1) * kq, :] = wb
                xpart = (xa, xb)[ks // 2][:, (ks % 2) * kq:(ks % 2 + 1) * kq]
                acc = acc + jnp.dot(xpart, wb, preferred_element_type=F32)
            zbuf[...] = acc

        def activation():
            hs = []
            for u in range(nsub):
                za = zbuf[:, (2 * u) * LANES:(2 * u + 1) * LANES]
                zb = zbuf[:, (2 * u + 1) * LANES:(2 * u + 2) * LANES]
                gate = jnp.where(first_half, jnp.take_along_axis(za, even, axis=1),
                                 jnp.take_along_axis(zb, even, axis=1))
                up = jnp.where(first_half, jnp.take_along_axis(za, odd, axis=1),
                               jnp.take_along_axis(zb, odd, axis=1))
                gate = jnp.minimum(gate, SWIGLU_LIMIT)
                up = jnp.clip(up, -SWIGLU_LIMIT, SWIGLU_LIMIT)
                glu = gate * jax.nn.sigmoid(SWIGLU_ALPHA * gate)
                hs.append(((up + 1.0) * glu).astype(BF16))
            return jnp.concatenate(hs, axis=1)

        def second_matmul(r, h):
            o_ref[rows_of(r), :] += jnp.dot(h, woutb[...], preferred_element_type=F32)

        first_matmul_casting(0)
        woutb[...] = wout_ref[...].astype(BF16)

        def step(r, carry):
            h = activation()
            first_matmul(r + 1)
            second_matmul(r, h)
            return carry

        lax.fori_loop(0, nb - 1, step, 0)
        second_matmul(nb - 1, activation())


def _moe_ffn(sb_e, sb_nb, nsb, xs, w_in, b_in, w_out, b_out, max_sb):
    ne, d, f2 = w_in.shape
    ff = w_out.shape[1]
    nf = ff // F_CHUNK
    half = xs.shape[1]

    def live(s, f, nsb_ref):
        ok = s < nsb_ref[0]
        return jnp.where(ok, s, nsb_ref[0] - 1), jnp.where(ok, f, nf - 1)

    def x_map(s, f, sbe, sbn, nsb_ref):
        return live(s, f, nsb_ref)[0], 0

    def win_map(s, f, sbe, sbn, nsb_ref):
        se, fe = live(s, f, nsb_ref)
        return sbe[se], 0, fe

    def wout_map(s, f, sbe, sbn, nsb_ref):
        se, fe = live(s, f, nsb_ref)
        return sbe[se], fe, 0

    def bout_map(s, f, sbe, sbn, nsb_ref):
        return sbe[live(s, f, nsb_ref)[0]], 0, 0

    return pl.pallas_call(
        _moe_kernel,
        out_shape=jax.ShapeDtypeStruct((max_sb * SUPER_ROWS, d), F32),
        grid_spec=pltpu.PrefetchScalarGridSpec(
            num_scalar_prefetch=3,
            grid=(nsb[0], nf),
            in_specs=[pl.BlockSpec((SUPER_ROWS, half), x_map),
                      pl.BlockSpec((None, d, 2 * F_CHUNK), win_map),
                      pl.BlockSpec((None, 1, 2 * F_CHUNK), win_map),
                      pl.BlockSpec((None, F_CHUNK, d), wout_map),
                      pl.BlockSpec((None, 1, d), bout_map)],
            out_specs=pl.BlockSpec((SUPER_ROWS, d), x_map),
            scratch_shapes=[pltpu.VMEM((d, 2 * F_CHUNK), BF16), pltpu.VMEM((F_CHUNK, d), BF16),
                            pltpu.VMEM((ROW_BLK, 2 * F_CHUNK), F32)]),
        compiler_params=_cparams(("arbitrary", "arbitrary")),
        name="moe_ffn",
    )(sb_e, sb_nb, nsb, xs, w_in, b_in.reshape(ne, 1, f2), w_out, b_out.reshape(ne, 1, d))


def _combine_kernel(dest_s, gate_ref, x1_ref, gt2_ref, g_ref, ys_hbm, o_ref, buf, sem):
    step = pl.program_id(0)
    nstep = pl.num_programs(0)
    tt = x1_ref.shape[0]

    def fetch(tile, slot):
        base = tile * tt * TOP_K

        def issue(jj, c):
            for u in range(ROW_UNROLL):
                j = jj * ROW_UNROLL + u
                for k in range(TOP_K):
                    d = dest_s[base + j * TOP_K + k]
                    pltpu.make_async_copy(ys_hbm.at[pl.ds(d, 1)], buf.at[slot, k, pl.ds(j, 1)],
                                          sem.at[slot]).start()
            return c

        lax.fori_loop(0, tt // ROW_UNROLL, issue, 0)

    @pl.when(step == 0)
    def _():
        fetch(0, 0)

    @pl.when(step + 1 < nstep)
    def _():
        fetch(step + 1, (step + 1) % 2)

    slot = step % 2
    for k in range(TOP_K):
        pltpu.make_async_copy(ys_hbm.at[pl.ds(0, tt)], buf.at[slot, k], sem.at[slot]).wait()

    g = gate_ref[...]
    y = buf[slot, 0] * g[:, 0:1]
    for k in range(1, TOP_K):
        y = y + buf[slot, k] * g[:, k:k + 1]
    o_ref[...] = x1_ref[...] + gt2_ref[...] * _rms(y, g_ref[...])


def _combine(dest, gates, x1, gt2, g_post, ys, seq):
    t, d = x1.shape
    tt = 128
    return pl.pallas_call(
        _combine_kernel,
        out_shape=jax.ShapeDtypeStruct((t, d), F32),
        grid_spec=pltpu.PrefetchScalarGridSpec(
            num_scalar_prefetch=1,
            grid=(t // tt,),
            in_specs=[pl.BlockSpec((tt, LANES), lambda i, *_: (i, 0)),
                      pl.BlockSpec((tt, d), lambda i, *_: (i, 0)),
                      pl.BlockSpec((None, 1, d), lambda i, *_: (i * tt // seq, 0, 0)),
                      pl.BlockSpec((1, d), lambda i, *_: (0, 0)),
                      pl.BlockSpec(memory_space=pl.ANY)],
            out_specs=pl.BlockSpec((tt, d), lambda i, *_: (i, 0)),
            scratch_shapes=[pltpu.VMEM((2, TOP_K, tt, d), F32), pltpu.SemaphoreType.DMA((2,))]),
        compiler_params=_cparams(("arbitrary",)),
        name="combine",
    )(dest, gates, x1, gt2, g_post.reshape(1, d), ys)


def kernel(x, c, w_ada, b_ada, g_pre_mix, g_post_mix, w_in, rpb_na, t5_table, g_out_na, g_out_dil, w_o,
           g_pre_ffn, g_post_ffn, w_router, b_router, w_e_in, b_e_in, w_e_out, b_e_out):
    bsz, seq, d = x.shape
    t = bsz * seq
    for l in range(w_ada.shape[0]):
        mod = _ada_mod(c, w_ada[l], b_ada[l])
        sh1, sc1, gt1, sh2, sc2, gt2 = (m.reshape(bsz, 1, d) for m in jnp.split(mod, 6, axis=-1))
        x2 = x.reshape(t, d)

        proj = _qkv_proj(x2, g_pre_mix[l], sc1, sh1, w_in[l].astype(BF16), seq)
        out_na = _na_attn(proj, _na_bias_table(rpb_na[l]), bsz, seq)
        out_dil = _dil_attn(proj, t5_table, bsz, seq)

        ne = w_router.shape[-1]
        wr = jnp.zeros((d, LANES), BF16).at[:, :ne].set(w_router[l].astype(BF16))
        br = jnp.full((1, LANES), NEG, F32).at[0, :ne].set(b_router[l])
        x1, hp, logits = _mix_out(out_na, out_dil, g_out_na[l], g_out_dil[l], w_o[l].astype(BF16), x2,
                                  gt1, sc2, sh2, g_post_mix[l], g_pre_ffn[l], wr, br, seq)

        idx, rank, gates, cnt = _route(logits)
        max_sb, row_start, nsb, sb_e, sb_nb, zero_start = _schedule(cnt[0, :ne], t * TOP_K)
        eidx = idx[:, :TOP_K].reshape(-1)
        onehot = eidx[:, None] == jnp.arange(ne, dtype=I32)[None, :]
        dest = rank[:, :TOP_K].reshape(-1) + jnp.sum(jnp.where(onehot, row_start[None, :], 0), axis=1)
        xs = _dispatch(dest, zero_start, nsb, hp, max_sb * SUPER_ROWS)
        ys = _moe_ffn(sb_e, sb_nb, nsb, xs, w_e_in[l], b_e_in[l], w_e_out[l], b_e_out[l], max_sb)
        x = _combine(dest, gates, x1, gt2, g_post_ffn[l], ys, seq).reshape(bsz, seq, d)
    return x
```

```python
import functools

import numpy as np
import jax
import jax.numpy as jnp
from jax import lax
from jax.experimental import pallas as pl
from jax.experimental.pallas import tpu as pltpu

F32 = jnp.float32
BF16 = jnp.bfloat16
U32 = jnp.uint32
I32 = jnp.int32

HEAD_DIM = 128
N_HEADS_NA = 8
N_HEADS_DIL = 8
GRID_W = 64
NA_ROWS = 8
NA_COLS = 16
DIL_PATTERNS = ((128, 1), (512, 4), (2048, 16))
T5_BUCKETS = 32
T5_MAX_DIST = 1024
N_EXPERTS = 32
TOP_K = 4
SWIGLU_LIMIT = 7.0
SWIGLU_ALPHA = 1.702
EPS = 1e-6
NEG = -1e30
SCALE = HEAD_DIM ** -0.5

LANES = 128
QBLK = 128
ROW_BLK = 256
SUPER_BLKS = 8
SUPER_ROWS = ROW_BLK * SUPER_BLKS
UP_CHUNK = 512
DOWN_CHUNK = 512
VMEM_LIMIT = 56 * 1024 * 1024


def _cparams(sem, vmem=VMEM_LIMIT):
    return pltpu.CompilerParams(dimension_semantics=sem, vmem_limit_bytes=vmem)


def _rms(x, g):
    return x * lax.rsqrt(jnp.mean(x * x, axis=-1, keepdims=True) + EPS) * g


def _ada_kernel(c_ref, w_ref, b_ref, o_ref):
    c = c_ref[...]
    s = c * jax.nn.sigmoid(c)
    o_ref[...] = jnp.dot(s.astype(BF16), w_ref[...].astype(BF16),
                         preferred_element_type=F32) + b_ref[...]


def _ada_mod(c, w, b):
    bsz, d = c.shape
    n = w.shape[1]
    tn = 1024
    cp = jnp.zeros((8, d), F32).at[:bsz].set(c)
    out = pl.pallas_call(
        _ada_kernel,
        out_shape=jax.ShapeDtypeStruct((8, n), F32),
        grid=(n // tn,),
        in_specs=[pl.BlockSpec((8, d), lambda j: (0, 0)),
                  pl.BlockSpec((d, tn), lambda j: (0, j)),
                  pl.BlockSpec((1, tn), lambda j: (0, j))],
        out_specs=pl.BlockSpec((8, tn), lambda j: (0, j)),
        compiler_params=_cparams(("arbitrary",)),
        name="ada_mod",
    )(cp, w, b.reshape(1, n))
    return out[:bsz]


def _qkv_kernel(x_ref, g_ref, sc_ref, sh_ref, w_ref, o_ref, h_scr):
    @pl.when(pl.program_id(1) == 0)
    def _():
        h = _rms(x_ref[...], g_ref[...]) * (1.0 + sc_ref[...]) + sh_ref[...]
        h_scr[...] = h.astype(BF16)

    acc = jnp.dot(h_scr[...], w_ref[...], preferred_element_type=F32)
    for u in range(o_ref.shape[0]):
        o_ref[u] = acc[:, u * LANES:(u + 1) * LANES].astype(BF16)


def _qkv_proj(x2, g, sc, sh, w_bf, seq):
    t, d = x2.shape
    n = w_bf.shape[1]
    tm, tn = 1024, 1024
    return pl.pallas_call(
        _qkv_kernel,
        out_shape=jax.ShapeDtypeStruct((n // LANES, t, LANES), BF16),
        grid=(t // tm, n // tn),
        in_specs=[pl.BlockSpec((tm, d), lambda i, j: (i, 0)),
                  pl.BlockSpec((1, d), lambda i, j: (0, 0)),
                  pl.BlockSpec((None, 1, d), lambda i, j: (i * tm // seq, 0, 0)),
                  pl.BlockSpec((None, 1, d), lambda i, j: (i * tm // seq, 0, 0)),
                  pl.BlockSpec((d, tn), lambda i, j: (0, j))],
        out_specs=pl.BlockSpec((tn // LANES, tm, LANES), lambda i, j: (j, i, 0)),
        scratch_shapes=[pltpu.VMEM((tm, d), BF16)],
        compiler_params=_cparams(("arbitrary", "arbitrary")),
        name="qkv_proj",
    )(x2, g.reshape(1, d), sc, sh, w_bf)


def _toeplitz(vec, rows, cols):
    n = rows + cols - 1
    assert vec.shape[-1] == n
    lead = vec.shape[:-1]
    ext = jnp.concatenate([vec, jnp.zeros(lead + (1,), vec.dtype)], axis=-1)
    flat = jnp.broadcast_to(ext[..., None, :], lead + (rows, n + 1)).reshape(lead + (rows * (n + 1),))
    skew = flat[..., :rows * n].reshape(lead + (rows, n))
    return skew[..., rows - 1:rows - 1 + cols]


def _na_bias_table(rpb):
    cidx = np.arange(GRID_W)
    col_start = np.clip(cidx - NA_COLS // 2, 0, GRID_W - NA_COLS)
    col_ok = (cidx[None, :] >= col_start[:, None]) & (cidx[None, :] < col_start[:, None] + NA_COLS)
    pad = GRID_W - NA_COLS
    vec = jnp.pad(rpb.astype(F32), ((0, 0), (0, 0), (pad, pad)))
    tab = jnp.where(col_ok, _toeplitz(vec, GRID_W, GRID_W), NEG)
    nh = rpb.shape[0]
    per_var = [tab[:, v:v + NA_ROWS].transpose(0, 2, 1, 3).reshape(nh, GRID_W, NA_ROWS * GRID_W)
               for v in range(NA_ROWS)]
    return jnp.stack(per_var, axis=1)


def _attn_group(qs, ks, vs, biases):
    ss = [lax.dot_general(q, k, (((1,), (1,)), ((), ())), preferred_element_type=F32) * SCALE + b
          for q, k, b in zip(qs, ks, biases)]
    ms = [jnp.max(s, axis=-1, keepdims=True) for s in ss]
    ps = [jnp.exp(s - m) for s, m in zip(ss, ms)]
    ls = [jnp.sum(p, axis=-1, keepdims=True) for p in ps]
    os = [jnp.dot(p.astype(BF16), v, preferred_element_type=F32) / l for p, v, l in zip(ps, vs, ls)]
    return os, [m + jnp.log(l) for m, l in zip(ms, ls)]


NA_GROUP = 8


def _na_kernel(q_ref, k_ref, v_ref, bias_ref, o_ref, *, rows):
    nkeys = NA_ROWS * GRID_W

    def body(g, carry):
        qs, ks, vs, bs, q0s = [], [], [], [], []
        for i in range(NA_GROUP):
            r = g * NA_GROUP + i
            rs = jnp.clip(r - NA_ROWS // 2, 0, rows - NA_ROWS)
            q0 = pl.multiple_of(r * GRID_W, GRID_W)
            k0 = pl.multiple_of(rs * GRID_W, GRID_W)
            q0s.append(q0)
            qs.append(q_ref[pl.ds(q0, GRID_W), :])
            ks.append(k_ref[pl.ds(k0, nkeys), :])
            vs.append(v_ref[pl.ds(k0, nkeys), :])
            bs.append(bias_ref[rs - r + (NA_ROWS - 1)])
        os, _ = _attn_group(qs, ks, vs, bs)
        for q0, o in zip(q0s, os):
            o_ref[pl.ds(q0, GRID_W), :] = o
        return carry

    lax.fori_loop(0, rows // NA_GROUP, body, 0)


def _na_attn(proj, bias, bsz, seq):
    nh = N_HEADS_NA
    t = bsz * seq
    blk = lambda off: pl.BlockSpec((None, seq, LANES), lambda h, b: (h + off, b, 0))
    return pl.pallas_call(
        functools.partial(_na_kernel, rows=seq // GRID_W),
        out_shape=jax.ShapeDtypeStruct((nh, t, LANES), F32),
        grid=(nh, bsz),
        in_specs=[blk(0), blk(nh), blk(2 * nh),
                  pl.BlockSpec((None, NA_ROWS, GRID_W, NA_ROWS * GRID_W), lambda h, b: (h, 0, 0, 0))],
        out_specs=pl.BlockSpec((None, seq, LANES), lambda h, b: (h, b, 0)),
        compiler_params=_cparams(("arbitrary", "arbitrary")),
        name="na_attn",
    )(proj, proj, proj, bias)


def _t5_bucket(rel):
    nb = T5_BUCKETS // 2
    max_exact = nb // 2
    ret = (rel > 0).astype(np.int32) * nb
    n = np.abs(rel)
    large = max_exact + (np.log(np.maximum(n, 1) / max_exact) / np.log(T5_MAX_DIST / max_exact)
                         * (nb - max_exact)).astype(np.int32)
    large = np.minimum(large, nb - 1)
    return (ret + np.where(n < max_exact, n, large)).astype(np.int32)


def _dil_geometry(sub_len):
    half = DIL_PATTERNS[0][0] // 2
    width = min(sub_len, QBLK + 2 * half)
    nblk = sub_len // QBLK
    starts = [min(max(QBLK * n - half, 0), sub_len - width) for n in range(nblk)]
    offs = sorted({ws - QBLK * n for n, ws in enumerate(starts)}, reverse=True)
    var = [offs.index(ws - QBLK * n) for n, ws in enumerate(starts)]
    return width, starts, offs, var


def _dil_bias_table(t5_table, dil, width, offs):
    half = DIL_PATTERNS[0][0] // 2
    tabs = []
    for off in offs:
        delta = np.arange(QBLK + width - 1) - (QBLK - 1) + off
        onehot = np.eye(T5_BUCKETS, dtype=np.float32)[_t5_bucket(delta * dil)]
        vals = jnp.dot(jnp.asarray(onehot), t5_table.astype(F32), precision=lax.Precision.HIGHEST)
        vec = jnp.where((np.abs(delta) <= half)[:, None], vals, NEG).T
        tabs.append(_toeplitz(vec, QBLK, width))
    return jnp.stack(tabs, axis=1)


DIL_GROUP = 8


def _dil_kernel(q_ref, k_ref, v_ref, b1_ref, b4_ref, b16_ref, o_ref, qf, kf, vf, o4, l4, o16, l16, *, seq):
    qf[...] = q_ref[...].astype(F32)
    kf[...] = k_ref[...].astype(F32)
    vf[...] = v_ref[...].astype(F32)

    for dil, bias_ref, o_s, l_s in ((DIL_PATTERNS[2][1], b16_ref, o16, l16), (DIL_PATTERNS[1][1], b4_ref, o4, l4)):
        width, starts, _, var = _dil_geometry(seq // dil)
        blocks = [(rho, n, ws) for rho in range(dil) for n, ws in enumerate(starts)]
        for g in range(0, len(blocks), DIL_GROUP):
            grp = blocks[g:g + DIL_GROUP]
            qrows = [pl.ds(rho + dil * QBLK * n, QBLK, stride=dil) for rho, n, _ in grp]
            krows = [pl.ds(rho + dil * ws, width, stride=dil) for rho, _, ws in grp]
            os, lses = _attn_group([qf[r, :].astype(BF16) for r in qrows],
                                   [kf[r, :].astype(BF16) for r in krows],
                                   [vf[r, :].astype(BF16) for r in krows],
                                   [bias_ref[var[n]] for _, n, _ in grp])
            for r, o, lse in zip(qrows, os, lses):
                o_s[r, :] = o
                l_s[r, :] = jnp.broadcast_to(lse, (QBLK, LANES))

    width, starts, _, _ = _dil_geometry(seq)
    nblk = len(starts)
    half = DIL_PATTERNS[0][0] // 2

    def body(g, carry):
        rows, krows, bs = [], [], []
        for i in range(DIL_GROUP):
            n = g * DIL_GROUP + i
            ws = pl.multiple_of(jnp.clip(n * QBLK - half, 0, seq - width), half)
            rows.append(pl.ds(pl.multiple_of(n * QBLK, QBLK), QBLK))
            krows.append(pl.ds(ws, width))
            bs.append(b1_ref[jnp.where(n == 0, 0, jnp.where(n == nblk - 1, 2, 1))])
        os, lses = _attn_group([q_ref[r, :] for r in rows], [k_ref[r, :] for r in krows],
                               [v_ref[r, :] for r in krows], bs)
        for r, o1, lse1 in zip(rows, os, lses):
            lse4, lse16 = l4[r, :], l16[r, :]
            mx = jnp.maximum(jnp.maximum(lse4, lse16), lse1)
            e1 = jnp.exp(lse1 - mx)
            e4 = jnp.exp(lse4 - mx)
            e16 = jnp.exp(lse16 - mx)
            o_ref[r, :] = (e1 * o1 + e4 * o4[r, :] + e16 * o16[r, :]) / (e1 + e4 + e16)
        return carry

    lax.fori_loop(0, nblk // DIL_GROUP, body, 0)


def _dil_attn(proj, t5_table, bsz, seq):
    nh = N_HEADS_DIL
    tables = []
    for _, dil in DIL_PATTERNS:
        width, _, offs, var = _dil_geometry(seq // dil)
        tables.append(_dil_bias_table(t5_table, dil, width, offs))
    width, _, offs, var = _dil_geometry(seq)
    assert offs == [0, -(DIL_PATTERNS[0][0] // 2), -DIL_PATTERNS[0][0]] and var[0] == 0 and var[-1] == 2
    first = 3 * N_HEADS_NA
    blk = lambda off: pl.BlockSpec((None, seq, LANES), lambda h, b: (first + off + h, b, 0))
    tab = lambda t: pl.BlockSpec((None,) + t.shape[1:], lambda h, b: (h, 0, 0, 0))
    return pl.pallas_call(
        functools.partial(_dil_kernel, seq=seq),
        out_shape=jax.ShapeDtypeStruct((nh, bsz * seq, LANES), F32),
        grid=(nh, bsz),
        in_specs=[blk(0), blk(nh), blk(2 * nh)] + [tab(t) for t in tables],
        out_specs=pl.BlockSpec((None, seq, LANES), lambda h, b: (h, b, 0)),
        scratch_shapes=[pltpu.VMEM((seq, LANES), F32)] * 7,
        compiler_params=_cparams(("arbitrary", "arbitrary")),
        name="dil_attn",
    )(proj, proj, proj, *tables)


def _mix_kernel(na_ref, dl_ref, gna_ref, gdl_ref, wo_ref, x_ref, gt1_ref, sc2_ref, sh2_ref,
                gpost_ref, gpre_ref, wr_ref, br_ref, x1_ref, hp_ref, lg_ref):
    nh = na_ref.shape[0]
    na = jnp.concatenate([na_ref[h] for h in range(nh)], axis=1)
    dl = jnp.concatenate([dl_ref[h] for h in range(nh)], axis=1)
    lhs = jnp.concatenate([_rms(na, gna_ref[...]), _rms(dl, gdl_ref[...])], axis=1).astype(BF16)
    mixed = jnp.dot(lhs, wo_ref[...], preferred_element_type=F32)
    x1 = x_ref[...] + gt1_ref[...] * _rms(mixed, gpost_ref[...])
    x1_ref[...] = x1
    hf = _rms(x1, gpre_ref[...]) * (1.0 + sc2_ref[...]) + sh2_ref[...]
    hb = hf.astype(BF16)
    lg_ref[...] = jnp.dot(hb, wr_ref[...], preferred_element_type=F32) + br_ref[...]
    half = hb.shape[1] // 2
    lo = lax.bitcast_convert_type(hb[:, :half].astype(F32), U32)
    hi = lax.bitcast_convert_type(hb[:, half:].astype(F32), U32)
    hp_ref[...] = (hi & U32(0xFFFF0000)) | lax.shift_right_logical(lo, U32(16))


def _mix_out(out_na, out_dil, g_na, g_dil, wo_bf, x2, gt1, sc2, sh2, g_post, g_pre, wr_bf, br, seq):
    t, d = x2.shape
    nh = out_na.shape[0]
    tm = 256
    wna = g_na.shape[-1]
    row = lambda n: pl.BlockSpec((1, n), lambda i: (0, 0))
    per_b = pl.BlockSpec((None, 1, d), lambda i: (i * tm // seq, 0, 0))
    heads = pl.BlockSpec((nh, tm, LANES), lambda i: (0, i, 0))
    return pl.pallas_call(
        _mix_kernel,
        out_shape=(jax.ShapeDtypeStruct((t, d), F32),
                   jax.ShapeDtypeStruct((t, d // 2), U32),
                   jax.ShapeDtypeStruct((t, LANES), F32)),
        grid=(t // tm,),
        in_specs=[heads, heads, row(wna), row(wna),
                  pl.BlockSpec((d, d), lambda i: (0, 0)),
                  pl.BlockSpec((tm, d), lambda i: (i, 0)),
                  per_b, per_b, per_b, row(d), row(d),
                  pl.BlockSpec((d, LANES), lambda i: (0, 0)), row(LANES)],
        out_specs=(pl.BlockSpec((tm, d), lambda i: (i, 0)),
                   pl.BlockSpec((tm, d // 2), lambda i: (i, 0)),
                   pl.BlockSpec((tm, LANES), lambda i: (i, 0))),
        compiler_params=_cparams(("arbitrary",)),
        name="mix_out",
    )(out_na, out_dil, g_na.reshape(1, wna), g_dil.reshape(1, wna), wo_bf, x2, gt1, sc2, sh2,
      g_post.reshape(1, d), g_pre.reshape(1, d), wr_bf, br)


def _route_kernel(lg_ref, idx_ref, rank_ref, gate_ref, cnt_ref, carry):
    step = pl.program_id(0)

    @pl.when(step == 0)
    def _():
        carry[...] = jnp.zeros_like(carry)

    v = lg_ref[...]
    ch = v.shape[0]
    lane = lax.broadcasted_iota(I32, v.shape, 1).astype(F32)
    vals, idxs = [], []
    for _ in range(TOP_K):
        m = jnp.max(v, axis=1, keepdims=True)
        ik = jnp.min(jnp.where(v == m, lane, float(LANES)), axis=1, keepdims=True)
        vals.append(m)
        idxs.append(ik)
        v = jnp.where(lane == ik, -jnp.inf, v)
    es = [jnp.exp(val - vals[0]) for val in vals]
    den = es[0] + es[1] + es[2] + es[3]
    sel = jnp.zeros(v.shape, F32)
    for ik in idxs:
        sel = jnp.where(lane == ik, 1.0, sel)
    ti = lax.broadcasted_iota(I32, (ch, ch), 0)
    tj = lax.broadcasted_iota(I32, (ch, ch), 1)
    lower = jnp.where(tj < ti, 1.0, 0.0).astype(BF16)
    cum = jnp.dot(lower, sel.astype(BF16), preferred_element_type=F32) + carry[0:1, :]
    idx_o = jnp.zeros(v.shape, F32)
    rank_o = jnp.zeros(v.shape, F32)
    gate_o = jnp.zeros(v.shape, F32)
    for k in range(TOP_K):
        rk = jnp.sum(jnp.where(lane == idxs[k], cum, 0.0), axis=1, keepdims=True)
        idx_o = jnp.where(lane == float(k), idxs[k], idx_o)
        rank_o = jnp.where(lane == float(k), rk, rank_o)
        gate_o = jnp.where(lane == float(k), es[k] / den, gate_o)
    idx_ref[...] = idx_o.astype(I32)
    rank_ref[...] = rank_o.astype(I32)
    gate_ref[...] = gate_o
    total = carry[...] + jnp.sum(sel, axis=0, keepdims=True)
    carry[...] = total
    cnt_ref[...] = total.astype(I32)


def _route(logits):
    t = logits.shape[0]
    ch = 512
    blk = pl.BlockSpec((ch, LANES), lambda i: (i, 0))
    return pl.pallas_call(
        _route_kernel,
        out_shape=(jax.ShapeDtypeStruct((t, LANES), I32),
                   jax.ShapeDtypeStruct((t, LANES), I32),
                   jax.ShapeDtypeStruct((t, LANES), F32),
                   jax.ShapeDtypeStruct((8, LANES), I32)),
        grid=(t // ch,),
        in_specs=[blk],
        out_specs=(blk, blk, blk, pl.BlockSpec((8, LANES), lambda i: (0, 0))),
        scratch_shapes=[pltpu.VMEM((8, LANES), F32)],
        compiler_params=_cparams(("arbitrary",)),
        name="route",
    )(logits)


def _schedule(counts, n_assign):
    max_sb = n_assign // SUPER_ROWS + N_EXPERTS
    nsb_e = (counts + SUPER_ROWS - 1) // SUPER_ROWS
    sb_end = jnp.cumsum(nsb_e)
    sb_start = sb_end - nsb_e
    row_start = (sb_start * SUPER_ROWS).astype(I32)
    nsb = sb_end[-1].astype(I32)
    s = jnp.arange(max_sb, dtype=I32)
    sb_e = jnp.minimum(jnp.searchsorted(sb_end, s, side="right"), N_EXPERTS - 1).astype(I32)
    rem = counts[sb_e] - (s - sb_start[sb_e]) * SUPER_ROWS
    sb_nb = jnp.where(s < nsb, (jnp.clip(rem, 0, SUPER_ROWS) + ROW_BLK - 1) // ROW_BLK, 0).astype(I32)
    zero_start = (s * SUPER_ROWS + jnp.maximum(sb_nb - 1, 0) * ROW_BLK).astype(I32)
    return max_sb, row_start, nsb.reshape(1), sb_e, sb_nb, zero_start


ROW_UNROLL = 8


def _dispatch_kernel(dest_s, zs_s, nsb_s, hp_ref, xs_hbm, zbuf, zsem, sem):
    step = pl.program_id(0)
    tt = hp_ref.shape[0]

    def zero_copy(s):
        return pltpu.make_async_copy(zbuf, xs_hbm.at[pl.ds(pl.multiple_of(zs_s[s], ROW_BLK), ROW_BLK)], zsem)

    @pl.when(step == 0)
    def _():
        zbuf[...] = jnp.zeros_like(zbuf)

        def zstart(s, c):
            zero_copy(s).start()
            return c

        def zwait(s, c):
            zero_copy(s).wait()
            return c

        lax.fori_loop(0, nsb_s[0], zstart, 0)
        lax.fori_loop(0, nsb_s[0], zwait, 0)

    base = step * tt * TOP_K

    def issue(jj, c):
        for u in range(ROW_UNROLL):
            j = jj * ROW_UNROLL + u
            for k in range(TOP_K):
                d = dest_s[base + j * TOP_K + k]
                pltpu.make_async_copy(hp_ref.at[pl.ds(j, 1)], xs_hbm.at[pl.ds(d, 1)], sem).start()
        return c

    lax.fori_loop(0, tt // ROW_UNROLL, issue, 0)
    for k in range(TOP_K):
        pltpu.make_async_copy(hp_ref, xs_hbm.at[pl.ds(0, tt)], sem).wait()


def _dispatch(dest, zero_start, nsb, hp, n_rows):
    t, w = hp.shape
    tt = 256
    return pl.pallas_call(
        _dispatch_kernel,
        out_shape=jax.ShapeDtypeStruct((n_rows, w), U32),
        grid_spec=pltpu.PrefetchScalarGridSpec(
            num_scalar_prefetch=3,
            grid=(t // tt,),
            in_specs=[pl.BlockSpec((tt, w), lambda i, *_: (i, 0))],
            out_specs=pl.BlockSpec(memory_space=pl.ANY),
            scratch_shapes=[pltpu.VMEM((ROW_BLK, w), U32),
                            pltpu.SemaphoreType.DMA(()), pltpu.SemaphoreType.DMA(())]),
        compiler_params=_cparams(("arbitrary",)),
        name="dispatch",
    )(dest, zero_start, nsb, hp)


def _row_block(r):
    return pl.ds(pl.multiple_of(r * ROW_BLK, ROW_BLK), ROW_BLK)


def _moe_up_kernel(sbe_s, sbn_s, x_ref, win_ref, bin_ref, h_ref, winb, zbuf):
    nb = sbn_s[pl.program_id(0)]
    half = x_ref.shape[1]
    nsub = win_ref.shape[1] // (2 * LANES)
    lane = lax.broadcasted_iota(I32, (ROW_BLK, LANES), 1)
    even = (2 * lane) & (LANES - 1)
    odd = even + 1
    first_half = lane < LANES // 2

    def unpack(r):
        xu = x_ref[_row_block(r), :]
        xa = lax.bitcast_convert_type(lax.shift_left(xu, U32(16)), F32).astype(BF16)
        xb = lax.bitcast_convert_type(xu & U32(0xFFFF0000), F32).astype(BF16)
        return xa, xb

    def matmul(r):
        xa, xb = unpack(r)
        zbuf[...] = (jnp.dot(xa, winb[0:half, :], preferred_element_type=F32)
                     + jnp.dot(xb, winb[half:2 * half, :], preferred_element_type=F32) + bin_ref[...])

    def matmul_casting(r):
        xa, xb = unpack(r)
        kq = half // 2
        acc = bin_ref[...]
        for ks in range(4):
            wb = win_ref[ks * kq:(ks + 1) * kq, :].astype(BF16)
            winb[ks * kq:(ks + 1) * kq, :] = wb
            xpart = (xa, xb)[ks // 2][:, (ks % 2) * kq:(ks % 2 + 1) * kq]
            acc = acc + jnp.dot(xpart, wb, preferred_element_type=F32)
        zbuf[...] = acc

    def activation():
        hs = []
        for u in range(nsub):
            za = zbuf[:, (2 * u) * LANES:(2 * u + 1) * LANES]
            zb = zbuf[:, (2 * u + 1) * LANES:(2 * u + 2) * LANES]
            gate = jnp.where(first_half, jnp.take_along_axis(za, even, axis=1),
                             jnp.take_along_axis(zb, even, axis=1))
            up = jnp.where(first_half, jnp.take_along_axis(za, odd, axis=1),
                           jnp.take_along_axis(zb, odd, axis=1))
            gate = jnp.minimum(gate, SWIGLU_LIMIT)
            up = jnp.clip(up, -SWIGLU_LIMIT, SWIGLU_LIMIT)
            glu = gate * jax.nn.sigmoid(SWIGLU_ALPHA * gate)
            hs.append(((up + 1.0) * glu).astype(BF16))
        return jnp.concatenate(hs, axis=1)

    matmul_casting(0)

    def step(r, carry):
        h = activation()
        matmul(r + 1)
        h_ref[_row_block(r), :] = h
        return carry

    lax.fori_loop(0, nb - 1, step, 0)
    h_ref[_row_block(nb - 1), :] = activation()


def _moe_down_kernel(sbe_s, sbn_s, h_ref, wout_ref, bout_ref, o_ref, woutb):
    nb = sbn_s[pl.program_id(0)]
    kq = h_ref.shape[1] // 4
    h0 = h_ref[_row_block(0), :]
    acc = bout_ref[...]
    for ks in range(4):
        wb = wout_ref[ks * kq:(ks + 1) * kq, :].astype(BF16)
        woutb[ks * kq:(ks + 1) * kq, :] = wb
        acc = acc + jnp.dot(h0[:, ks * kq:(ks + 1) * kq], wb, preferred_element_type=F32)
    o_ref[_row_block(0), :] = acc

    def project(r):
        return jnp.dot(h_ref[_row_block(r), :], woutb[...], preferred_element_type=F32) + bout_ref[...]

    def pair(p, carry):
        r = 1 + 2 * p
        y0, y1 = project(r), project(r + 1)
        o_ref[_row_block(r), :] = y0
        o_ref[_row_block(r + 1), :] = y1
        return carry

    lax.fori_loop(0, (nb - 1) // 2, pair, 0)

    @pl.when((nb - 1) % 2 == 1)
    def _():
        o_ref[_row_block(nb - 1), :] = project(nb - 1)


def _moe_ffn(sb_e, sb_nb, nsb, xs, w_in, b_in, w_out, b_out, max_sb):
    ne, d, f2 = w_in.shape
    ff = w_out.shape[1]
    half = xs.shape[1]
    rows = max_sb * SUPER_ROWS
    sem = ("arbitrary", "arbitrary")

    h = pl.pallas_call(
        _moe_up_kernel,
        out_shape=jax.ShapeDtypeStruct((rows, ff), BF16),
        grid_spec=pltpu.PrefetchScalarGridSpec(
            num_scalar_prefetch=2,
            grid=(nsb[0], ff // UP_CHUNK),
            in_specs=[pl.BlockSpec((SUPER_ROWS, half), lambda s, j, sbe, sbn: (s, 0)),
                      pl.BlockSpec((None, d, 2 * UP_CHUNK), lambda s, j, sbe, sbn: (sbe[s], 0, j)),
                      pl.BlockSpec((None, 1, 2 * UP_CHUNK), lambda s, j, sbe, sbn: (sbe[s], 0, j))],
            out_specs=pl.BlockSpec((SUPER_ROWS, UP_CHUNK), lambda s, j, sbe, sbn: (s, j)),
            scratch_shapes=[pltpu.VMEM((d, 2 * UP_CHUNK), BF16), pltpu.VMEM((ROW_BLK, 2 * UP_CHUNK), F32)]),
        compiler_params=_cparams(sem),
        name="moe_up",
    )(sb_e, sb_nb, xs, w_in, b_in.reshape(ne, 1, f2))

    return pl.pallas_call(
        _moe_down_kernel,
        out_shape=jax.ShapeDtypeStruct((rows, d), F32),
        grid_spec=pltpu.PrefetchScalarGridSpec(
            num_scalar_prefetch=2,
            grid=(nsb[0], d // DOWN_CHUNK),
            in_specs=[pl.BlockSpec((SUPER_ROWS, ff), lambda s, n, sbe, sbn: (s, 0)),
                      pl.BlockSpec((None, ff, DOWN_CHUNK), lambda s, n, sbe, sbn: (sbe[s], 0, n)),
                      pl.BlockSpec((None, 1, DOWN_CHUNK), lambda s, n, sbe, sbn: (sbe[s], 0, n))],
            out_specs=pl.BlockSpec((SUPER_ROWS, DOWN_CHUNK), lambda s, n, sbe, sbn: (s, n)),
            scratch_shapes=[pltpu.VMEM((ff, DOWN_CHUNK), BF16)]),
        compiler_params=_cparams(sem),
        name="moe_down",
    )(sb_e, sb_nb, h, w_out, b_out.reshape(ne, 1, d))


def _combine_kernel(dest_s, gate_ref, x1_ref, gt2_ref, g_ref, ys_hbm, o_ref, buf, sem):
    step = pl.program_id(0)
    nstep = pl.num_programs(0)
    tt = x1_ref.shape[0]

    def fetch(tile, slot):
        base = tile * tt * TOP_K

        def issue(jj, c):
            for u in range(ROW_UNROLL):
                j = jj * ROW_UNROLL + u
                for k in range(TOP_K):
                    d = dest_s[base + j * TOP_K + k]
                    pltpu.make_async_copy(ys_hbm.at[pl.ds(d, 1)], buf.at[slot, k, pl.ds(j, 1)],
                                          sem.at[slot]).start()
            return c

        lax.fori_loop(0, tt // ROW_UNROLL, issue, 0)

    @pl.when(step == 0)
    def _():
        fetch(0, 0)

    @pl.when(step + 1 < nstep)
    def _():
        fetch(step + 1, (step + 1) % 2)

    slot = step % 2
    for k in range(TOP_K):
        pltpu.make_async_copy(ys_hbm.at[pl.ds(0, tt)], buf.at[slot, k], sem.at[slot]).wait()

    g = gate_ref[...]
    y = buf[slot, 0] * g[:, 0:1]
    for k in range(1, TOP_K):
        y = y + buf[slot, k] * g[:, k:k + 1]
    o_ref[...] = x1_ref[...] + gt2_ref[...] * _rms(y, g_ref[...])


def _combine(dest, gates, x1, gt2, g_post, ys, seq):
    t, d = x1.shape
    tt = 128
    return pl.pallas_call(
        _combine_kernel,
        out_shape=jax.ShapeDtypeStruct((t, d), F32),
        grid_spec=pltpu.PrefetchScalarGridSpec(
            num_scalar_prefetch=1,
            grid=(t // tt,),
            in_specs=[pl.BlockSpec((tt, LANES), lambda i, *_: (i, 0)),
                      pl.BlockSpec((tt, d), lambda i, *_: (i, 0)),
                      pl.BlockSpec((None, 1, d), lambda i, *_: (i * tt // seq, 0, 0)),
                      pl.BlockSpec((1, d), lambda i, *_: (0, 0)),
                      pl.BlockSpec(memory_space=pl.ANY)],
            out_specs=pl.BlockSpec((tt, d), lambda i, *_: (i, 0)),
            scratch_shapes=[pltpu.VMEM((2, TOP_K, tt, d), F32), pltpu.SemaphoreType.DMA((2,))]),
        compiler_params=_cparams(("arbitrary",)),
        name="combine",
    )(dest, gates, x1, gt2, g_post.reshape(1, d), ys)


def kernel(x, c, w_ada, b_ada, g_pre_mix, g_post_mix, w_in, rpb_na, t5_table, g_out_na, g_out_dil, w_o,
           g_pre_ffn, g_post_ffn, w_router, b_router, w_e_in, b_e_in, w_e_out, b_e_out):
    bsz, seq, d = x.shape
    t = bsz * seq
    for l in range(w_ada.shape[0]):
        mod = _ada_mod(c, w_ada[l], b_ada[l])
        sh1, sc1, gt1, sh2, sc2, gt2 = (m.reshape(bsz, 1, d) for m in jnp.split(mod, 6, axis=-1))
        x2 = x.reshape(t, d)

        proj = _qkv_proj(x2, g_pre_mix[l], sc1, sh1, w_in[l].astype(BF16), seq)
        out_na = _na_attn(proj, _na_bias_table(rpb_na[l]), bsz, seq)
        out_dil = _dil_attn(proj, t5_table, bsz, seq)

        ne = w_router.shape[-1]
        wr = jnp.zeros((d, LANES), BF16).at[:, :ne].set(w_router[l].astype(BF16))
        br = jnp.full((1, LANES), NEG, F32).at[0, :ne].set(b_router[l])
        x1, hp, logits = _mix_out(out_na, out_dil, g_out_na[l], g_out_dil[l], w_o[l].astype(BF16), x2,
                                  gt1, sc2, sh2, g_post_mix[l], g_pre_ffn[l], wr, br, seq)

        idx, rank, gates, cnt = _route(logits)
        max_sb, row_start, nsb, sb_e, sb_nb, zero_start = _schedule(cnt[0, :ne], t * TOP_K)
        eidx = idx[:, :TOP_K].reshape(-1)
        onehot = eidx[:, None] == jnp.arange(ne, dtype=I32)[None, :]
        dest = rank[:, :TOP_K].reshape(-1) + jnp.sum(jnp.where(onehot, row_start[None, :], 0), axis=1)
        xs = _dispatch(dest, zero_start, nsb, hp, max_sb * SUPER_ROWS)
        ys = _moe_ffn(sb_e, sb_nb, nsb, xs, w_e_in[l], b_e_in[l], w_e_out[l], b_e_out[l], max_sb)
        x = _combine(dest, gates, x1, gt2, g_post_ffn[l], ys, seq).reshape(bsz, seq, d)
    return x
```

```python
import functools

import numpy as np
import jax
import jax.numpy as jnp
from jax import lax
from jax.experimental import pallas as pl
from jax.experimental.pallas import tpu as pltpu

F32 = jnp.float32
BF16 = jnp.bfloat16
U32 = jnp.uint32
I32 = jnp.int32

HEAD_DIM = 128
N_HEADS_NA = 8
N_HEADS_DIL = 8
GRID_W = 64
NA_ROWS = 8
NA_COLS = 16
DIL_PATTERNS = ((128, 1), (512, 4), (2048, 16))
T5_BUCKETS = 32
T5_MAX_DIST = 1024
N_EXPERTS = 32
TOP_K = 4
SWIGLU_LIMIT = 7.0
SWIGLU_ALPHA = 1.702
EPS = 1e-6
NEG = -1e30
SCALE = HEAD_DIM ** -0.5

LANES = 128
QBLK = 128
ROW_BLK = 256
SUPER_BLKS = 7
SUPER_ROWS = ROW_BLK * SUPER_BLKS
UP_CHUNK = 512
DOWN_CHUNK = 1024
VMEM_LIMIT = 56 * 1024 * 1024


def _cparams(sem, vmem=VMEM_LIMIT):
    return pltpu.CompilerParams(dimension_semantics=sem, vmem_limit_bytes=vmem)


def _rms(x, g):
    return x * lax.rsqrt(jnp.mean(x * x, axis=-1, keepdims=True) + EPS) * g


def _ada_kernel(c_ref, w_ref, b_ref, o_ref):
    c = c_ref[...]
    s = c * jax.nn.sigmoid(c)
    o_ref[...] = jnp.dot(s.astype(BF16), w_ref[...].astype(BF16),
                         preferred_element_type=F32) + b_ref[...]


def _ada_mod(c, w, b):
    bsz, d = c.shape
    n = w.shape[1]
    tn = 1024
    cp = jnp.zeros((8, d), F32).at[:bsz].set(c)
    out = pl.pallas_call(
        _ada_kernel,
        out_shape=jax.ShapeDtypeStruct((8, n), F32),
        grid=(n // tn,),
        in_specs=[pl.BlockSpec((8, d), lambda j: (0, 0)),
                  pl.BlockSpec((d, tn), lambda j: (0, j)),
                  pl.BlockSpec((1, tn), lambda j: (0, j))],
        out_specs=pl.BlockSpec((8, tn), lambda j: (0, j)),
        compiler_params=_cparams(("arbitrary",)),
        name="ada_mod",
    )(cp, w, b.reshape(1, n))
    return out[:bsz]


def _qkv_kernel(x_ref, g_ref, sc_ref, sh_ref, w_ref, o_ref, h_scr):
    @pl.when(pl.program_id(1) == 0)
    def _():
        h = _rms(x_ref[...], g_ref[...]) * (1.0 + sc_ref[...]) + sh_ref[...]
        h_scr[...] = h.astype(BF16)

    acc = jnp.dot(h_scr[...], w_ref[...], preferred_element_type=F32)
    for u in range(o_ref.shape[0]):
        o_ref[u] = acc[:, u * LANES:(u + 1) * LANES].astype(BF16)


def _qkv_proj(x2, g, sc, sh, w_bf, seq):
    t, d = x2.shape
    n = w_bf.shape[1]
    tm, tn = 1024, 1024
    return pl.pallas_call(
        _qkv_kernel,
        out_shape=jax.ShapeDtypeStruct((n // LANES, t, LANES), BF16),
        grid=(t // tm, n // tn),
        in_specs=[pl.BlockSpec((tm, d), lambda i, j: (i, 0)),
                  pl.BlockSpec((1, d), lambda i, j: (0, 0)),
                  pl.BlockSpec((None, 1, d), lambda i, j: (i * tm // seq, 0, 0)),
                  pl.BlockSpec((None, 1, d), lambda i, j: (i * tm // seq, 0, 0)),
                  pl.BlockSpec((d, tn), lambda i, j: (0, j))],
        out_specs=pl.BlockSpec((tn // LANES, tm, LANES), lambda i, j: (j, i, 0)),
        scratch_shapes=[pltpu.VMEM((tm, d), BF16)],
        compiler_params=_cparams(("arbitrary", "arbitrary")),
        name="qkv_proj",
    )(x2, g.reshape(1, d), sc, sh, w_bf)


def _toeplitz(vec, rows, cols):
    n = rows + cols - 1
    assert vec.shape[-1] == n
    lead = vec.shape[:-1]
    ext = jnp.concatenate([vec, jnp.zeros(lead + (1,), vec.dtype)], axis=-1)
    flat = jnp.broadcast_to(ext[..., None, :], lead + (rows, n + 1)).reshape(lead + (rows * (n + 1),))
    skew = flat[..., :rows * n].reshape(lead + (rows, n))
    return skew[..., rows - 1:rows - 1 + cols]


def _na_bias_table(rpb):
    cidx = np.arange(GRID_W)
    col_start = np.clip(cidx - NA_COLS // 2, 0, GRID_W - NA_COLS)
    col_ok = (cidx[None, :] >= col_start[:, None]) & (cidx[None, :] < col_start[:, None] + NA_COLS)
    pad = GRID_W - NA_COLS
    vec = jnp.pad(rpb.astype(F32), ((0, 0), (0, 0), (pad, pad)))
    tab = jnp.where(col_ok, _toeplitz(vec, GRID_W, GRID_W), NEG)
    nh = rpb.shape[0]
    per_var = [tab[:, v:v + NA_ROWS].transpose(0, 2, 1, 3).reshape(nh, GRID_W, NA_ROWS * GRID_W)
               for v in range(NA_ROWS)]
    return jnp.stack(per_var, axis=1)


def _attn_group(qs, ks, vs, biases):
    ss = [lax.dot_general(q, k, (((1,), (1,)), ((), ())), preferred_element_type=F32) * SCALE + b
          for q, k, b in zip(qs, ks, biases)]
    ms = [jnp.max(s, axis=-1, keepdims=True) for s in ss]
    ps = [jnp.exp(s - m) for s, m in zip(ss, ms)]
    ls = [jnp.sum(p, axis=-1, keepdims=True) for p in ps]
    os = [jnp.dot(p.astype(BF16), v, preferred_element_type=F32) / l for p, v, l in zip(ps, vs, ls)]
    return os, [m + jnp.log(l) for m, l in zip(ms, ls)]


NA_GROUP = 8


def _na_kernel(q_ref, k_ref, v_ref, bias_ref, o_ref, *, rows):
    nkeys = NA_ROWS * GRID_W

    def body(g, carry):
        qs, ks, vs, bs, q0s = [], [], [], [], []
        for i in range(NA_GROUP):
            r = g * NA_GROUP + i
            rs = jnp.clip(r - NA_ROWS // 2, 0, rows - NA_ROWS)
            q0 = pl.multiple_of(r * GRID_W, GRID_W)
            k0 = pl.multiple_of(rs * GRID_W, GRID_W)
            q0s.append(q0)
            qs.append(q_ref[pl.ds(q0, GRID_W), :])
            ks.append(k_ref[pl.ds(k0, nkeys), :])
            vs.append(v_ref[pl.ds(k0, nkeys), :])
            bs.append(bias_ref[rs - r + (NA_ROWS - 1)])
        os, _ = _attn_group(qs, ks, vs, bs)
        for q0, o in zip(q0s, os):
            o_ref[pl.ds(q0, GRID_W), :] = o
        return carry

    lax.fori_loop(0, rows // NA_GROUP, body, 0)


def _na_attn(proj, bias, bsz, seq):
    nh = N_HEADS_NA
    t = bsz * seq
    blk = lambda off: pl.BlockSpec((None, seq, LANES), lambda h, b: (h + off, b, 0))
    return pl.pallas_call(
        functools.partial(_na_kernel, rows=seq // GRID_W),
        out_shape=jax.ShapeDtypeStruct((nh, t, LANES), F32),
        grid=(nh, bsz),
        in_specs=[blk(0), blk(nh), blk(2 * nh),
                  pl.BlockSpec((None, NA_ROWS, GRID_W, NA_ROWS * GRID_W), lambda h, b: (h, 0, 0, 0))],
        out_specs=pl.BlockSpec((None, seq, LANES), lambda h, b: (h, b, 0)),
        compiler_params=_cparams(("arbitrary", "arbitrary")),
        name="na_attn",
    )(proj, proj, proj, bias)


def _t5_bucket(rel):
    nb = T5_BUCKETS // 2
    max_exact = nb // 2
    ret = (rel > 0).astype(np.int32) * nb
    n = np.abs(rel)
    large = max_exact + (np.log(np.maximum(n, 1) / max_exact) / np.log(T5_MAX_DIST / max_exact)
                         * (nb - max_exact)).astype(np.int32)
    large = np.minimum(large, nb - 1)
    return (ret + np.where(n < max_exact, n, large)).astype(np.int32)


def _dil_geometry(sub_len):
    half = DIL_PATTERNS[0][0] // 2
    width = min(sub_len, QBLK + 2 * half)
    nblk = sub_len // QBLK
    starts = [min(max(QBLK * n - half, 0), sub_len - width) for n in range(nblk)]
    offs = sorted({ws - QBLK * n for n, ws in enumerate(starts)}, reverse=True)
    var = [offs.index(ws - QBLK * n) for n, ws in enumerate(starts)]
    return width, starts, offs, var


def _dil_bias_table(t5_table, dil, width, offs):
    half = DIL_PATTERNS[0][0] // 2
    tabs = []
    for off in offs:
        delta = np.arange(QBLK + width - 1) - (QBLK - 1) + off
        onehot = np.eye(T5_BUCKETS, dtype=np.float32)[_t5_bucket(delta * dil)]
        vals = jnp.dot(jnp.asarray(onehot), t5_table.astype(F32), precision=lax.Precision.HIGHEST)
        vec = jnp.where((np.abs(delta) <= half)[:, None], vals, NEG).T
        tabs.append(_toeplitz(vec, QBLK, width))
    return jnp.stack(tabs, axis=1)


DIL_GROUP = 8


def _dil_kernel(q_ref, k_ref, v_ref, b1_ref, b4_ref, b16_ref, o_ref, qf, kf, vf, o4, l4, o16, l16, *, seq):
    qf[...] = q_ref[...].astype(F32)
    kf[...] = k_ref[...].astype(F32)
    vf[...] = v_ref[...].astype(F32)

    for dil, bias_ref, o_s, l_s in ((DIL_PATTERNS[2][1], b16_ref, o16, l16), (DIL_PATTERNS[1][1], b4_ref, o4, l4)):
        width, starts, _, var = _dil_geometry(seq // dil)
        blocks = [(rho, n, ws) for rho in range(dil) for n, ws in enumerate(starts)]
        for g in range(0, len(blocks), DIL_GROUP):
            grp = blocks[g:g + DIL_GROUP]
            qrows = [pl.ds(rho + dil * QBLK * n, QBLK, stride=dil) for rho, n, _ in grp]
            krows = [pl.ds(rho + dil * ws, width, stride=dil) for rho, _, ws in grp]
            os, lses = _attn_group([qf[r, :].astype(BF16) for r in qrows],
                                   [kf[r, :].astype(BF16) for r in krows],
                                   [vf[r, :].astype(BF16) for r in krows],
                                   [bias_ref[var[n]] for _, n, _ in grp])
            for r, o, lse in zip(qrows, os, lses):
                o_s[r, :] = o
                l_s[r, :] = jnp.broadcast_to(lse, (QBLK, LANES))

    width, starts, _, _ = _dil_geometry(seq)
    nblk = len(starts)
    half = DIL_PATTERNS[0][0] // 2

    def body(g, carry):
        rows, krows, bs = [], [], []
        for i in range(DIL_GROUP):
            n = g * DIL_GROUP + i
            ws = pl.multiple_of(jnp.clip(n * QBLK - half, 0, seq - width), half)
            rows.append(pl.ds(pl.multiple_of(n * QBLK, QBLK), QBLK))
            krows.append(pl.ds(ws, width))
            bs.append(b1_ref[jnp.where(n == 0, 0, jnp.where(n == nblk - 1, 2, 1))])
        os, lses = _attn_group([q_ref[r, :] for r in rows], [k_ref[r, :] for r in krows],
                               [v_ref[r, :] for r in krows], bs)
        for r, o1, lse1 in zip(rows, os, lses):
            lse4, lse16 = l4[r, :], l16[r, :]
            mx = jnp.maximum(jnp.maximum(lse4, lse16), lse1)
            e1 = jnp.exp(lse1 - mx)
            e4 = jnp.exp(lse4 - mx)
            e16 = jnp.exp(lse16 - mx)
            o_ref[r, :] = (e1 * o1 + e4 * o4[r, :] + e16 * o16[r, :]) / (e1 + e4 + e16)
        return carry

    lax.fori_loop(0, nblk // DIL_GROUP, body, 0)


def _dil_attn(proj, t5_table, bsz, seq):
    nh = N_HEADS_DIL
    tables = []
    for _, dil in DIL_PATTERNS:
        width, _, offs, var = _dil_geometry(seq // dil)
        tables.append(_dil_bias_table(t5_table, dil, width, offs))
    width, _, offs, var = _dil_geometry(seq)
    assert offs == [0, -(DIL_PATTERNS[0][0] // 2), -DIL_PATTERNS[0][0]] and var[0] == 0 and var[-1] == 2
    first = 3 * N_HEADS_NA
    blk = lambda off: pl.BlockSpec((None, seq, LANES), lambda h, b: (first + off + h, b, 0))
    tab = lambda t: pl.BlockSpec((None,) + t.shape[1:], lambda h, b: (h, 0, 0, 0))
    return pl.pallas_call(
        functools.partial(_dil_kernel, seq=seq),
        out_shape=jax.ShapeDtypeStruct((nh, bsz * seq, LANES), F32),
        grid=(nh, bsz),
        in_specs=[blk(0), blk(nh), blk(2 * nh)] + [tab(t) for t in tables],
        out_specs=pl.BlockSpec((None, seq, LANES), lambda h, b: (h, b, 0)),
        scratch_shapes=[pltpu.VMEM((seq, LANES), F32)] * 7,
        compiler_params=_cparams(("arbitrary", "arbitrary")),
        name="dil_attn",
    )(proj, proj, proj, *tables)


def _mix_kernel(na_ref, dl_ref, gna_ref, gdl_ref, wo_ref, x_ref, gt1_ref, sc2_ref, sh2_ref,
                gpost_ref, gpre_ref, wr_ref, br_ref, x1_ref, hp_ref, lg_ref):
    nh = na_ref.shape[0]
    na = jnp.concatenate([na_ref[h] for h in range(nh)], axis=1)
    dl = jnp.concatenate([dl_ref[h] for h in range(nh)], axis=1)
    lhs = jnp.concatenate([_rms(na, gna_ref[...]), _rms(dl, gdl_ref[...])], axis=1).astype(BF16)
    mixed = jnp.dot(lhs, wo_ref[...], preferred_element_type=F32)
    x1 = x_ref[...] + gt1_ref[...] * _rms(mixed, gpost_ref[...])
    x1_ref[...] = x1
    hf = _rms(x1, gpre_ref[...]) * (1.0 + sc2_ref[...]) + sh2_ref[...]
    hb = hf.astype(BF16)
    lg_ref[...] = jnp.dot(hb, wr_ref[...], preferred_element_type=F32) + br_ref[...]
    half = hb.shape[1] // 2
    lo = lax.bitcast_convert_type(hb[:, :half].astype(F32), U32)
    hi = lax.bitcast_convert_type(hb[:, half:].astype(F32), U32)
    hp_ref[:, 0, :] = (hi & U32(0xFFFF0000)) | lax.shift_right_logical(lo, U32(16))


def _mix_out(out_na, out_dil, g_na, g_dil, wo_bf, x2, gt1, sc2, sh2, g_post, g_pre, wr_bf, br, seq):
    t, d = x2.shape
    nh = out_na.shape[0]
    tm = 256
    wna = g_na.shape[-1]
    row = lambda n: pl.BlockSpec((1, n), lambda i: (0, 0))
    per_b = pl.BlockSpec((None, 1, d), lambda i: (i * tm // seq, 0, 0))
    heads = pl.BlockSpec((nh, tm, LANES), lambda i: (0, i, 0))
    return pl.pallas_call(
        _mix_kernel,
        out_shape=(jax.ShapeDtypeStruct((t, d), F32),
                   jax.ShapeDtypeStruct((t, 1, d // 2), U32),
                   jax.ShapeDtypeStruct((t, LANES), F32)),
        grid=(t // tm,),
        in_specs=[heads, heads, row(wna), row(wna),
                  pl.BlockSpec((d, d), lambda i: (0, 0)),
                  pl.BlockSpec((tm, d), lambda i: (i, 0)),
                  per_b, per_b, per_b, row(d), row(d),
                  pl.BlockSpec((d, LANES), lambda i: (0, 0)), row(LANES)],
        out_specs=(pl.BlockSpec((tm, d), lambda i: (i, 0)),
                   pl.BlockSpec((tm, 1, d // 2), lambda i: (i, 0, 0)),
                   pl.BlockSpec((tm, LANES), lambda i: (i, 0))),
        compiler_params=_cparams(("arbitrary",)),
        name="mix_out",
    )(out_na, out_dil, g_na.reshape(1, wna), g_dil.reshape(1, wna), wo_bf, x2, gt1, sc2, sh2,
      g_post.reshape(1, d), g_pre.reshape(1, d), wr_bf, br)


def _route_kernel(lg_ref, idx_ref, rank_ref, gate_ref, cnt_ref, carry):
    step = pl.program_id(0)

    @pl.when(step == 0)
    def _():
        carry[...] = jnp.zeros_like(carry)

    v = lg_ref[...]
    ch = v.shape[0]
    lane = lax.broadcasted_iota(I32, v.shape, 1).astype(F32)
    vals, idxs = [], []
    for _ in range(TOP_K):
        m = jnp.max(v, axis=1, keepdims=True)
        ik = jnp.min(jnp.where(v == m, lane, float(LANES)), axis=1, keepdims=True)
        vals.append(m)
        idxs.append(ik)
        v = jnp.where(lane == ik, -jnp.inf, v)
    es = [jnp.exp(val - vals[0]) for val in vals]
    den = es[0] + es[1] + es[2] + es[3]
    sel = jnp.zeros(v.shape, F32)
    for ik in idxs:
        sel = jnp.where(lane == ik, 1.0, sel)
    ti = lax.broadcasted_iota(I32, (ch, ch), 0)
    tj = lax.broadcasted_iota(I32, (ch, ch), 1)
    lower = jnp.where(tj < ti, 1.0, 0.0).astype(BF16)
    cum = jnp.dot(lower, sel.astype(BF16), preferred_element_type=F32) + carry[0:1, :]
    idx_o = jnp.zeros(v.shape, F32)
    rank_o = jnp.zeros(v.shape, F32)
    gate_o = jnp.zeros(v.shape, F32)
    for k in range(TOP_K):
        rk = jnp.sum(jnp.where(lane == idxs[k], cum, 0.0), axis=1, keepdims=True)
        idx_o = jnp.where(lane == float(k), idxs[k], idx_o)
        rank_o = jnp.where(lane == float(k), rk, rank_o)
        gate_o = jnp.where(lane == float(k), es[k] / den, gate_o)
    idx_ref[...] = idx_o.astype(I32)
    rank_ref[...] = rank_o.astype(I32)
    gate_ref[...] = gate_o
    total = carry[...] + jnp.sum(sel, axis=0, keepdims=True)
    carry[...] = total
    cnt_ref[...] = total.astype(I32)


def _route(logits):
    t = logits.shape[0]
    ch = 512
    blk = pl.BlockSpec((ch, LANES), lambda i: (i, 0))
    return pl.pallas_call(
        _route_kernel,
        out_shape=(jax.ShapeDtypeStruct((t, LANES), I32),
                   jax.ShapeDtypeStruct((t, LANES), I32),
                   jax.ShapeDtypeStruct((t, LANES), F32),
                   jax.ShapeDtypeStruct((8, LANES), I32)),
        grid=(t // ch,),
        in_specs=[blk],
        out_specs=(blk, blk, blk, pl.BlockSpec((8, LANES), lambda i: (0, 0))),
        scratch_shapes=[pltpu.VMEM((8, LANES), F32)],
        compiler_params=_cparams(("arbitrary",)),
        name="route",
    )(logits)


def _schedule(counts, n_assign):
    max_sb = n_assign // SUPER_ROWS + N_EXPERTS
    nsb_e = (counts + SUPER_ROWS - 1) // SUPER_ROWS
    sb_end = jnp.cumsum(nsb_e)
    sb_start = sb_end - nsb_e
    row_start = (sb_start * SUPER_ROWS).astype(I32)
    nsb = sb_end[-1].astype(I32)
    s = jnp.arange(max_sb, dtype=I32)
    sb_e = jnp.minimum(jnp.searchsorted(sb_end, s, side="right"), N_EXPERTS - 1).astype(I32)
    rem = counts[sb_e] - (s - sb_start[sb_e]) * SUPER_ROWS
    sb_nb = jnp.where(s < nsb, (jnp.clip(rem, 0, SUPER_ROWS) + ROW_BLK - 1) // ROW_BLK, 0).astype(I32)
    zero_start = (s * SUPER_ROWS + jnp.maximum(sb_nb - 1, 0) * ROW_BLK).astype(I32)
    return max_sb, row_start, nsb.reshape(1), sb_e, sb_nb, zero_start


ROW_UNROLL = 8


def _dispatch_kernel(dest_s, zs_s, nsb_s, hp_ref, xs_hbm, zbuf, zsem, sem):
    step = pl.program_id(0)
    tt = hp_ref.shape[0]

    def zero_copy(s):
        return pltpu.make_async_copy(zbuf, xs_hbm.at[pl.ds(pl.multiple_of(zs_s[s], ROW_BLK), ROW_BLK)], zsem)

    @pl.when(step == 0)
    def _():
        zbuf[...] = jnp.zeros_like(zbuf)

        def zstart(s, c):
            zero_copy(s).start()
            return c

        def zwait(s, c):
            zero_copy(s).wait()
            return c

        lax.fori_loop(0, nsb_s[0], zstart, 0)
        lax.fori_loop(0, nsb_s[0], zwait, 0)

    base = step * tt * TOP_K

    def issue(jj, c):
        for u in range(ROW_UNROLL):
            j = jj * ROW_UNROLL + u
            for k in range(TOP_K):
                d = dest_s[base + j * TOP_K + k]
                pltpu.make_async_copy(hp_ref.at[pl.ds(j, 1)], xs_hbm.at[pl.ds(d, 1)], sem).start()
        return c

    lax.fori_loop(0, tt // ROW_UNROLL, issue, 0)
    for k in range(TOP_K):
        pltpu.make_async_copy(hp_ref, xs_hbm.at[pl.ds(0, tt)], sem).wait()


def _dispatch(dest, zero_start, nsb, hp, n_rows):
    t, _, w = hp.shape
    tt = 256
    return pl.pallas_call(
        _dispatch_kernel,
        out_shape=jax.ShapeDtypeStruct((n_rows, 1, w), U32),
        grid_spec=pltpu.PrefetchScalarGridSpec(
            num_scalar_prefetch=3,
            grid=(t // tt,),
            in_specs=[pl.BlockSpec((tt, 1, w), lambda i, *_: (i, 0, 0))],
            out_specs=pl.BlockSpec(memory_space=pl.ANY),
            scratch_shapes=[pltpu.VMEM((ROW_BLK, 1, w), U32),
                            pltpu.SemaphoreType.DMA(()), pltpu.SemaphoreType.DMA(())]),
        compiler_params=_cparams(("arbitrary",)),
        name="dispatch",
    )(dest, zero_start, nsb, hp)


def _row_block(r):
    return pl.ds(pl.multiple_of(r * ROW_BLK, ROW_BLK), ROW_BLK)


def _moe_up_kernel(sbe_s, sbn_s, x_ref, win_ref, bin_ref, h_ref, winb, zbuf, xstage):
    nb = sbn_s[pl.program_id(0)]
    half = x_ref.shape[2]
    nsub = win_ref.shape[1] // (2 * LANES)
    lane = lax.broadcasted_iota(I32, (ROW_BLK, LANES), 1)
    even = (2 * lane) & (LANES - 1)
    odd = even + 1
    first_half = lane < LANES // 2

    def unpack(r):
        xstage[...] = x_ref[_row_block(r), 0, :]
        xu = xstage[...]
        xa = lax.bitcast_convert_type(lax.shift_left(xu, U32(16)), F32).astype(BF16)
        xb = lax.bitcast_convert_type(xu & U32(0xFFFF0000), F32).astype(BF16)
        return xa, xb

    def matmul(r):
        xa, xb = unpack(r)
        zbuf[...] = (jnp.dot(xa, winb[0:half, :], preferred_element_type=F32)
                     + jnp.dot(xb, winb[half:2 * half, :], preferred_element_type=F32) + bin_ref[...])

    def matmul_casting(r):
        xa, xb = unpack(r)
        kq = half // 2
        acc = bin_ref[...]
        for ks in range(4):
            wb = win_ref[ks * kq:(ks + 1) * kq, :].astype(BF16)
            winb[ks * kq:(ks + 1) * kq, :] = wb
            xpart = (xa, xb)[ks // 2][:, (ks % 2) * kq:(ks % 2 + 1) * kq]
            acc = acc + jnp.dot(xpart, wb, preferred_element_type=F32)
        zbuf[...] = acc

    def activation():
        hs = []
        for u in range(nsub):
            za = zbuf[:, (2 * u) * LANES:(2 * u + 1) * LANES]
            zb = zbuf[:, (2 * u + 1) * LANES:(2 * u + 2) * LANES]
            gate = jnp.where(first_half, jnp.take_along_axis(za, even, axis=1),
                             jnp.take_along_axis(zb, even, axis=1))
            up = jnp.where(first_half, jnp.take_along_axis(za, odd, axis=1),
                           jnp.take_along_axis(zb, odd, axis=1))
            gate = jnp.minimum(gate, SWIGLU_LIMIT)
            up = jnp.clip(up, -SWIGLU_LIMIT, SWIGLU_LIMIT)
            glu = gate * jax.nn.sigmoid(SWIGLU_ALPHA * gate)
            hs.append(((up + 1.0) * glu).astype(BF16))
        return jnp.concatenate(hs, axis=1)

    matmul_casting(0)

    def step(r, carry):
        h = activation()
        matmul(r + 1)
        h_ref[_row_block(r), :] = h
        return carry

    lax.fori_loop(0, nb - 1, step, 0)
    h_ref[_row_block(nb - 1), :] = activation()


def _moe_down_kernel(sbe_s, sbn_s, h_ref, wout_ref, bout_ref, o_ref, woutb):
    nb = sbn_s[pl.program_id(0)]
    kq = h_ref.shape[1] // 4
    h0 = h_ref[_row_block(0), :]
    acc = bout_ref[...]
    for ks in range(4):
        wb = wout_ref[ks * kq:(ks + 1) * kq, :].astype(BF16)
        woutb[ks * kq:(ks + 1) * kq, :] = wb
        acc = acc + jnp.dot(h0[:, ks * kq:(ks + 1) * kq], wb, preferred_element_type=F32)
    o_ref[_row_block(0), :] = acc

    def project(r):
        return jnp.dot(h_ref[_row_block(r), :], woutb[...], preferred_element_type=F32) + bout_ref[...]

    def pair(p, carry):
        r = 1 + 2 * p
        y0, y1 = project(r), project(r + 1)
        o_ref[_row_block(r), :] = y0
        o_ref[_row_block(r + 1), :] = y1
        return carry

    lax.fori_loop(0, (nb - 1) // 2, pair, 0)

    @pl.when((nb - 1) % 2 == 1)
    def _():
        o_ref[_row_block(nb - 1), :] = project(nb - 1)


def _moe_ffn(sb_e, sb_nb, nsb, xs, w_in, b_in, w_out, b_out, max_sb):
    ne, d, f2 = w_in.shape
    ff = w_out.shape[1]
    half = xs.shape[2]
    rows = max_sb * SUPER_ROWS
    sem = ("arbitrary", "arbitrary")

    h = pl.pallas_call(
        _moe_up_kernel,
        out_shape=jax.ShapeDtypeStruct((rows, ff), BF16),
        grid_spec=pltpu.PrefetchScalarGridSpec(
            num_scalar_prefetch=2,
            grid=(nsb[0], ff // UP_CHUNK),
            in_specs=[pl.BlockSpec((SUPER_ROWS, 1, half), lambda s, j, sbe, sbn: (s, 0, 0)),
                      pl.BlockSpec((None, d, 2 * UP_CHUNK), lambda s, j, sbe, sbn: (sbe[s], 0, j)),
                      pl.BlockSpec((None, 1, 2 * UP_CHUNK), lambda s, j, sbe, sbn: (sbe[s], 0, j))],
            out_specs=pl.BlockSpec((SUPER_ROWS, UP_CHUNK), lambda s, j, sbe, sbn: (s, j)),
            scratch_shapes=[pltpu.VMEM((d, 2 * UP_CHUNK), BF16), pltpu.VMEM((ROW_BLK, 2 * UP_CHUNK), F32),
                            pltpu.VMEM((ROW_BLK, half), U32)]),
        compiler_params=_cparams(sem),
        name="moe_up",
    )(sb_e, sb_nb, xs, w_in, b_in.reshape(ne, 1, f2))

    return pl.pallas_call(
        _moe_down_kernel,
        out_shape=jax.ShapeDtypeStruct((rows, d), F32),
        grid_spec=pltpu.PrefetchScalarGridSpec(
            num_scalar_prefetch=2,
            grid=(nsb[0], d // DOWN_CHUNK),
            in_specs=[pl.BlockSpec((SUPER_ROWS, ff), lambda s, n, sbe, sbn: (s, 0)),
                      pl.BlockSpec((None, ff, DOWN_CHUNK), lambda s, n, sbe, sbn: (sbe[s], 0, n)),
                      pl.BlockSpec((None, 1, DOWN_CHUNK), lambda s, n, sbe, sbn: (sbe[s], 0, n))],
            out_specs=pl.BlockSpec((SUPER_ROWS, DOWN_CHUNK), lambda s, n, sbe, sbn: (s, n)),
            scratch_shapes=[pltpu.VMEM((ff, DOWN_CHUNK), BF16)]),
        compiler_params=_cparams(sem),
        name="moe_down",
    )(sb_e, sb_nb, h, w_out, b_out.reshape(ne, 1, d))


def _combine_kernel(dest_s, gate_ref, x1_ref, gt2_ref, g_ref, ys_hbm, o_ref, buf, sem):
    step = pl.program_id(0)
    nstep = pl.num_programs(0)
    tt = x1_ref.shape[0]

    def fetch(tile, slot):
        base = tile * tt * TOP_K

        def issue(jj, c):
            for u in range(ROW_UNROLL):
                j = jj * ROW_UNROLL + u
                for k in range(TOP_K):
                    d = dest_s[base + j * TOP_K + k]
                    pltpu.make_async_copy(ys_hbm.at[pl.ds(d, 1)], buf.at[slot, k, pl.ds(j, 1)],
                                          sem.at[slot]).start()
            return c

        lax.fori_loop(0, tt // ROW_UNROLL, issue, 0)

    @pl.when(step == 0)
    def _():
        fetch(0, 0)

    @pl.when(step + 1 < nstep)
    def _():
        fetch(step + 1, (step + 1) % 2)

    slot = step % 2
    for k in range(TOP_K):
        pltpu.make_async_copy(ys_hbm.at[pl.ds(0, tt)], buf.at[slot, k], sem.at[slot]).wait()

    g = gate_ref[...]
    y = buf[slot, 0] * g[:, 0:1]
    for k in range(1, TOP_K):
        y = y + buf[slot, k] * g[:, k:k + 1]
    o_ref[...] = x1_ref[...] + gt2_ref[...] * _rms(y, g_ref[...])


def _combine(dest, gates, x1, gt2, g_post, ys, seq):
    t, d = x1.shape
    tt = 128
    return pl.pallas_call(
        _combine_kernel,
        out_shape=jax.ShapeDtypeStruct((t, d), F32),
        grid_spec=pltpu.PrefetchScalarGridSpec(
            num_scalar_prefetch=1,
            grid=(t // tt,),
            in_specs=[pl.BlockSpec((tt, LANES), lambda i, *_: (i, 0)),
                      pl.BlockSpec((tt, d), lambda i, *_: (i, 0)),
                      pl.BlockSpec((None, 1, d), lambda i, *_: (i * tt // seq, 0, 0)),
                      pl.BlockSpec((1, d), lambda i, *_: (0, 0)),
                      pl.BlockSpec(memory_space=pl.ANY)],
            out_specs=pl.BlockSpec((tt, d), lambda i, *_: (i, 0)),
            scratch_shapes=[pltpu.VMEM((2, TOP_K, tt, d), F32), pltpu.SemaphoreType.DMA((2,))]),
        compiler_params=_cparams(("arbitrary",)),
        name="combine",
    )(dest, gates, x1, gt2, g_post.reshape(1, d), ys)


def kernel(x, c, w_ada, b_ada, g_pre_mix, g_post_mix, w_in, rpb_na, t5_table, g_out_na, g_out_dil, w_o,
           g_pre_ffn, g_post_ffn, w_router, b_router, w_e_in, b_e_in, w_e_out, b_e_out):
    bsz, seq, d = x.shape
    t = bsz * seq
    for l in range(w_ada.shape[0]):
        mod = _ada_mod(c, w_ada[l], b_ada[l])
        sh1, sc1, gt1, sh2, sc2, gt2 = (m.reshape(bsz, 1, d) for m in jnp.split(mod, 6, axis=-1))
        x2 = x.reshape(t, d)

        proj = _qkv_proj(x2, g_pre_mix[l], sc1, sh1, w_in[l].astype(BF16), seq)
        out_na = _na_attn(proj, _na_bias_table(rpb_na[l]), bsz, seq)
        out_dil = _dil_attn(proj, t5_table, bsz, seq)

        ne = w_router.shape[-1]
        wr = jnp.zeros((d, LANES), BF16).at[:, :ne].set(w_router[l].astype(BF16))
        br = jnp.full((1, LANES), NEG, F32).at[0, :ne].set(b_router[l])
        x1, hp, logits = _mix_out(out_na, out_dil, g_out_na[l], g_out_dil[l], w_o[l].astype(BF16), x2,
                                  gt1, sc2, sh2, g_post_mix[l], g_pre_ffn[l], wr, br, seq)

        idx, rank, gates, cnt = _route(logits)
        max_sb, row_start, nsb, sb_e, sb_nb, zero_start = _schedule(cnt[0, :ne], t * TOP_K)
        eidx = idx[:, :TOP_K].reshape(-1)
        onehot = eidx[:, None] == jnp.arange(ne, dtype=I32)[None, :]
        dest = rank[:, :TOP_K].reshape(-1) + jnp.sum(jnp.where(onehot, row_start[None, :], 0), axis=1)
        xs = _dispatch(dest, zero_start, nsb, hp, max_sb * SUPER_ROWS)
        ys = _moe_ffn(sb_e, sb_nb, nsb, xs, w_e_in[l], b_e_in[l], w_e_out[l], b_e_out[l], max_sb)
        x = _combine(dest, gates, x1, gt2, g_post_ffn[l], ys, seq).reshape(bsz, seq, d)
    return x
```

```python
import functools

import numpy as np
import jax
import jax.numpy as jnp
from jax import lax
from jax.experimental import pallas as pl
from jax.experimental.pallas import tpu as pltpu

F32 = jnp.float32
BF16 = jnp.bfloat16
U32 = jnp.uint32
I32 = jnp.int32

HEAD_DIM = 128
N_HEADS_NA = 8
N_HEADS_DIL = 8
GRID_W = 64
NA_ROWS = 8
NA_COLS = 16
DIL_PATTERNS = ((128, 1), (512, 4), (2048, 16))
T5_BUCKETS = 32
T5_MAX_DIST = 1024
N_EXPERTS = 32
TOP_K = 4
SWIGLU_LIMIT = 7.0
SWIGLU_ALPHA = 1.702
EPS = 1e-6
NEG = -1e30
SCALE = HEAD_DIM ** -0.5

LANES = 128
QBLK = 128
ROW_BLK = 256
SUPER_BLKS = 7
SUPER_ROWS = ROW_BLK * SUPER_BLKS
UP_CHUNK = 512
DOWN_CHUNK = 1024
VMEM_LIMIT = 56 * 1024 * 1024


def _cparams(sem, vmem=VMEM_LIMIT):
    return pltpu.CompilerParams(dimension_semantics=sem, vmem_limit_bytes=vmem)


def _rms(x, g):
    return x * lax.rsqrt(jnp.mean(x * x, axis=-1, keepdims=True) + EPS) * g


def _ada_kernel(c_ref, w_ref, b_ref, o_ref):
    c = c_ref[...]
    s = c * jax.nn.sigmoid(c)
    o_ref[...] = jnp.dot(s.astype(BF16), w_ref[...].astype(BF16),
                         preferred_element_type=F32) + b_ref[...]


def _ada_mod(c, w, b):
    bsz, d = c.shape
    n = w.shape[1]
    tn = 1024
    cp = jnp.zeros((8, d), F32).at[:bsz].set(c)
    out = pl.pallas_call(
        _ada_kernel,
        out_shape=jax.ShapeDtypeStruct((8, n), F32),
        grid=(n // tn,),
        in_specs=[pl.BlockSpec((8, d), lambda j: (0, 0)),
                  pl.BlockSpec((d, tn), lambda j: (0, j)),
                  pl.BlockSpec((1, tn), lambda j: (0, j))],
        out_specs=pl.BlockSpec((8, tn), lambda j: (0, j)),
        compiler_params=_cparams(("arbitrary",)),
        name="ada_mod",
    )(cp, w, b.reshape(1, n))
    return out[:bsz]


def _qkv_kernel(x_ref, g_ref, sc_ref, sh_ref, w_ref, o_ref, h_scr):
    @pl.when(pl.program_id(1) == 0)
    def _():
        h = _rms(x_ref[...], g_ref[...]) * (1.0 + sc_ref[...]) + sh_ref[...]
        h_scr[...] = h.astype(BF16)

    acc = jnp.dot(h_scr[...], w_ref[...], preferred_element_type=F32)
    for u in range(o_ref.shape[0]):
        o_ref[u] = acc[:, u * LANES:(u + 1) * LANES].astype(BF16)


def _qkv_proj(x2, g, sc, sh, w_bf, seq):
    t, d = x2.shape
    n = w_bf.shape[1]
    tm, tn = 1024, 1024
    return pl.pallas_call(
        _qkv_kernel,
        out_shape=jax.ShapeDtypeStruct((n // LANES, t, LANES), BF16),
        grid=(t // tm, n // tn),
        in_specs=[pl.BlockSpec((tm, d), lambda i, j: (i, 0)),
                  pl.BlockSpec((1, d), lambda i, j: (0, 0)),
                  pl.BlockSpec((None, 1, d), lambda i, j: (i * tm // seq, 0, 0)),
                  pl.BlockSpec((None, 1, d), lambda i, j: (i * tm // seq, 0, 0)),
                  pl.BlockSpec((d, tn), lambda i, j: (0, j))],
        out_specs=pl.BlockSpec((tn // LANES, tm, LANES), lambda i, j: (j, i, 0)),
        scratch_shapes=[pltpu.VMEM((tm, d), BF16)],
        compiler_params=_cparams(("arbitrary", "arbitrary")),
        name="qkv_proj",
    )(x2, g.reshape(1, d), sc, sh, w_bf)


def _toeplitz(vec, rows, cols):
    n = rows + cols - 1
    assert vec.shape[-1] == n
    lead = vec.shape[:-1]
    ext = jnp.concatenate([vec, jnp.zeros(lead + (1,), vec.dtype)], axis=-1)
    flat = jnp.broadcast_to(ext[..., None, :], lead + (rows, n + 1)).reshape(lead + (rows * (n + 1),))
    skew = flat[..., :rows * n].reshape(lead + (rows, n))
    return skew[..., rows - 1:rows - 1 + cols]


def _na_bias_table(rpb):
    cidx = np.arange(GRID_W)
    col_start = np.clip(cidx - NA_COLS // 2, 0, GRID_W - NA_COLS)
    col_ok = (cidx[None, :] >= col_start[:, None]) & (cidx[None, :] < col_start[:, None] + NA_COLS)
    pad = GRID_W - NA_COLS
    vec = jnp.pad(rpb.astype(F32), ((0, 0), (0, 0), (pad, pad)))
    tab = jnp.where(col_ok, _toeplitz(vec, GRID_W, GRID_W), NEG)
    nh = rpb.shape[0]
    per_var = [tab[:, v:v + NA_ROWS].transpose(0, 2, 1, 3).reshape(nh, GRID_W, NA_ROWS * GRID_W)
               for v in range(NA_ROWS)]
    return jnp.stack(per_var, axis=1)


def _attn_group(qs, ks, vs, biases):
    ss = [lax.dot_general(q, k, (((1,), (1,)), ((), ())), preferred_element_type=F32) * SCALE + b
          for q, k, b in zip(qs, ks, biases)]
    ms = [jnp.max(s, axis=-1, keepdims=True) for s in ss]
    ps = [jnp.exp(s - m) for s, m in zip(ss, ms)]
    ls = [jnp.sum(p, axis=-1, keepdims=True) for p in ps]
    os = [jnp.dot(p.astype(BF16), v, preferred_element_type=F32) / l for p, v, l in zip(ps, vs, ls)]
    return os, [m + jnp.log(l) for m, l in zip(ms, ls)]


NA_GROUP = 8


def _na_kernel(q_ref, k_ref, v_ref, bias_ref, o_ref, *, rows):
    nkeys = NA_ROWS * GRID_W

    def body(g, carry):
        qs, ks, vs, bs, q0s = [], [], [], [], []
        for i in range(NA_GROUP):
            r = g * NA_GROUP + i
            rs = jnp.clip(r - NA_ROWS // 2, 0, rows - NA_ROWS)
            q0 = pl.multiple_of(r * GRID_W, GRID_W)
            k0 = pl.multiple_of(rs * GRID_W, GRID_W)
            q0s.append(q0)
            qs.append(q_ref[pl.ds(q0, GRID_W), :])
            ks.append(k_ref[pl.ds(k0, nkeys), :])
            vs.append(v_ref[pl.ds(k0, nkeys), :])
            bs.append(bias_ref[rs - r + (NA_ROWS - 1)])
        os, _ = _attn_group(qs, ks, vs, bs)
        for q0, o in zip(q0s, os):
            o_ref[pl.ds(q0, GRID_W), :] = o
        return carry

    lax.fori_loop(0, rows // NA_GROUP, body, 0)


def _na_attn(proj, bias, bsz, seq):
    nh = N_HEADS_NA
    t = bsz * seq
    blk = lambda off: pl.BlockSpec((None, seq, LANES), lambda h, b: (h + off, b, 0))
    return pl.pallas_call(
        functools.partial(_na_kernel, rows=seq // GRID_W),
        out_shape=jax.ShapeDtypeStruct((nh, t, LANES), F32),
        grid=(nh, bsz),
        in_specs=[blk(0), blk(nh), blk(2 * nh),
                  pl.BlockSpec((None, NA_ROWS, GRID_W, NA_ROWS * GRID_W), lambda h, b: (h, 0, 0, 0))],
        out_specs=pl.BlockSpec((None, seq, LANES), lambda h, b: (h, b, 0)),
        compiler_params=_cparams(("arbitrary", "arbitrary")),
        name="na_attn",
    )(proj, proj, proj, bias)


def _t5_bucket(rel):
    nb = T5_BUCKETS // 2
    max_exact = nb // 2
    ret = (rel > 0).astype(np.int32) * nb
    n = np.abs(rel)
    large = max_exact + (np.log(np.maximum(n, 1) / max_exact) / np.log(T5_MAX_DIST / max_exact)
                         * (nb - max_exact)).astype(np.int32)
    large = np.minimum(large, nb - 1)
    return (ret + np.where(n < max_exact, n, large)).astype(np.int32)


def _dil_geometry(sub_len):
    half = DIL_PATTERNS[0][0] // 2
    width = min(sub_len, QBLK + 2 * half)
    nblk = sub_len // QBLK
    starts = [min(max(QBLK * n - half, 0), sub_len - width) for n in range(nblk)]
    offs = sorted({ws - QBLK * n for n, ws in enumerate(starts)}, reverse=True)
    var = [offs.index(ws - QBLK * n) for n, ws in enumerate(starts)]
    return width, starts, offs, var


def _dil_bias_table(t5_table, dil, width, offs):
    half = DIL_PATTERNS[0][0] // 2
    tabs = []
    for off in offs:
        delta = np.arange(QBLK + width - 1) - (QBLK - 1) + off
        onehot = np.eye(T5_BUCKETS, dtype=np.float32)[_t5_bucket(delta * dil)]
        vals = jnp.dot(jnp.asarray(onehot), t5_table.astype(F32), precision=lax.Precision.HIGHEST)
        vec = jnp.where((np.abs(delta) <= half)[:, None], vals, NEG).T
        tabs.append(_toeplitz(vec, QBLK, width))
    return jnp.stack(tabs, axis=1)


DIL_GROUP = 8


def _dil_kernel(q_ref, k_ref, v_ref, b1_ref, b4_ref, b16_ref, o_ref, qf, kf, vf, o4, l4, o16, l16, *, seq):
    qf[...] = q_ref[...].astype(F32)
    kf[...] = k_ref[...].astype(F32)
    vf[...] = v_ref[...].astype(F32)

    for dil, bias_ref, o_s, l_s in ((DIL_PATTERNS[2][1], b16_ref, o16, l16), (DIL_PATTERNS[1][1], b4_ref, o4, l4)):
        width, starts, _, var = _dil_geometry(seq // dil)
        blocks = [(rho, n, ws) for rho in range(dil) for n, ws in enumerate(starts)]
        for g in range(0, len(blocks), DIL_GROUP):
            grp = blocks[g:g + DIL_GROUP]
            qrows = [pl.ds(rho + dil * QBLK * n, QBLK, stride=dil) for rho, n, _ in grp]
            krows = [pl.ds(rho + dil * ws, width, stride=dil) for rho, _, ws in grp]
            os, lses = _attn_group([qf[r, :].astype(BF16) for r in qrows],
                                   [kf[r, :].astype(BF16) for r in krows],
                                   [vf[r, :].astype(BF16) for r in krows],
                                   [bias_ref[var[n]] for _, n, _ in grp])
            for r, o, lse in zip(qrows, os, lses):
                o_s[r, :] = o
                l_s[r, :] = jnp.broadcast_to(lse, (QBLK, LANES))

    width, starts, _, _ = _dil_geometry(seq)
    nblk = len(starts)
    half = DIL_PATTERNS[0][0] // 2

    def body(g, carry):
        rows, krows, bs = [], [], []
        for i in range(DIL_GROUP):
            n = g * DIL_GROUP + i
            ws = pl.multiple_of(jnp.clip(n * QBLK - half, 0, seq - width), half)
            rows.append(pl.ds(pl.multiple_of(n * QBLK, QBLK), QBLK))
            krows.append(pl.ds(ws, width))
            bs.append(b1_ref[jnp.where(n == 0, 0, jnp.where(n == nblk - 1, 2, 1))])
        os, lses = _attn_group([q_ref[r, :] for r in rows], [k_ref[r, :] for r in krows],
                               [v_ref[r, :] for r in krows], bs)
        for r, o1, lse1 in zip(rows, os, lses):
            lse4, lse16 = l4[r, :], l16[r, :]
            mx = jnp.maximum(jnp.maximum(lse4, lse16), lse1)
            e1 = jnp.exp(lse1 - mx)
            e4 = jnp.exp(lse4 - mx)
            e16 = jnp.exp(lse16 - mx)
            o_ref[r, :] = (e1 * o1 + e4 * o4[r, :] + e16 * o16[r, :]) / (e1 + e4 + e16)
        return carry

    lax.fori_loop(0, nblk // DIL_GROUP, body, 0)


def _dil_attn(proj, t5_table, bsz, seq):
    nh = N_HEADS_DIL
    tables = []
    for _, dil in DIL_PATTERNS:
        width, _, offs, var = _dil_geometry(seq // dil)
        tables.append(_dil_bias_table(t5_table, dil, width, offs))
    width, _, offs, var = _dil_geometry(seq)
    assert offs == [0, -(DIL_PATTERNS[0][0] // 2), -DIL_PATTERNS[0][0]] and var[0] == 0 and var[-1] == 2
    first = 3 * N_HEADS_NA
    blk = lambda off: pl.BlockSpec((None, seq, LANES), lambda h, b: (first + off + h, b, 0))
    tab = lambda t: pl.BlockSpec((None,) + t.shape[1:], lambda h, b: (h, 0, 0, 0))
    return pl.pallas_call(
        functools.partial(_dil_kernel, seq=seq),
        out_shape=jax.ShapeDtypeStruct((nh, bsz * seq, LANES), F32),
        grid=(nh, bsz),
        in_specs=[blk(0), blk(nh), blk(2 * nh)] + [tab(t) for t in tables],
        out_specs=pl.BlockSpec((None, seq, LANES), lambda h, b: (h, b, 0)),
        scratch_shapes=[pltpu.VMEM((seq, LANES), F32)] * 7,
        compiler_params=_cparams(("arbitrary", "arbitrary")),
        name="dil_attn",
    )(proj, proj, proj, *tables)


MIX_SUB = 256


def _mix_kernel(na_ref, dl_ref, gna_ref, gdl_ref, wo_ref, x_ref, gt1_ref, sc2_ref, sh2_ref,
                gpost_ref, gpre_ref, wr_ref, br_ref, x1_ref, hp_ref, lg_ref):
    nh = na_ref.shape[0]
    tm = x_ref.shape[0]
    subs = [slice(i * MIX_SUB, (i + 1) * MIX_SUB) for i in range(tm // MIX_SUB)]
    lhs = []
    for r in subs:
        na = jnp.concatenate([na_ref[h, r, :] for h in range(nh)], axis=1)
        dl = jnp.concatenate([dl_ref[h, r, :] for h in range(nh)], axis=1)
        lhs.append(jnp.concatenate([_rms(na, gna_ref[...]), _rms(dl, gdl_ref[...])], axis=1).astype(BF16))
    mixed = [jnp.dot(a, wo_ref[...], preferred_element_type=F32) for a in lhs]
    for r, m in zip(subs, mixed):
        x1 = x_ref[r, :] + gt1_ref[...] * _rms(m, gpost_ref[...])
        x1_ref[r, :] = x1
        hf = _rms(x1, gpre_ref[...]) * (1.0 + sc2_ref[...]) + sh2_ref[...]
        hb = hf.astype(BF16)
        lg_ref[r, :] = jnp.dot(hb, wr_ref[...], preferred_element_type=F32) + br_ref[...]
        half = hb.shape[1] // 2
        lo = lax.bitcast_convert_type(hb[:, :half].astype(F32), U32)
        hi = lax.bitcast_convert_type(hb[:, half:].astype(F32), U32)
        hp_ref[r, 0, :] = (hi & U32(0xFFFF0000)) | lax.shift_right_logical(lo, U32(16))


def _mix_out(out_na, out_dil, g_na, g_dil, wo_bf, x2, gt1, sc2, sh2, g_post, g_pre, wr_bf, br, seq):
    t, d = x2.shape
    nh = out_na.shape[0]
    tm = 512
    wna = g_na.shape[-1]
    row = lambda n: pl.BlockSpec((1, n), lambda i: (0, 0))
    per_b = pl.BlockSpec((None, 1, d), lambda i: (i * tm // seq, 0, 0))
    heads = pl.BlockSpec((nh, tm, LANES), lambda i: (0, i, 0))
    return pl.pallas_call(
        _mix_kernel,
        out_shape=(jax.ShapeDtypeStruct((t, d), F32),
                   jax.ShapeDtypeStruct((t, 1, d // 2), U32),
                   jax.ShapeDtypeStruct((t, LANES), F32)),
        grid=(t // tm,),
        in_specs=[heads, heads, row(wna), row(wna),
                  pl.BlockSpec((d, d), lambda i: (0, 0)),
                  pl.BlockSpec((tm, d), lambda i: (i, 0)),
                  per_b, per_b, per_b, row(d), row(d),
                  pl.BlockSpec((d, LANES), lambda i: (0, 0)), row(LANES)],
        out_specs=(pl.BlockSpec((tm, d), lambda i: (i, 0)),
                   pl.BlockSpec((tm, 1, d // 2), lambda i: (i, 0, 0)),
                   pl.BlockSpec((tm, LANES), lambda i: (i, 0))),
        compiler_params=_cparams(("arbitrary",)),
        name="mix_out",
    )(out_na, out_dil, g_na.reshape(1, wna), g_dil.reshape(1, wna), wo_bf, x2, gt1, sc2, sh2,
      g_post.reshape(1, d), g_pre.reshape(1, d), wr_bf, br)


def _route_kernel(lg_ref, idx_ref, rank_ref, gate_ref, cnt_ref, carry):
    step = pl.program_id(0)

    @pl.when(step == 0)
    def _():
        carry[...] = jnp.zeros_like(carry)

    v = lg_ref[...]
    ch = v.shape[0]
    lane = lax.broadcasted_iota(I32, v.shape, 1).astype(F32)
    vals, idxs = [], []
    for _ in range(TOP_K):
        m = jnp.max(v, axis=1, keepdims=True)
        ik = jnp.min(jnp.where(v == m, lane, float(LANES)), axis=1, keepdims=True)
        vals.append(m)
        idxs.append(ik)
        v = jnp.where(lane == ik, -jnp.inf, v)
    es = [jnp.exp(val - vals[0]) for val in vals]
    den = es[0] + es[1] + es[2] + es[3]
    sel = jnp.zeros(v.shape, F32)
    for ik in idxs:
        sel = jnp.where(lane == ik, 1.0, sel)
    ti = lax.broadcasted_iota(I32, (ch, ch), 0)
    tj = lax.broadcasted_iota(I32, (ch, ch), 1)
    lower = jnp.where(tj < ti, 1.0, 0.0).astype(BF16)
    cum = jnp.dot(lower, sel.astype(BF16), preferred_element_type=F32) + carry[0:1, :]
    idx_o = jnp.zeros(v.shape, F32)
    rank_o = jnp.zeros(v.shape, F32)
    gate_o = jnp.zeros(v.shape, F32)
    for k in range(TOP_K):
        rk = jnp.sum(jnp.where(lane == idxs[k], cum, 0.0), axis=1, keepdims=True)
        idx_o = jnp.where(lane == float(k), idxs[k], idx_o)
        rank_o = jnp.where(lane == float(k), rk, rank_o)
        gate_o = jnp.where(lane == float(k), es[k] / den, gate_o)
    idx_ref[...] = idx_o.astype(I32)
    rank_ref[...] = rank_o.astype(I32)
    gate_ref[...] = gate_o
    total = carry[...] + jnp.sum(sel, axis=0, keepdims=True)
    carry[...] = total
    cnt_ref[...] = total.astype(I32)


def _route(logits):
    t = logits.shape[0]
    ch = 512
    blk = pl.BlockSpec((ch, LANES), lambda i: (i, 0))
    return pl.pallas_call(
        _route_kernel,
        out_shape=(jax.ShapeDtypeStruct((t, LANES), I32),
                   jax.ShapeDtypeStruct((t, LANES), I32),
                   jax.ShapeDtypeStruct((t, LANES), F32),
                   jax.ShapeDtypeStruct((8, LANES), I32)),
        grid=(t // ch,),
        in_specs=[blk],
        out_specs=(blk, blk, blk, pl.BlockSpec((8, LANES), lambda i: (0, 0))),
        scratch_shapes=[pltpu.VMEM((8, LANES), F32)],
        compiler_params=_cparams(("arbitrary",)),
        name="route",
    )(logits)


def _schedule(counts, n_assign):
    max_sb = n_assign // SUPER_ROWS + N_EXPERTS
    nsb_e = (counts + SUPER_ROWS - 1) // SUPER_ROWS
    sb_end = jnp.cumsum(nsb_e)
    sb_start = sb_end - nsb_e
    row_start = (sb_start * SUPER_ROWS).astype(I32)
    nsb = sb_end[-1].astype(I32)
    s = jnp.arange(max_sb, dtype=I32)
    sb_e = jnp.minimum(jnp.searchsorted(sb_end, s, side="right"), N_EXPERTS - 1).astype(I32)
    rem = counts[sb_e] - (s - sb_start[sb_e]) * SUPER_ROWS
    sb_nb = jnp.where(s < nsb, (jnp.clip(rem, 0, SUPER_ROWS) + ROW_BLK - 1) // ROW_BLK, 0).astype(I32)
    zero_start = (s * SUPER_ROWS + jnp.maximum(sb_nb - 1, 0) * ROW_BLK).astype(I32)
    return max_sb, row_start, nsb.reshape(1), sb_e, sb_nb, zero_start


ROW_UNROLL = 8


def _dispatch_kernel(dest_s, zs_s, nsb_s, hp_ref, xs_hbm, zbuf, zsem, sem):
    step = pl.program_id(0)
    tt = hp_ref.shape[0]

    def zero_copy(s):
        return pltpu.make_async_copy(zbuf, xs_hbm.at[pl.ds(pl.multiple_of(zs_s[s], ROW_BLK), ROW_BLK)], zsem)

    @pl.when(step == 0)
    def _():
        zbuf[...] = jnp.zeros_like(zbuf)

        def zstart(s, c):
            zero_copy(s).start()
            return c

        def zwait(s, c):
            zero_copy(s).wait()
            return c

        lax.fori_loop(0, nsb_s[0], zstart, 0)
        lax.fori_loop(0, nsb_s[0], zwait, 0)

    base = step * tt * TOP_K

    def issue(jj, c):
        for u in range(ROW_UNROLL):
            j = jj * ROW_UNROLL + u
            for k in range(TOP_K):
                d = dest_s[base + j * TOP_K + k]
                pltpu.make_async_copy(hp_ref.at[pl.ds(j, 1)], xs_hbm.at[pl.ds(d, 1)], sem).start(
                    priority=k % 2)
        return c

    lax.fori_loop(0, tt // ROW_UNROLL, issue, 0)
    for k in range(TOP_K):
        pltpu.make_async_copy(hp_ref, xs_hbm.at[pl.ds(0, tt)], sem).wait()


def _dispatch(dest, zero_start, nsb, hp, n_rows):
    t, _, w = hp.shape
    tt = 256
    return pl.pallas_call(
        _dispatch_kernel,
        out_shape=jax.ShapeDtypeStruct((n_rows, 1, w), U32),
        grid_spec=pltpu.PrefetchScalarGridSpec(
            num_scalar_prefetch=3,
            grid=(t // tt,),
            in_specs=[pl.BlockSpec((tt, 1, w), lambda i, *_: (i, 0, 0))],
            out_specs=pl.BlockSpec(memory_space=pl.ANY),
            scratch_shapes=[pltpu.VMEM((ROW_BLK, 1, w), U32),
                            pltpu.SemaphoreType.DMA(()), pltpu.SemaphoreType.DMA(())]),
        compiler_params=_cparams(("arbitrary",)),
        name="dispatch",
    )(dest, zero_start, nsb, hp)


def _row_block(r):
    return pl.ds(pl.multiple_of(r * ROW_BLK, ROW_BLK), ROW_BLK)


def _moe_up_kernel(sbe_s, sbn_s, x_ref, win_ref, bin_ref, h_ref, winb, zbuf, xstage):
    nb = sbn_s[pl.program_id(0)]
    half = x_ref.shape[2]
    nsub = win_ref.shape[1] // (2 * LANES)
    lane = lax.broadcasted_iota(I32, (ROW_BLK, LANES), 1)
    even = (2 * lane) & (LANES - 1)
    odd = even + 1
    first_half = lane < LANES // 2

    def unpack(r):
        xstage[...] = x_ref[_row_block(r), 0, :]
        xu = xstage[...]
        xa = lax.bitcast_convert_type(lax.shift_left(xu, U32(16)), F32).astype(BF16)
        xb = lax.bitcast_convert_type(xu & U32(0xFFFF0000), F32).astype(BF16)
        return xa, xb

    def matmul(r):
        xa, xb = unpack(r)
        zbuf[...] = (jnp.dot(xa, winb[0:half, :], preferred_element_type=F32)
                     + jnp.dot(xb, winb[half:2 * half, :], preferred_element_type=F32) + bin_ref[...])

    def matmul_casting(r):
        xa, xb = unpack(r)
        kq = half // 2
        acc = bin_ref[...]
        for ks in range(4):
            wb = win_ref[ks * kq:(ks + 1) * kq, :].astype(BF16)
            winb[ks * kq:(ks + 1) * kq, :] = wb
            xpart = (xa, xb)[ks // 2][:, (ks % 2) * kq:(ks % 2 + 1) * kq]
            acc = acc + jnp.dot(xpart, wb, preferred_element_type=F32)
        zbuf[...] = acc

    def activation():
        hs = []
        for u in range(nsub):
            za = zbuf[:, (2 * u) * LANES:(2 * u + 1) * LANES]
            zb = zbuf[:, (2 * u + 1) * LANES:(2 * u + 2) * LANES]
            gate = jnp.where(first_half, jnp.take_along_axis(za, even, axis=1),
                             jnp.take_along_axis(zb, even, axis=1))
            up = jnp.where(first_half, jnp.take_along_axis(za, odd, axis=1),
                           jnp.take_along_axis(zb, odd, axis=1))
            gate = jnp.minimum(gate, SWIGLU_LIMIT)
            up = jnp.clip(up, -SWIGLU_LIMIT, SWIGLU_LIMIT)
            glu = gate * jax.nn.sigmoid(SWIGLU_ALPHA * gate)
            hs.append(((up + 1.0) * glu).astype(BF16))
        return jnp.concatenate(hs, axis=1)

    matmul_casting(0)

    def step(r, carry):
        h = activation()
        matmul(r + 1)
        h_ref[_row_block(r), :] = h
        return carry

    lax.fori_loop(0, nb - 1, step, 0)
    h_ref[_row_block(nb - 1), :] = activation()


def _moe_down_kernel(sbe_s, sbn_s, h_ref, wout_ref, bout_ref, o_ref, woutb):
    nb = sbn_s[pl.program_id(0)]
    kq = h_ref.shape[1] // 4
    h0 = h_ref[_row_block(0), :]
    acc = bout_ref[...]
    for ks in range(4):
        wb = wout_ref[ks * kq:(ks + 1) * kq, :].astype(BF16)
        woutb[ks * kq:(ks + 1) * kq, :] = wb
        acc = acc + jnp.dot(h0[:, ks * kq:(ks + 1) * kq], wb, preferred_element_type=F32)
    o_ref[_row_block(0), :] = acc

    def project(r):
        return jnp.dot(h_ref[_row_block(r), :], woutb[...], preferred_element_type=F32) + bout_ref[...]

    def pair(p, carry):
        r = 1 + 2 * p
        y0, y1 = project(r), project(r + 1)
        o_ref[_row_block(r), :] = y0
        o_ref[_row_block(r + 1), :] = y1
        return carry

    lax.fori_loop(0, (nb - 1) // 2, pair, 0)

    @pl.when((nb - 1) % 2 == 1)
    def _():
        o_ref[_row_block(nb - 1), :] = project(nb - 1)


def _moe_ffn(sb_e, sb_nb, nsb, xs, w_in, b_in, w_out, b_out, max_sb):
    ne, d, f2 = w_in.shape
    ff = w_out.shape[1]
    half = xs.shape[2]
    rows = max_sb * SUPER_ROWS
    sem = ("arbitrary", "arbitrary")

    h = pl.pallas_call(
        _moe_up_kernel,
        out_shape=jax.ShapeDtypeStruct((rows, ff), BF16),
        grid_spec=pltpu.PrefetchScalarGridSpec(
            num_scalar_prefetch=2,
            grid=(nsb[0], ff // UP_CHUNK),
            in_specs=[pl.BlockSpec((SUPER_ROWS, 1, half), lambda s, j, sbe, sbn: (s, 0, 0)),
                      pl.BlockSpec((None, d, 2 * UP_CHUNK), lambda s, j, sbe, sbn: (sbe[s], 0, j)),
                      pl.BlockSpec((None, 1, 2 * UP_CHUNK), lambda s, j, sbe, sbn: (sbe[s], 0, j))],
            out_specs=pl.BlockSpec((SUPER_ROWS, UP_CHUNK), lambda s, j, sbe, sbn: (s, j)),
            scratch_shapes=[pltpu.VMEM((d, 2 * UP_CHUNK), BF16), pltpu.VMEM((ROW_BLK, 2 * UP_CHUNK), F32),
                            pltpu.VMEM((ROW_BLK, half), U32)]),
        compiler_params=_cparams(sem),
        name="moe_up",
    )(sb_e, sb_nb, xs, w_in, b_in.reshape(ne, 1, f2))

    return pl.pallas_call(
        _moe_down_kernel,
        out_shape=jax.ShapeDtypeStruct((rows, d), F32),
        grid_spec=pltpu.PrefetchScalarGridSpec(
            num_scalar_prefetch=2,
            grid=(nsb[0], d // DOWN_CHUNK),
            in_specs=[pl.BlockSpec((SUPER_ROWS, ff), lambda s, n, sbe, sbn: (s, 0)),
                      pl.BlockSpec((None, ff, DOWN_CHUNK), lambda s, n, sbe, sbn: (sbe[s], 0, n)),
                      pl.BlockSpec((None, 1, DOWN_CHUNK), lambda s, n, sbe, sbn: (sbe[s], 0, n))],
            out_specs=pl.BlockSpec((SUPER_ROWS, DOWN_CHUNK), lambda s, n, sbe, sbn: (s, n)),
            scratch_shapes=[pltpu.VMEM((ff, DOWN_CHUNK), BF16)]),
        compiler_params=_cparams(sem),
        name="moe_down",
    )(sb_e, sb_nb, h, w_out, b_out.reshape(ne, 1, d))


def _combine_kernel(dest_s, gate_ref, x1_ref, gt2_ref, g_ref, ys_hbm, o_ref, buf, sem):
    step = pl.program_id(0)
    nstep = pl.num_programs(0)
    tt = x1_ref.shape[0]

    def fetch(tile, slot):
        base = tile * tt * TOP_K

        def issue(jj, c):
            for u in range(ROW_UNROLL):
                j = jj * ROW_UNROLL + u
                for k in range(TOP_K):
                    d = dest_s[base + j * TOP_K + k]
                    pltpu.make_async_copy(ys_hbm.at[pl.ds(d, 1)], buf.at[slot, k, pl.ds(j, 1)],
                                          sem.at[slot]).start(priority=k % 2)
            return c

        lax.fori_loop(0, tt // ROW_UNROLL, issue, 0)

    @pl.when(step == 0)
    def _():
        fetch(0, 0)

    @pl.when(step + 1 < nstep)
    def _():
        fetch(step + 1, (step + 1) % 2)

    slot = step % 2
    for k in range(TOP_K):
        pltpu.make_async_copy(ys_hbm.at[pl.ds(0, tt)], buf.at[slot, k], sem.at[slot]).wait()

    g = gate_ref[...]
    y = buf[slot, 0] * g[:, 0:1]
    for k in range(1, TOP_K):
        y = y + buf[slot, k] * g[:, k:k + 1]
    o_ref[...] = x1_ref[...] + gt2_ref[...] * _rms(y, g_ref[...])


def _combine(dest, gates, x1, gt2, g_post, ys, seq):
    t, d = x1.shape
    tt = 128
    return pl.pallas_call(
        _combine_kernel,
        out_shape=jax.ShapeDtypeStruct((t, d), F32),
        grid_spec=pltpu.PrefetchScalarGridSpec(
            num_scalar_prefetch=1,
            grid=(t // tt,),
            in_specs=[pl.BlockSpec((tt, LANES), lambda i, *_: (i, 0)),
                      pl.BlockSpec((tt, d), lambda i, *_: (i, 0)),
                      pl.BlockSpec((None, 1, d), lambda i, *_: (i * tt // seq, 0, 0)),
                      pl.BlockSpec((1, d), lambda i, *_: (0, 0)),
                      pl.BlockSpec(memory_space=pl.ANY)],
            out_specs=pl.BlockSpec((tt, d), lambda i, *_: (i, 0)),
            scratch_shapes=[pltpu.VMEM((2, TOP_K, tt, d), F32), pltpu.SemaphoreType.DMA((2,))]),
        compiler_params=_cparams(("arbitrary",)),
        name="combine",
    )(dest, gates, x1, gt2, g_post.reshape(1, d), ys)


def kernel(x, c, w_ada, b_ada, g_pre_mix, g_post_mix, w_in, rpb_na, t5_table, g_out_na, g_out_dil, w_o,
           g_pre_ffn, g_post_ffn, w_router, b_router, w_e_in, b_e_in, w_e_out, b_e_out):
    bsz, seq, d = x.shape
    t = bsz * seq
    for l in range(w_ada.shape[0]):
        mod = _ada_mod(c, w_ada[l], b_ada[l])
        sh1, sc1, gt1, sh2, sc2, gt2 = (m.reshape(bsz, 1, d) for m in jnp.split(mod, 6, axis=-1))
        x2 = x.reshape(t, d)

        proj = _qkv_proj(x2, g_pre_mix[l], sc1, sh1, w_in[l].astype(BF16), seq)
        out_na = _na_attn(proj, _na_bias_table(rpb_na[l]), bsz, seq)
        out_dil = _dil_attn(proj, t5_table, bsz, seq)

        ne = w_router.shape[-1]
        wr = jnp.zeros((d, LANES), BF16).at[:, :ne].set(w_router[l].astype(BF16))
        br = jnp.full((1, LANES), NEG, F32).at[0, :ne].set(b_router[l])
        x1, hp, logits = _mix_out(out_na, out_dil, g_out_na[l], g_out_dil[l], w_o[l].astype(BF16), x2,
                                  gt1, sc2, sh2, g_post_mix[l], g_pre_ffn[l], wr, br, seq)

        idx, rank, gates, cnt = _route(logits)
        max_sb, row_start, nsb, sb_e, sb_nb, zero_start = _schedule(cnt[0, :ne], t * TOP_K)
        eidx = idx[:, :TOP_K].reshape(-1)
        onehot = eidx[:, None] == jnp.arange(ne, dtype=I32)[None, :]
        dest = rank[:, :TOP_K].reshape(-1) + jnp.sum(jnp.where(onehot, row_start[None, :], 0), axis=1)
        xs = _dispatch(dest, zero_start, nsb, hp, max_sb * SUPER_ROWS)
        ys = _moe_ffn(sb_e, sb_nb, nsb, xs, w_e_in[l], b_e_in[l], w_e_out[l], b_e_out[l], max_sb)
        x = _combine(dest, gates, x1, gt2, g_post_ffn[l], ys, seq).reshape(bsz, seq, d)
    return x
```

```python
import functools

import numpy as np
import jax
import jax.numpy as jnp
from jax import lax
from jax.experimental import pallas as pl
from jax.experimental.pallas import tpu as pltpu

F32 = jnp.float32
BF16 = jnp.bfloat16
U32 = jnp.uint32
I32 = jnp.int32

HEAD_DIM = 128
N_HEADS_NA = 8
N_HEADS_DIL = 8
GRID_W = 64
NA_ROWS = 8
NA_COLS = 16
DIL_PATTERNS = ((128, 1), (512, 4), (2048, 16))
T5_BUCKETS = 32
T5_MAX_DIST = 1024
N_EXPERTS = 32
TOP_K = 4
SWIGLU_LIMIT = 7.0
SWIGLU_ALPHA = 1.702
EPS = 1e-6
NEG = -1e30
SCALE = HEAD_DIM ** -0.5

LANES = 128
QBLK = 128
ROW_BLK = 256
SUPER_BLKS = 7
SUPER_ROWS = ROW_BLK * SUPER_BLKS
UP_CHUNK = 512
DOWN_CHUNK = 1024
VMEM_LIMIT = 56 * 1024 * 1024


def _cparams(sem, vmem=VMEM_LIMIT):
    return pltpu.CompilerParams(dimension_semantics=sem, vmem_limit_bytes=vmem)


def _rms(x, g):
    return x * lax.rsqrt(jnp.mean(x * x, axis=-1, keepdims=True) + EPS) * g


def _ada_kernel(c_ref, w_ref, b_ref, o_ref):
    c = c_ref[...]
    s = c * jax.nn.sigmoid(c)
    o_ref[...] = jnp.dot(s.astype(BF16), w_ref[...].astype(BF16),
                         preferred_element_type=F32) + b_ref[...]


def _ada_mod(c, w, b):
    bsz, d = c.shape
    n = w.shape[1]
    tn = 1024
    cp = jnp.zeros((8, d), F32).at[:bsz].set(c)
    out = pl.pallas_call(
        _ada_kernel,
        out_shape=jax.ShapeDtypeStruct((8, n), F32),
        grid=(n // tn,),
        in_specs=[pl.BlockSpec((8, d), lambda j: (0, 0)),
                  pl.BlockSpec((d, tn), lambda j: (0, j)),
                  pl.BlockSpec((1, tn), lambda j: (0, j))],
        out_specs=pl.BlockSpec((8, tn), lambda j: (0, j)),
        compiler_params=_cparams(("arbitrary",)),
        name="ada_mod",
    )(cp, w, b.reshape(1, n))
    return out[:bsz]


def _qkv_kernel(x_ref, g_ref, sc_ref, sh_ref, w_ref, o_ref, h_scr):
    @pl.when(pl.program_id(1) == 0)
    def _():
        h = _rms(x_ref[...], g_ref[...]) * (1.0 + sc_ref[...]) + sh_ref[...]
        h_scr[...] = h.astype(BF16)

    acc = jnp.dot(h_scr[...], w_ref[...], preferred_element_type=F32)
    for u in range(o_ref.shape[0]):
        o_ref[u] = acc[:, u * LANES:(u + 1) * LANES].astype(BF16)


def _qkv_proj(x2, g, sc, sh, w_bf, seq):
    t, d = x2.shape
    n = w_bf.shape[1]
    tm, tn = 1024, 1024
    return pl.pallas_call(
        _qkv_kernel,
        out_shape=jax.ShapeDtypeStruct((n // LANES, t, LANES), BF16),
        grid=(t // tm, n // tn),
        in_specs=[pl.BlockSpec((tm, d), lambda i, j: (i, 0)),
                  pl.BlockSpec((1, d), lambda i, j: (0, 0)),
                  pl.BlockSpec((None, 1, d), lambda i, j: (i * tm // seq, 0, 0)),
                  pl.BlockSpec((None, 1, d), lambda i, j: (i * tm // seq, 0, 0)),
                  pl.BlockSpec((d, tn), lambda i, j: (0, j))],
        out_specs=pl.BlockSpec((tn // LANES, tm, LANES), lambda i, j: (j, i, 0)),
        scratch_shapes=[pltpu.VMEM((tm, d), BF16)],
        compiler_params=_cparams(("arbitrary", "arbitrary")),
        name="qkv_proj",
    )(x2, g.reshape(1, d), sc, sh, w_bf)


def _toeplitz(vec, rows, cols):
    n = rows + cols - 1
    assert vec.shape[-1] == n
    lead = vec.shape[:-1]
    ext = jnp.concatenate([vec, jnp.zeros(lead + (1,), vec.dtype)], axis=-1)
    flat = jnp.broadcast_to(ext[..., None, :], lead + (rows, n + 1)).reshape(lead + (rows * (n + 1),))
    skew = flat[..., :rows * n].reshape(lead + (rows, n))
    return skew[..., rows - 1:rows - 1 + cols]


def _na_bias_table(rpb):
    cidx = np.arange(GRID_W)
    col_start = np.clip(cidx - NA_COLS // 2, 0, GRID_W - NA_COLS)
    col_ok = (cidx[None, :] >= col_start[:, None]) & (cidx[None, :] < col_start[:, None] + NA_COLS)
    pad = GRID_W - NA_COLS
    vec = jnp.pad(rpb.astype(F32), ((0, 0), (0, 0), (pad, pad)))
    tab = jnp.where(col_ok, _toeplitz(vec, GRID_W, GRID_W), NEG)
    nh = rpb.shape[0]
    per_var = [tab[:, v:v + NA_ROWS].transpose(0, 2, 1, 3).reshape(nh, GRID_W, NA_ROWS * GRID_W)
               for v in range(NA_ROWS)]
    return jnp.stack(per_var, axis=1)


def _attn_group(qs, ks, vs, biases):
    ss = [lax.dot_general(q, k, (((1,), (1,)), ((), ())), preferred_element_type=F32) * SCALE + b
          for q, k, b in zip(qs, ks, biases)]
    ms = [jnp.max(s, axis=-1, keepdims=True) for s in ss]
    ps = [jnp.exp(s - m) for s, m in zip(ss, ms)]
    ls = [jnp.sum(p, axis=-1, keepdims=True) for p in ps]
    os = [jnp.dot(p.astype(BF16), v, preferred_element_type=F32) / l for p, v, l in zip(ps, vs, ls)]
    return os, [m + jnp.log(l) for m, l in zip(ms, ls)]


NA_GROUP = 8


def _na_kernel(q_ref, k_ref, v_ref, bias_ref, o_ref, *, rows):
    nkeys = NA_ROWS * GRID_W

    def body(g, carry):
        qs, ks, vs, bs, q0s = [], [], [], [], []
        for i in range(NA_GROUP):
            r = g * NA_GROUP + i
            rs = jnp.clip(r - NA_ROWS // 2, 0, rows - NA_ROWS)
            q0 = pl.multiple_of(r * GRID_W, GRID_W)
            k0 = pl.multiple_of(rs * GRID_W, GRID_W)
            q0s.append(q0)
            qs.append(q_ref[pl.ds(q0, GRID_W), :])
            ks.append(k_ref[pl.ds(k0, nkeys), :])
            vs.append(v_ref[pl.ds(k0, nkeys), :])
            bs.append(bias_ref[rs - r + (NA_ROWS - 1)])
        os, _ = _attn_group(qs, ks, vs, bs)
        for q0, o in zip(q0s, os):
            o_ref[pl.ds(q0, GRID_W), :] = o
        return carry

    lax.fori_loop(0, rows // NA_GROUP, body, 0)


def _na_attn(proj, bias, bsz, seq):
    nh = N_HEADS_NA
    t = bsz * seq
    blk = lambda off: pl.BlockSpec((None, seq, LANES), lambda h, b: (h + off, b, 0))
    return pl.pallas_call(
        functools.partial(_na_kernel, rows=seq // GRID_W),
        out_shape=jax.ShapeDtypeStruct((nh, t, LANES), F32),
        grid=(nh, bsz),
        in_specs=[blk(0), blk(nh), blk(2 * nh),
                  pl.BlockSpec((None, NA_ROWS, GRID_W, NA_ROWS * GRID_W), lambda h, b: (h, 0, 0, 0))],
        out_specs=pl.BlockSpec((None, seq, LANES), lambda h, b: (h, b, 0)),
        compiler_params=_cparams(("arbitrary", "arbitrary")),
        name="na_attn",
    )(proj, proj, proj, bias)


def _t5_bucket(rel):
    nb = T5_BUCKETS // 2
    max_exact = nb // 2
    ret = (rel > 0).astype(np.int32) * nb
    n = np.abs(rel)
    large = max_exact + (np.log(np.maximum(n, 1) / max_exact) / np.log(T5_MAX_DIST / max_exact)
                         * (nb - max_exact)).astype(np.int32)
    large = np.minimum(large, nb - 1)
    return (ret + np.where(n < max_exact, n, large)).astype(np.int32)


def _dil_geometry(sub_len):
    half = DIL_PATTERNS[0][0] // 2
    width = min(sub_len, QBLK + 2 * half)
    nblk = sub_len // QBLK
    starts = [min(max(QBLK * n - half, 0), sub_len - width) for n in range(nblk)]
    offs = sorted({ws - QBLK * n for n, ws in enumerate(starts)}, reverse=True)
    var = [offs.index(ws - QBLK * n) for n, ws in enumerate(starts)]
    return width, starts, offs, var


def _dil_bias_table(t5_table, dil, width, offs):
    half = DIL_PATTERNS[0][0] // 2
    tabs = []
    for off in offs:
        delta = np.arange(QBLK + width - 1) - (QBLK - 1) + off
        onehot = np.eye(T5_BUCKETS, dtype=np.float32)[_t5_bucket(delta * dil)]
        vals = jnp.dot(jnp.asarray(onehot), t5_table.astype(F32), precision=lax.Precision.HIGHEST)
        vec = jnp.where((np.abs(delta) <= half)[:, None], vals, NEG).T
        tabs.append(_toeplitz(vec, QBLK, width))
    return jnp.stack(tabs, axis=1)


DIL_GROUP = 8


def _dil_kernel(q_ref, k_ref, v_ref, b1_ref, b4_ref, b16_ref, o_ref, qf, kf, vf, o4, l4, o16, l16, *, seq):
    qf[...] = q_ref[...].astype(F32)
    kf[...] = k_ref[...].astype(F32)
    vf[...] = v_ref[...].astype(F32)

    for dil, bias_ref, o_s, l_s in ((DIL_PATTERNS[2][1], b16_ref, o16, l16), (DIL_PATTERNS[1][1], b4_ref, o4, l4)):
        width, starts, _, var = _dil_geometry(seq // dil)
        blocks = [(rho, n, ws) for rho in range(dil) for n, ws in enumerate(starts)]
        for g in range(0, len(blocks), DIL_GROUP):
            grp = blocks[g:g + DIL_GROUP]
            qrows = [pl.ds(rho + dil * QBLK * n, QBLK, stride=dil) for rho, n, _ in grp]
            krows = [pl.ds(rho + dil * ws, width, stride=dil) for rho, _, ws in grp]
            os, lses = _attn_group([qf[r, :].astype(BF16) for r in qrows],
                                   [kf[r, :].astype(BF16) for r in krows],
                                   [vf[r, :].astype(BF16) for r in krows],
                                   [bias_ref[var[n]] for _, n, _ in grp])
            for r, o, lse in zip(qrows, os, lses):
                o_s[r, :] = o
                l_s[r, :] = jnp.broadcast_to(lse, (QBLK, LANES))

    width, starts, _, _ = _dil_geometry(seq)
    nblk = len(starts)
    half = DIL_PATTERNS[0][0] // 2

    def body(g, carry):
        rows, krows, bs = [], [], []
        for i in range(DIL_GROUP):
            n = g * DIL_GROUP + i
            ws = pl.multiple_of(jnp.clip(n * QBLK - half, 0, seq - width), half)
            rows.append(pl.ds(pl.multiple_of(n * QBLK, QBLK), QBLK))
            krows.append(pl.ds(ws, width))
            bs.append(b1_ref[jnp.where(n == 0, 0, jnp.where(n == nblk - 1, 2, 1))])
        os, lses = _attn_group([q_ref[r, :] for r in rows], [k_ref[r, :] for r in krows],
                               [v_ref[r, :] for r in krows], bs)
        for r, o1, lse1 in zip(rows, os, lses):
            lse4, lse16 = l4[r, :], l16[r, :]
            mx = jnp.maximum(jnp.maximum(lse4, lse16), lse1)
            e1 = jnp.exp(lse1 - mx)
            e4 = jnp.exp(lse4 - mx)
            e16 = jnp.exp(lse16 - mx)
            o_ref[r, :] = (e1 * o1 + e4 * o4[r, :] + e16 * o16[r, :]) / (e1 + e4 + e16)
        return carry

    lax.fori_loop(0, nblk // DIL_GROUP, body, 0)


def _dil_attn(proj, t5_table, bsz, seq):
    nh = N_HEADS_DIL
    tables = []
    for _, dil in DIL_PATTERNS:
        width, _, offs, var = _dil_geometry(seq // dil)
        tables.append(_dil_bias_table(t5_table, dil, width, offs))
    width, _, offs, var = _dil_geometry(seq)
    assert offs == [0, -(DIL_PATTERNS[0][0] // 2), -DIL_PATTERNS[0][0]] and var[0] == 0 and var[-1] == 2
    first = 3 * N_HEADS_NA
    blk = lambda off: pl.BlockSpec((None, seq, LANES), lambda h, b: (first + off + h, b, 0))
    tab = lambda t: pl.BlockSpec((None,) + t.shape[1:], lambda h, b: (h, 0, 0, 0))
    return pl.pallas_call(
        functools.partial(_dil_kernel, seq=seq),
        out_shape=jax.ShapeDtypeStruct((nh, bsz * seq, LANES), F32),
        grid=(nh, bsz),
        in_specs=[blk(0), blk(nh), blk(2 * nh)] + [tab(t) for t in tables],
        out_specs=pl.BlockSpec((None, seq, LANES), lambda h, b: (h, b, 0)),
        scratch_shapes=[pltpu.VMEM((seq, LANES), F32)] * 7,
        compiler_params=_cparams(("arbitrary", "arbitrary")),
        name="dil_attn",
    )(proj, proj, proj, *tables)


MIX_SUB = 256


def _mix_kernel(na_ref, dl_ref, gna_ref, gdl_ref, wo_ref, x_ref, gt1_ref, sc2_ref, sh2_ref,
                gpost_ref, gpre_ref, wr_ref, br_ref, x1_ref, hp_ref, lg_ref):
    nh = na_ref.shape[0]
    tm = x_ref.shape[0]
    subs = [slice(i * MIX_SUB, (i + 1) * MIX_SUB) for i in range(tm // MIX_SUB)]
    lhs = []
    for r in subs:
        na = jnp.concatenate([na_ref[h, r, :] for h in range(nh)], axis=1)
        dl = jnp.concatenate([dl_ref[h, r, :] for h in range(nh)], axis=1)
        lhs.append(jnp.concatenate([_rms(na, gna_ref[...]), _rms(dl, gdl_ref[...])], axis=1).astype(BF16))
    mixed = [jnp.dot(a, wo_ref[...], preferred_element_type=F32) for a in lhs]
    for r, m in zip(subs, mixed):
        x1 = x_ref[r, :] + gt1_ref[...] * _rms(m, gpost_ref[...])
        x1_ref[r, :] = x1
        hf = _rms(x1, gpre_ref[...]) * (1.0 + sc2_ref[...]) + sh2_ref[...]
        hb = hf.astype(BF16)
        lg_ref[r, :] = jnp.dot(hb, wr_ref[...], preferred_element_type=F32) + br_ref[...]
        half = hb.shape[1] // 2
        lo = lax.bitcast_convert_type(hb[:, :half].astype(F32), U32)
        hi = lax.bitcast_convert_type(hb[:, half:].astype(F32), U32)
        hp_ref[r, 0, :] = (hi & U32(0xFFFF0000)) | lax.shift_right_logical(lo, U32(16))


def _mix_out(out_na, out_dil, g_na, g_dil, wo_bf, x2, gt1, sc2, sh2, g_post, g_pre, wr_bf, br, seq):
    t, d = x2.shape
    nh = out_na.shape[0]
    tm = 512
    wna = g_na.shape[-1]
    row = lambda n: pl.BlockSpec((1, n), lambda i: (0, 0))
    per_b = pl.BlockSpec((None, 1, d), lambda i: (i * tm // seq, 0, 0))
    heads = pl.BlockSpec((nh, tm, LANES), lambda i: (0, i, 0))
    return pl.pallas_call(
        _mix_kernel,
        out_shape=(jax.ShapeDtypeStruct((t, d), F32),
                   jax.ShapeDtypeStruct((t, 1, d // 2), U32),
                   jax.ShapeDtypeStruct((t, LANES), F32)),
        grid=(t // tm,),
        in_specs=[heads, heads, row(wna), row(wna),
                  pl.BlockSpec((d, d), lambda i: (0, 0)),
                  pl.BlockSpec((tm, d), lambda i: (i, 0)),
                  per_b, per_b, per_b, row(d), row(d),
                  pl.BlockSpec((d, LANES), lambda i: (0, 0)), row(LANES)],
        out_specs=(pl.BlockSpec((tm, d), lambda i: (i, 0)),
                   pl.BlockSpec((tm, 1, d // 2), lambda i: (i, 0, 0)),
                   pl.BlockSpec((tm, LANES), lambda i: (i, 0))),
        compiler_params=_cparams(("arbitrary",)),
        name="mix_out",
    )(out_na, out_dil, g_na.reshape(1, wna), g_dil.reshape(1, wna), wo_bf, x2, gt1, sc2, sh2,
      g_post.reshape(1, d), g_pre.reshape(1, d), wr_bf, br)


def _route_kernel(lg_ref, idx_ref, rank_ref, gate_ref, cnt_ref, carry):
    step = pl.program_id(0)

    @pl.when(step == 0)
    def _():
        carry[...] = jnp.zeros_like(carry)

    v = lg_ref[...]
    ch = v.shape[0]
    lane = lax.broadcasted_iota(I32, v.shape, 1).astype(F32)
    vals, idxs = [], []
    for _ in range(TOP_K):
        m = jnp.max(v, axis=1, keepdims=True)
        ik = jnp.min(jnp.where(v == m, lane, float(LANES)), axis=1, keepdims=True)
        vals.append(m)
        idxs.append(ik)
        v = jnp.where(lane == ik, -jnp.inf, v)
    es = [jnp.exp(val - vals[0]) for val in vals]
    den = es[0] + es[1] + es[2] + es[3]
    sel = jnp.zeros(v.shape, F32)
    for ik in idxs:
        sel = jnp.where(lane == ik, 1.0, sel)
    ti = lax.broadcasted_iota(I32, (ch, ch), 0)
    tj = lax.broadcasted_iota(I32, (ch, ch), 1)
    lower = jnp.where(tj < ti, 1.0, 0.0).astype(BF16)
    cum = jnp.dot(lower, sel.astype(BF16), preferred_element_type=F32) + carry[0:1, :]
    idx_o = jnp.zeros(v.shape, F32)
    rank_o = jnp.zeros(v.shape, F32)
    gate_o = jnp.zeros(v.shape, F32)
    for k in range(TOP_K):
        rk = jnp.sum(jnp.where(lane == idxs[k], cum, 0.0), axis=1, keepdims=True)
        idx_o = jnp.where(lane == float(k), idxs[k], idx_o)
        rank_o = jnp.where(lane == float(k), rk, rank_o)
        gate_o = jnp.where(lane == float(k), es[k] / den, gate_o)
    idx_ref[...] = idx_o.astype(I32)
    rank_ref[...] = rank_o.astype(I32)
    gate_ref[...] = gate_o
    total = carry[...] + jnp.sum(sel, axis=0, keepdims=True)
    carry[...] = total
    cnt_ref[...] = total.astype(I32)


def _route(logits):
    t = logits.shape[0]
    ch = 512
    blk = pl.BlockSpec((ch, LANES), lambda i: (i, 0))
    return pl.pallas_call(
        _route_kernel,
        out_shape=(jax.ShapeDtypeStruct((t, LANES), I32),
                   jax.ShapeDtypeStruct((t, LANES), I32),
                   jax.ShapeDtypeStruct((t, LANES), F32),
                   jax.ShapeDtypeStruct((8, LANES), I32)),
        grid=(t // ch,),
        in_specs=[blk],
        out_specs=(blk, blk, blk, pl.BlockSpec((8, LANES), lambda i: (0, 0))),
        scratch_shapes=[pltpu.VMEM((8, LANES), F32)],
        compiler_params=_cparams(("arbitrary",)),
        name="route",
    )(logits)


def _schedule(counts, n_assign):
    max_sb = n_assign // SUPER_ROWS + N_EXPERTS
    nsb_e = (counts + SUPER_ROWS - 1) // SUPER_ROWS
    sb_end = jnp.cumsum(nsb_e)
    sb_start = sb_end - nsb_e
    row_start = (sb_start * SUPER_ROWS).astype(I32)
    nsb = sb_end[-1].astype(I32)
    s = jnp.arange(max_sb, dtype=I32)
    sb_e = jnp.minimum(jnp.searchsorted(sb_end, s, side="right"), N_EXPERTS - 1).astype(I32)
    rem = counts[sb_e] - (s - sb_start[sb_e]) * SUPER_ROWS
    sb_nb = jnp.where(s < nsb, (jnp.clip(rem, 0, SUPER_ROWS) + ROW_BLK - 1) // ROW_BLK, 0).astype(I32)
    zero_start = (s * SUPER_ROWS + jnp.maximum(sb_nb - 1, 0) * ROW_BLK).astype(I32)
    return max_sb, row_start, nsb.reshape(1), sb_e, sb_nb, zero_start


ROW_UNROLL = 8


def _dispatch_kernel(dest_s, zs_s, nsb_s, hp_ref, xs_hbm, zbuf, zsem, sem):
    step = pl.program_id(0)
    tt = hp_ref.shape[0]

    def zero_copy(s):
        return pltpu.make_async_copy(zbuf, xs_hbm.at[pl.ds(pl.multiple_of(zs_s[s], ROW_BLK), ROW_BLK)], zsem)

    @pl.when(step == 0)
    def _():
        zbuf[...] = jnp.zeros_like(zbuf)

        def zstart(s, c):
            zero_copy(s).start()
            return c

        def zwait(s, c):
            zero_copy(s).wait()
            return c

        lax.fori_loop(0, nsb_s[0], zstart, 0)
        lax.fori_loop(0, nsb_s[0], zwait, 0)

    base = step * tt * TOP_K

    def issue(jj, c):
        for u in range(ROW_UNROLL):
            j = jj * ROW_UNROLL + u
            for k in range(TOP_K):
                d = dest_s[base + j * TOP_K + k]
                pltpu.make_async_copy(hp_ref.at[pl.ds(j, 1)], xs_hbm.at[pl.ds(d, 1)], sem).start(
                    priority=k % 2)
        return c

    lax.fori_loop(0, tt // ROW_UNROLL, issue, 0)
    for k in range(TOP_K):
        pltpu.make_async_copy(hp_ref, xs_hbm.at[pl.ds(0, tt)], sem).wait()


def _dispatch(dest, zero_start, nsb, hp, n_rows):
    t, _, w = hp.shape
    tt = 256
    return pl.pallas_call(
        _dispatch_kernel,
        out_shape=jax.ShapeDtypeStruct((n_rows, 1, w), U32),
        grid_spec=pltpu.PrefetchScalarGridSpec(
            num_scalar_prefetch=3,
            grid=(t // tt,),
            in_specs=[pl.BlockSpec((tt, 1, w), lambda i, *_: (i, 0, 0))],
            out_specs=pl.BlockSpec(memory_space=pl.ANY),
            scratch_shapes=[pltpu.VMEM((ROW_BLK, 1, w), U32),
                            pltpu.SemaphoreType.DMA(()), pltpu.SemaphoreType.DMA(())]),
        compiler_params=_cparams(("arbitrary",)),
        name="dispatch",
    )(dest, zero_start, nsb, hp)


def _row_block(r):
    return pl.ds(pl.multiple_of(r * ROW_BLK, ROW_BLK), ROW_BLK)


def _moe_up_kernel(sbe_s, sbn_s, x_ref, win_ref, bin_ref, h_ref, winb, zbuf, xstage):
    nb = sbn_s[pl.program_id(0)]
    half = x_ref.shape[2]
    nsub = win_ref.shape[1] // (2 * LANES)
    lane = lax.broadcasted_iota(I32, (ROW_BLK, LANES), 1)
    even = (2 * lane) & (LANES - 1)
    odd = even + 1
    first_half = lane < LANES // 2

    def unpack(r):
        xstage[...] = x_ref[_row_block(r), 0, :]
        xu = xstage[...]
        xa = lax.bitcast_convert_type(lax.shift_left(xu, U32(16)), F32).astype(BF16)
        xb = lax.bitcast_convert_type(xu & U32(0xFFFF0000), F32).astype(BF16)
        return xa, xb

    def matmul(r):
        xa, xb = unpack(r)
        zbuf[...] = (jnp.dot(xa, winb[0:half, :], preferred_element_type=F32)
                     + jnp.dot(xb, winb[half:2 * half, :], preferred_element_type=F32) + bin_ref[...])

    def matmul_casting(r):
        xa, xb = unpack(r)
        kq = half // 2
        acc = bin_ref[...]
        for ks in range(4):
            wb = win_ref[ks * kq:(ks + 1) * kq, :].astype(BF16)
            winb[ks * kq:(ks + 1) * kq, :] = wb
            xpart = (xa, xb)[ks // 2][:, (ks % 2) * kq:(ks % 2 + 1) * kq]
            acc = acc + jnp.dot(xpart, wb, preferred_element_type=F32)
        zbuf[...] = acc

    def activation():
        hs = []
        for u in range(nsub):
            za = zbuf[:, (2 * u) * LANES:(2 * u + 1) * LANES]
            zb = zbuf[:, (2 * u + 1) * LANES:(2 * u + 2) * LANES]
            gate = jnp.where(first_half, jnp.take_along_axis(za, even, axis=1),
                             jnp.take_along_axis(zb, even, axis=1))
            up = jnp.where(first_half, jnp.take_along_axis(za, odd, axis=1),
                           jnp.take_along_axis(zb, odd, axis=1))
            gate = jnp.minimum(gate, SWIGLU_LIMIT)
            up = jnp.clip(up, -SWIGLU_LIMIT, SWIGLU_LIMIT)
            glu = gate * jax.nn.sigmoid(SWIGLU_ALPHA * gate)
            hs.append(((up + 1.0) * glu).astype(BF16))
        return jnp.concatenate(hs, axis=1)

    matmul_casting(0)

    def step(r, carry):
        h = activation()
        matmul(r + 1)
        h_ref[_row_block(r), :] = h
        return carry

    lax.fori_loop(0, nb - 1, step, 0)
    h_ref[_row_block(nb - 1), :] = activation()


def _moe_down_kernel(sbe_s, sbn_s, h_ref, wout_ref, bout_ref, o_ref, woutb):
    nb = sbn_s[pl.program_id(0)]
    kq = h_ref.shape[1] // 4
    h0 = h_ref[_row_block(0), :]
    acc = bout_ref[...]
    for ks in range(4):
        wb = wout_ref[ks * kq:(ks + 1) * kq, :].astype(BF16)
        woutb[ks * kq:(ks + 1) * kq, :] = wb
        acc = acc + jnp.dot(h0[:, ks * kq:(ks + 1) * kq], wb, preferred_element_type=F32)
    o_ref[_row_block(0), 0, :] = acc

    def project(r):
        return jnp.dot(h_ref[_row_block(r), :], woutb[...], preferred_element_type=F32) + bout_ref[...]

    def pair(p, carry):
        r = 1 + 2 * p
        y0, y1 = project(r), project(r + 1)
        o_ref[_row_block(r), 0, :] = y0
        o_ref[_row_block(r + 1), 0, :] = y1
        return carry

    lax.fori_loop(0, (nb - 1) // 2, pair, 0)

    @pl.when((nb - 1) % 2 == 1)
    def _():
        o_ref[_row_block(nb - 1), 0, :] = project(nb - 1)


def _moe_ffn(sb_e, sb_nb, nsb, xs, w_in, b_in, w_out, b_out, max_sb):
    ne, d, f2 = w_in.shape
    ff = w_out.shape[1]
    half = xs.shape[2]
    rows = max_sb * SUPER_ROWS
    sem = ("arbitrary", "arbitrary")

    h = pl.pallas_call(
        _moe_up_kernel,
        out_shape=jax.ShapeDtypeStruct((rows, ff), BF16),
        grid_spec=pltpu.PrefetchScalarGridSpec(
            num_scalar_prefetch=2,
            grid=(nsb[0], ff // UP_CHUNK),
            in_specs=[pl.BlockSpec((SUPER_ROWS, 1, half), lambda s, j, sbe, sbn: (s, 0, 0)),
                      pl.BlockSpec((None, d, 2 * UP_CHUNK), lambda s, j, sbe, sbn: (sbe[s], 0, j)),
                      pl.BlockSpec((None, 1, 2 * UP_CHUNK), lambda s, j, sbe, sbn: (sbe[s], 0, j))],
            out_specs=pl.BlockSpec((SUPER_ROWS, UP_CHUNK), lambda s, j, sbe, sbn: (s, j)),
            scratch_shapes=[pltpu.VMEM((d, 2 * UP_CHUNK), BF16), pltpu.VMEM((ROW_BLK, 2 * UP_CHUNK), F32),
                            pltpu.VMEM((ROW_BLK, half), U32)]),
        compiler_params=_cparams(sem),
        name="moe_up",
    )(sb_e, sb_nb, xs, w_in, b_in.reshape(ne, 1, f2))

    return pl.pallas_call(
        _moe_down_kernel,
        out_shape=jax.ShapeDtypeStruct((rows, 1, d), F32),
        grid_spec=pltpu.PrefetchScalarGridSpec(
            num_scalar_prefetch=2,
            grid=(nsb[0], d // DOWN_CHUNK),
            in_specs=[pl.BlockSpec((SUPER_ROWS, ff), lambda s, n, sbe, sbn: (s, 0)),
                      pl.BlockSpec((None, ff, DOWN_CHUNK), lambda s, n, sbe, sbn: (sbe[s], 0, n)),
                      pl.BlockSpec((None, 1, DOWN_CHUNK), lambda s, n, sbe, sbn: (sbe[s], 0, n))],
            out_specs=pl.BlockSpec((SUPER_ROWS, 1, DOWN_CHUNK), lambda s, n, sbe, sbn: (s, 0, n)),
            scratch_shapes=[pltpu.VMEM((ff, DOWN_CHUNK), BF16)]),
        compiler_params=_cparams(sem),
        name="moe_down",
    )(sb_e, sb_nb, h, w_out, b_out.reshape(ne, 1, d))


COMBINE_GROUP = 32


def _combine_kernel(dest_s, gate_ref, x1_ref, gt2_ref, g_ref, ys_hbm, o_ref, buf, stage, sem):
    step = pl.program_id(0)
    nstep = pl.num_programs(0)
    tt = x1_ref.shape[0]
    slot = step % 2

    def issue(tile, slot_, jj):
        base = tile * tt * TOP_K
        for u in range(COMBINE_GROUP):
            j = jj * COMBINE_GROUP + u
            for k in range(TOP_K):
                d = dest_s[base + j * TOP_K + k]
                pltpu.make_async_copy(ys_hbm.at[pl.ds(d, 1)], buf.at[slot_, k, pl.ds(j, 1)],
                                      sem.at[slot_]).start(priority=k % 2)

    def finish(jj):
        rows = pl.ds(pl.multiple_of(jj * COMBINE_GROUP, COMBINE_GROUP), COMBINE_GROUP)
        g = gate_ref[rows, :]
        y = None
        for k in range(TOP_K):
            stage[k] = buf[slot, k, rows, 0, :]
            term = stage[k] * g[:, k:k + 1]
            y = term if y is None else y + term
        o_ref[rows, :] = x1_ref[rows, :] + gt2_ref[...] * _rms(y, g_ref[...])

    @pl.when(step == 0)
    def _():
        def first(jj, c):
            issue(0, 0, jj)
            return c

        lax.fori_loop(0, tt // COMBINE_GROUP, first, 0)

    for k in range(TOP_K):
        pltpu.make_async_copy(ys_hbm.at[pl.ds(0, tt)], buf.at[slot, k], sem.at[slot]).wait()

    @pl.when(step + 1 < nstep)
    def _():
        def both(jj, c):
            finish(jj)
            issue(step + 1, 1 - slot, jj)
            return c

        lax.fori_loop(0, tt // COMBINE_GROUP, both, 0)

    @pl.when(step + 1 == nstep)
    def _():
        def last(jj, c):
            finish(jj)
            return c

        lax.fori_loop(0, tt // COMBINE_GROUP, last, 0)


def _combine(dest, gates, x1, gt2, g_post, ys, seq):
    t, d = x1.shape
    tt = 128
    return pl.pallas_call(
        _combine_kernel,
        out_shape=jax.ShapeDtypeStruct((t, d), F32),
        grid_spec=pltpu.PrefetchScalarGridSpec(
            num_scalar_prefetch=1,
            grid=(t // tt,),
            in_specs=[pl.BlockSpec((tt, LANES), lambda i, *_: (i, 0)),
                      pl.BlockSpec((tt, d), lambda i, *_: (i, 0)),
                      pl.BlockSpec((None, 1, d), lambda i, *_: (i * tt // seq, 0, 0)),
                      pl.BlockSpec((1, d), lambda i, *_: (0, 0)),
                      pl.BlockSpec(memory_space=pl.ANY)],
            out_specs=pl.BlockSpec((tt, d), lambda i, *_: (i, 0)),
            scratch_shapes=[pltpu.VMEM((2, TOP_K, tt, 1, d), F32), pltpu.VMEM((TOP_K, COMBINE_GROUP, d), F32),
                            pltpu.SemaphoreType.DMA((2,))]),
        compiler_params=_cparams(("arbitrary",)),
        name="combine",
    )(dest, gates, x1, gt2, g_post.reshape(1, d), ys)


def kernel(x, c, w_ada, b_ada, g_pre_mix, g_post_mix, w_in, rpb_na, t5_table, g_out_na, g_out_dil, w_o,
           g_pre_ffn, g_post_ffn, w_router, b_router, w_e_in, b_e_in, w_e_out, b_e_out):
    bsz, seq, d = x.shape
    t = bsz * seq
    for l in range(w_ada.shape[0]):
        mod = _ada_mod(c, w_ada[l], b_ada[l])
        sh1, sc1, gt1, sh2, sc2, gt2 = (m.reshape(bsz, 1, d) for m in jnp.split(mod, 6, axis=-1))
        x2 = x.reshape(t, d)

        proj = _qkv_proj(x2, g_pre_mix[l], sc1, sh1, w_in[l].astype(BF16), seq)
        out_na = _na_attn(proj, _na_bias_table(rpb_na[l]), bsz, seq)
        out_dil = _dil_attn(proj, t5_table, bsz, seq)

        ne = w_router.shape[-1]
        wr = jnp.zeros((d, LANES), BF16).at[:, :ne].set(w_router[l].astype(BF16))
        br = jnp.full((1, LANES), NEG, F32).at[0, :ne].set(b_router[l])
        x1, hp, logits = _mix_out(out_na, out_dil, g_out_na[l], g_out_dil[l], w_o[l].astype(BF16), x2,
                                  gt1, sc2, sh2, g_post_mix[l], g_pre_ffn[l], wr, br, seq)

        idx, rank, gates, cnt = _route(logits)
        max_sb, row_start, nsb, sb_e, sb_nb, zero_start = _schedule(cnt[0, :ne], t * TOP_K)
        eidx = idx[:, :TOP_K].reshape(-1)
        onehot = eidx[:, None] == jnp.arange(ne, dtype=I32)[None, :]
        dest = rank[:, :TOP_K].reshape(-1) + jnp.sum(jnp.where(onehot, row_start[None, :], 0), axis=1)
        xs = _dispatch(dest, zero_start, nsb, hp, max_sb * SUPER_ROWS)
        ys = _moe_ffn(sb_e, sb_nb, nsb, xs, w_e_in[l], b_e_in[l], w_e_out[l], b_e_out[l], max_sb)
        x = _combine(dest, gates, x1, gt2, g_post_ffn[l], ys, seq).reshape(bsz, seq, d)
    return x
```

```python
import functools

import numpy as np
import jax
import jax.numpy as jnp
from jax import lax
from jax.experimental import pallas as pl
from jax.experimental.pallas import tpu as pltpu

F32 = jnp.float32
BF16 = jnp.bfloat16
U32 = jnp.uint32
I32 = jnp.int32

HEAD_DIM = 128
N_HEADS_NA = 8
N_HEADS_DIL = 8
GRID_W = 64
NA_ROWS = 8
NA_COLS = 16
DIL_PATTERNS = ((128, 1), (512, 4), (2048, 16))
T5_BUCKETS = 32
T5_MAX_DIST = 1024
N_EXPERTS = 32
TOP_K = 4
SWIGLU_LIMIT = 7.0
SWIGLU_ALPHA = 1.702
EPS = 1e-6
NEG = -1e30
SCALE = HEAD_DIM ** -0.5

LANES = 128
QBLK = 128
ROW_BLK = 256
SUPER_BLKS = 7
SUPER_ROWS = ROW_BLK * SUPER_BLKS
UP_CHUNK = 512
DOWN_CHUNK = 1024
VMEM_LIMIT = 56 * 1024 * 1024


def _cparams(sem, vmem=VMEM_LIMIT):
    return pltpu.CompilerParams(dimension_semantics=sem, vmem_limit_bytes=vmem)


def _rms(x, g):
    return x * lax.rsqrt(jnp.mean(x * x, axis=-1, keepdims=True) + EPS) * g


def _ada_kernel(c_ref, w_ref, b_ref, o_ref):
    c = c_ref[...]
    s = c * jax.nn.sigmoid(c)
    o_ref[...] = jnp.dot(s.astype(BF16), w_ref[...].astype(BF16),
                         preferred_element_type=F32) + b_ref[...]


def _ada_mod(c, w, b):
    bsz, d = c.shape
    n = w.shape[1]
    tn = 1024
    cp = jnp.zeros((8, d), F32).at[:bsz].set(c)
    out = pl.pallas_call(
        _ada_kernel,
        out_shape=jax.ShapeDtypeStruct((8, n), F32),
        grid=(n // tn,),
        in_specs=[pl.BlockSpec((8, d), lambda j: (0, 0)),
                  pl.BlockSpec((d, tn), lambda j: (0, j)),
                  pl.BlockSpec((1, tn), lambda j: (0, j))],
        out_specs=pl.BlockSpec((8, tn), lambda j: (0, j)),
        compiler_params=_cparams(("arbitrary",)),
        name="ada_mod",
    )(cp, w, b.reshape(1, n))
    return out[:bsz]


QKV_SUB = 256


def _qkv_kernel(x_ref, g_ref, sc_ref, sh_ref, w_ref, o_ref, h_scr):
    def emit(acc, rows):
        for u in range(o_ref.shape[0]):
            o_ref[u, rows, :] = acc[:, u * LANES:(u + 1) * LANES].astype(BF16)

    @pl.when(pl.program_id(1) == 0)
    def _():
        subs = [slice(i * QKV_SUB, (i + 1) * QKV_SUB) for i in range(x_ref.shape[0] // QKV_SUB)]
        hs = []
        for r in subs:
            h = (_rms(x_ref[r, :], g_ref[...]) * (1.0 + sc_ref[...]) + sh_ref[...]).astype(BF16)
            h_scr[r, :] = h
            hs.append(h)
        accs = [jnp.dot(h, w_ref[...], preferred_element_type=F32) for h in hs]
        for r, acc in zip(subs, accs):
            emit(acc, r)

    @pl.when(pl.program_id(1) > 0)
    def _():
        emit(jnp.dot(h_scr[...], w_ref[...], preferred_element_type=F32), slice(None))


def _qkv_proj(x2, g, sc, sh, w_bf, seq):
    t, d = x2.shape
    n = w_bf.shape[1]
    tm, tn = 1024, 1024
    return pl.pallas_call(
        _qkv_kernel,
        out_shape=jax.ShapeDtypeStruct((n // LANES, t, LANES), BF16),
        grid=(t // tm, n // tn),
        in_specs=[pl.BlockSpec((tm, d), lambda i, j: (i, 0)),
                  pl.BlockSpec((1, d), lambda i, j: (0, 0)),
                  pl.BlockSpec((None, 1, d), lambda i, j: (i * tm // seq, 0, 0)),
                  pl.BlockSpec((None, 1, d), lambda i, j: (i * tm // seq, 0, 0)),
                  pl.BlockSpec((d, tn), lambda i, j: (0, j))],
        out_specs=pl.BlockSpec((tn // LANES, tm, LANES), lambda i, j: (j, i, 0)),
        scratch_shapes=[pltpu.VMEM((tm, d), BF16)],
        compiler_params=_cparams(("arbitrary", "arbitrary")),
        name="qkv_proj",
    )(x2, g.reshape(1, d), sc, sh, w_bf)


def _toeplitz(vec, rows, cols):
    n = rows + cols - 1
    assert vec.shape[-1] == n
    lead = vec.shape[:-1]
    ext = jnp.concatenate([vec, jnp.zeros(lead + (1,), vec.dtype)], axis=-1)
    flat = jnp.broadcast_to(ext[..., None, :], lead + (rows, n + 1)).reshape(lead + (rows * (n + 1),))
    skew = flat[..., :rows * n].reshape(lead + (rows, n))
    return skew[..., rows - 1:rows - 1 + cols]


def _na_bias_table(rpb):
    cidx = np.arange(GRID_W)
    col_start = np.clip(cidx - NA_COLS // 2, 0, GRID_W - NA_COLS)
    col_ok = (cidx[None, :] >= col_start[:, None]) & (cidx[None, :] < col_start[:, None] + NA_COLS)
    pad = GRID_W - NA_COLS
    vec = jnp.pad(rpb.astype(F32), ((0, 0), (0, 0), (pad, pad)))
    return jnp.where(col_ok, _toeplitz(vec, GRID_W, GRID_W), NEG)


def _attn_group(qs, ks, vs, biases):
    ss = [lax.dot_general(q, k, (((1,), (1,)), ((), ())), preferred_element_type=F32) * SCALE + b
          for q, k, b in zip(qs, ks, biases)]
    ms = [jnp.max(s, axis=-1, keepdims=True) for s in ss]
    ps = [jnp.exp(s - m) for s, m in zip(ss, ms)]
    ls = [jnp.sum(p, axis=-1, keepdims=True) for p in ps]
    os = [jnp.dot(p.astype(BF16), v, preferred_element_type=F32) / l for p, v, l in zip(ps, vs, ls)]
    return os, [m + jnp.log(l) for m, l in zip(ms, ls)]


NA_GROUP = 8


def _na_kernel(q_ref, k_ref, v_ref, tab_ref, o_ref, bias_ref, *, rows):
    nkeys = NA_ROWS * GRID_W

    @pl.when(pl.program_id(1) == 0)
    def _():
        for var in range(NA_ROWS):
            bias_ref[var] = jnp.concatenate([tab_ref[var + kr] for kr in range(NA_ROWS)], axis=1)

    def body(g, carry):
        qs, ks, vs, bs, q0s = [], [], [], [], []
        for i in range(NA_GROUP):
            r = g * NA_GROUP + i
            rs = jnp.clip(r - NA_ROWS // 2, 0, rows - NA_ROWS)
            q0 = pl.multiple_of(r * GRID_W, GRID_W)
            k0 = pl.multiple_of(rs * GRID_W, GRID_W)
            q0s.append(q0)
            qs.append(q_ref[pl.ds(q0, GRID_W), :])
            ks.append(k_ref[pl.ds(k0, nkeys), :])
            vs.append(v_ref[pl.ds(k0, nkeys), :])
            bs.append(bias_ref[rs - r + (NA_ROWS - 1)])
        os, _ = _attn_group(qs, ks, vs, bs)
        for q0, o in zip(q0s, os):
            o_ref[pl.ds(q0, GRID_W), :] = o
        return carry

    lax.fori_loop(0, rows // NA_GROUP, body, 0)


def _na_attn(proj, bias, bsz, seq):
    nh = N_HEADS_NA
    t = bsz * seq
    blk = lambda off: pl.BlockSpec((None, seq, LANES), lambda h, b: (h + off, b, 0))
    return pl.pallas_call(
        functools.partial(_na_kernel, rows=seq // GRID_W),
        out_shape=jax.ShapeDtypeStruct((nh, t, LANES), F32),
        grid=(nh, bsz),
        in_specs=[blk(0), blk(nh), blk(2 * nh),
                  pl.BlockSpec((None, 2 * NA_ROWS - 1, GRID_W, GRID_W), lambda h, b: (h, 0, 0, 0))],
        out_specs=pl.BlockSpec((None, seq, LANES), lambda h, b: (h, b, 0)),
        scratch_shapes=[pltpu.VMEM((NA_ROWS, GRID_W, NA_ROWS * GRID_W), F32)],
        compiler_params=_cparams(("arbitrary", "arbitrary")),
        name="na_attn",
    )(proj, proj, proj, bias)


def _t5_bucket(rel):
    nb = T5_BUCKETS // 2
    max_exact = nb // 2
    ret = (rel > 0).astype(np.int32) * nb
    n = np.abs(rel)
    large = max_exact + (np.log(np.maximum(n, 1) / max_exact) / np.log(T5_MAX_DIST / max_exact)
                         * (nb - max_exact)).astype(np.int32)
    large = np.minimum(large, nb - 1)
    return (ret + np.where(n < max_exact, n, large)).astype(np.int32)


def _dil_geometry(sub_len):
    half = DIL_PATTERNS[0][0] // 2
    width = min(sub_len, QBLK + 2 * half)
    nblk = sub_len // QBLK
    starts = [min(max(QBLK * n - half, 0), sub_len - width) for n in range(nblk)]
    offs = sorted({ws - QBLK * n for n, ws in enumerate(starts)}, reverse=True)
    var = [offs.index(ws - QBLK * n) for n, ws in enumerate(starts)]
    return width, starts, offs, var


def _dil_bias_vecs(t5_table, seq):
    half = DIL_PATTERNS[0][0] // 2
    deltas, dils, counts, lens = [], [], [], []
    for _, dil in DIL_PATTERNS:
        width, _, offs, _ = _dil_geometry(seq // dil)
        counts.append(len(offs))
        lens.append(QBLK + width)
        for off in offs:
            d = np.full(2 * QBLK + 2 * half, 4 * half, np.int64)
            d[:QBLK + width - 1] = np.arange(QBLK + width - 1) - (QBLK - 1) + off
            deltas.append(d)
            dils.append(dil)
    delta = np.stack(deltas)
    onehot = np.eye(T5_BUCKETS, dtype=np.float32)[_t5_bucket(delta * np.asarray(dils)[:, None])]
    vals = jnp.einsum("vkb,bh->hvk", jnp.asarray(onehot), t5_table.astype(F32),
                      precision=lax.Precision.HIGHEST)
    vals = jnp.where(np.abs(delta) <= half, vals, NEG)
    out, v0 = [], 0
    for n, ln in zip(counts, lens):
        out.append(vals[:, v0:v0 + n, :ln])
        v0 += n
    return out


DIL_GROUP = 8


def _dil_kernel(q_ref, k_ref, v_ref, v1_ref, v4_ref, v16_ref, o_ref, qf, kf, vf, o4, l4, o16, l16,
                b1_ref, b4_ref, b16_ref, *, seq):
    @pl.when(pl.program_id(1) == 0)
    def _():
        for vec_ref, tab_ref in ((v1_ref, b1_ref), (v4_ref, b4_ref), (v16_ref, b16_ref)):
            n = vec_ref.shape[1]
            for v in range(vec_ref.shape[0]):
                full = jnp.broadcast_to(vec_ref[v:v + 1, :], (QBLK, n))
                tab_ref[v] = pltpu.roll(full, n - (QBLK - 1), 1, stride=1, stride_axis=0)[:, :n - QBLK]

    qf[...] = q_ref[...].astype(F32)
    kf[...] = k_ref[...].astype(F32)
    vf[...] = v_ref[...].astype(F32)

    for dil, bias_ref, o_s, l_s in ((DIL_PATTERNS[2][1], b16_ref, o16, l16), (DIL_PATTERNS[1][1], b4_ref, o4, l4)):
        width, starts, _, var = _dil_geometry(seq // dil)
        blocks = [(rho, n, ws) for rho in range(dil) for n, ws in enumerate(starts)]
        for g in range(0, len(blocks), DIL_GROUP):
            grp = blocks[g:g + DIL_GROUP]
            qrows = [pl.ds(rho + dil * QBLK * n, QBLK, stride=dil) for rho, n, _ in grp]
            krows = [pl.ds(rho + dil * ws, width, stride=dil) for rho, _, ws in grp]
            os, lses = _attn_group([qf[r, :].astype(BF16) for r in qrows],
                                   [kf[r, :].astype(BF16) for r in krows],
                                   [vf[r, :].astype(BF16) for r in krows],
                                   [bias_ref[var[n]] for _, n, _ in grp])
            for r, o, lse in zip(qrows, os, lses):
                o_s[r, :] = o
                l_s[r, :] = jnp.broadcast_to(lse, (QBLK, LANES))

    width, starts, _, _ = _dil_geometry(seq)
    nblk = len(starts)
    half = DIL_PATTERNS[0][0] // 2

    def body(g, carry):
        rows, krows, bs = [], [], []
        for i in range(DIL_GROUP):
            n = g * DIL_GROUP + i
            ws = pl.multiple_of(jnp.clip(n * QBLK - half, 0, seq - width), half)
            rows.append(pl.ds(pl.multiple_of(n * QBLK, QBLK), QBLK))
            krows.append(pl.ds(ws, width))
            bs.append(b1_ref[jnp.where(n == 0, 0, jnp.where(n == nblk - 1, 2, 1))])
        os, lses = _attn_group([q_ref[r, :] for r in rows], [k_ref[r, :] for r in krows],
                               [v_ref[r, :] for r in krows], bs)
        for r, o1, lse1 in zip(rows, os, lses):
            lse4, lse16 = l4[r, :], l16[r, :]
            mx = jnp.maximum(jnp.maximum(lse4, lse16), lse1)
            e1 = jnp.exp(lse1 - mx)
            e4 = jnp.exp(lse4 - mx)
            e16 = jnp.exp(lse16 - mx)
            o_ref[r, :] = (e1 * o1 + e4 * o4[r, :] + e16 * o16[r, :]) / (e1 + e4 + e16)
        return carry

    lax.fori_loop(0, nblk // DIL_GROUP, body, 0)


def _dil_attn(proj, t5_table, bsz, seq):
    nh = N_HEADS_DIL
    vecs = _dil_bias_vecs(t5_table, seq)
    width, _, offs, var = _dil_geometry(seq)
    assert offs == [0, -(DIL_PATTERNS[0][0] // 2), -DIL_PATTERNS[0][0]] and var[0] == 0 and var[-1] == 2
    first = 3 * N_HEADS_NA
    blk = lambda off: pl.BlockSpec((None, seq, LANES), lambda h, b: (first + off + h, b, 0))
    vec = lambda t: pl.BlockSpec((None,) + t.shape[1:], lambda h, b: (h, 0, 0))
    return pl.pallas_call(
        functools.partial(_dil_kernel, seq=seq),
        out_shape=jax.ShapeDtypeStruct((nh, bsz * seq, LANES), F32),
        grid=(nh, bsz),
        in_specs=[blk(0), blk(nh), blk(2 * nh)] + [vec(t) for t in vecs],
        out_specs=pl.BlockSpec((None, seq, LANES), lambda h, b: (h, b, 0)),
        scratch_shapes=[pltpu.VMEM((seq, LANES), F32)] * 7
        + [pltpu.VMEM((t.shape[1], QBLK, t.shape[2] - QBLK), F32) for t in vecs],
        compiler_params=_cparams(("arbitrary", "arbitrary")),
        name="dil_attn",
    )(proj, proj, proj, *vecs)


MIX_SUB = 256


def _mix_kernel(na_ref, dl_ref, gna_ref, gdl_ref, wo_ref, x_ref, gt1_ref, sc2_ref, sh2_ref,
                gpost_ref, gpre_ref, wr_ref, br_ref, x1_ref, hp_ref, lg_ref):
    nh = na_ref.shape[0]
    tm = x_ref.shape[0]
    subs = [slice(i * MIX_SUB, (i + 1) * MIX_SUB) for i in range(tm // MIX_SUB)]
    lhs = []
    for r in subs:
        na = jnp.concatenate([na_ref[h, r, :] for h in range(nh)], axis=1)
        dl = jnp.concatenate([dl_ref[h, r, :] for h in range(nh)], axis=1)
        lhs.append(jnp.concatenate([_rms(na, gna_ref[...]), _rms(dl, gdl_ref[...])], axis=1).astype(BF16))
    mixed = [jnp.dot(a, wo_ref[...], preferred_element_type=F32) for a in lhs]
    for r, m in zip(subs, mixed):
        x1 = x_ref[r, :] + gt1_ref[...] * _rms(m, gpost_ref[...])
        x1_ref[r, :] = x1
        hf = _rms(x1, gpre_ref[...]) * (1.0 + sc2_ref[...]) + sh2_ref[...]
        hb = hf.astype(BF16)
        lg_ref[r, :] = jnp.dot(hb, wr_ref[...], preferred_element_type=F32) + br_ref[...]
        half = hb.shape[1] // 2
        lo = lax.bitcast_convert_type(hb[:, :half].astype(F32), U32)
        hi = lax.bitcast_convert_type(hb[:, half:].astype(F32), U32)
        hp_ref[r, 0, :] = (hi & U32(0xFFFF0000)) | lax.shift_right_logical(lo, U32(16))


def _mix_out(out_na, out_dil, g_na, g_dil, wo_bf, x2, gt1, sc2, sh2, g_post, g_pre, wr_bf, br, seq):
    t, d = x2.shape
    nh = out_na.shape[0]
    tm = 512
    wna = g_na.shape[-1]
    row = lambda n: pl.BlockSpec((1, n), lambda i: (0, 0))
    per_b = pl.BlockSpec((None, 1, d), lambda i: (i * tm // seq, 0, 0))
    heads = pl.BlockSpec((nh, tm, LANES), lambda i: (0, i, 0))
    return pl.pallas_call(
        _mix_kernel,
        out_shape=(jax.ShapeDtypeStruct((t, d), F32),
                   jax.ShapeDtypeStruct((t, 1, d // 2), U32),
                   jax.ShapeDtypeStruct((t, LANES), F32)),
        grid=(t // tm,),
        in_specs=[heads, heads, row(wna), row(wna),
                  pl.BlockSpec((d, d), lambda i: (0, 0)),
                  pl.BlockSpec((tm, d), lambda i: (i, 0)),
                  per_b, per_b, per_b, row(d), row(d),
                  pl.BlockSpec((d, LANES), lambda i: (0, 0)), row(LANES)],
        out_specs=(pl.BlockSpec((tm, d), lambda i: (i, 0)),
                   pl.BlockSpec((tm, 1, d // 2), lambda i: (i, 0, 0)),
                   pl.BlockSpec((tm, LANES), lambda i: (i, 0))),
        compiler_params=_cparams(("arbitrary",)),
        name="mix_out",
    )(out_na, out_dil, g_na.reshape(1, wna), g_dil.reshape(1, wna), wo_bf, x2, gt1, sc2, sh2,
      g_post.reshape(1, d), g_pre.reshape(1, d), wr_bf, br)


def _route_kernel(lg_ref, idx_ref, rank_ref, gate_ref, cnt_ref, carry):
    step = pl.program_id(0)

    @pl.when(step == 0)
    def _():
        carry[...] = jnp.zeros_like(carry)

    v = lg_ref[...]
    ch = v.shape[0]
    lane = lax.broadcasted_iota(I32, v.shape, 1).astype(F32)
    vals, idxs = [], []
    for _ in range(TOP_K):
        m = jnp.max(v, axis=1, keepdims=True)
        ik = jnp.min(jnp.where(v == m, lane, float(LANES)), axis=1, keepdims=True)
        vals.append(m)
        idxs.append(ik)
        v = jnp.where(lane == ik, -jnp.inf, v)
    es = [jnp.exp(val - vals[0]) for val in vals]
    den = es[0] + es[1] + es[2] + es[3]
    sel = jnp.zeros(v.shape, F32)
    for ik in idxs:
        sel = jnp.where(lane == ik, 1.0, sel)
    ti = lax.broadcasted_iota(I32, (ch, ch), 0)
    tj = lax.broadcasted_iota(I32, (ch, ch), 1)
    lower = jnp.where(tj < ti, 1.0, 0.0).astype(BF16)
    cum = jnp.dot(lower, sel.astype(BF16), preferred_element_type=F32) + carry[0:1, :]
    idx_o = jnp.zeros(v.shape, F32)
    rank_o = jnp.zeros(v.shape, F32)
    gate_o = jnp.zeros(v.shape, F32)
    for k in range(TOP_K):
        rk = jnp.sum(jnp.where(lane == idxs[k], cum, 0.0), axis=1, keepdims=True)
        idx_o = jnp.where(lane == float(k), idxs[k], idx_o)
        rank_o = jnp.where(lane == float(k), rk, rank_o)
        gate_o = jnp.where(lane == float(k), es[k] / den, gate_o)
    idx_ref[...] = idx_o.astype(I32)
    rank_ref[...] = rank_o.astype(I32)
    gate_ref[...] = gate_o
    total = carry[...] + jnp.sum(sel, axis=0, keepdims=True)
    carry[...] = total
    cnt_ref[...] = total.astype(I32)


def _route(logits):
    t = logits.shape[0]
    ch = 512
    blk = pl.BlockSpec((ch, LANES), lambda i: (i, 0))
    return pl.pallas_call(
        _route_kernel,
        out_shape=(jax.ShapeDtypeStruct((t, LANES), I32),
                   jax.ShapeDtypeStruct((t, LANES), I32),
                   jax.ShapeDtypeStruct((t, LANES), F32),
                   jax.ShapeDtypeStruct((8, LANES), I32)),
        grid=(t // ch,),
        in_specs=[blk],
        out_specs=(blk, blk, blk, pl.BlockSpec((8, LANES), lambda i: (0, 0))),
        scratch_shapes=[pltpu.VMEM((8, LANES), F32)],
        compiler_params=_cparams(("arbitrary",)),
        name="route",
    )(logits)


def _schedule(counts, n_assign):
    max_sb = n_assign // SUPER_ROWS + N_EXPERTS
    nsb_e = (counts + SUPER_ROWS - 1) // SUPER_ROWS
    sb_end = jnp.cumsum(nsb_e)
    sb_start = sb_end - nsb_e
    row_start = (sb_start * SUPER_ROWS).astype(I32)
    nsb = sb_end[-1].astype(I32)
    s = jnp.arange(max_sb, dtype=I32)
    sb_e = jnp.minimum(jnp.searchsorted(sb_end, s, side="right"), N_EXPERTS - 1).astype(I32)
    rem = counts[sb_e] - (s - sb_start[sb_e]) * SUPER_ROWS
    sb_nb = jnp.where(s < nsb, (jnp.clip(rem, 0, SUPER_ROWS) + ROW_BLK - 1) // ROW_BLK, 0).astype(I32)
    zero_start = (s * SUPER_ROWS + jnp.maximum(sb_nb - 1, 0) * ROW_BLK).astype(I32)
    return max_sb, row_start, nsb.reshape(1), sb_e, sb_nb, zero_start


ROW_UNROLL = 8


def _dispatch_kernel(dest_s, zs_s, nsb_s, hp_ref, xs_hbm, zbuf, zsem, sem):
    step = pl.program_id(0)
    tt = hp_ref.shape[0]

    def zero_copy(s):
        return pltpu.make_async_copy(zbuf, xs_hbm.at[pl.ds(pl.multiple_of(zs_s[s], ROW_BLK), ROW_BLK)], zsem)

    @pl.when(step == 0)
    def _():
        zbuf[...] = jnp.zeros_like(zbuf)

        def zstart(s, c):
            zero_copy(s).start()
            return c

        def zwait(s, c):
            zero_copy(s).wait()
            return c

        lax.fori_loop(0, nsb_s[0], zstart, 0)
        lax.fori_loop(0, nsb_s[0], zwait, 0)

    base = step * tt * TOP_K

    def issue(jj, c):
        for u in range(ROW_UNROLL):
            j = jj * ROW_UNROLL + u
            for k in range(TOP_K):
                d = dest_s[base + j * TOP_K + k]
                pltpu.make_async_copy(hp_ref.at[pl.ds(j, 1)], xs_hbm.at[pl.ds(d, 1)], sem).start(
                    priority=k % 2)
        return c

    lax.fori_loop(0, tt // ROW_UNROLL, issue, 0)
    for k in range(TOP_K):
        pltpu.make_async_copy(hp_ref, xs_hbm.at[pl.ds(0, tt)], sem).wait()


def _dispatch(dest, zero_start, nsb, hp, n_rows):
    t, _, w = hp.shape
    tt = 256
    return pl.pallas_call(
        _dispatch_kernel,
        out_shape=jax.ShapeDtypeStruct((n_rows, 1, w), U32),
        grid_spec=pltpu.PrefetchScalarGridSpec(
            num_scalar_prefetch=3,
            grid=(t // tt,),
            in_specs=[pl.BlockSpec((tt, 1, w), lambda i, *_: (i, 0, 0))],
            out_specs=pl.BlockSpec(memory_space=pl.ANY),
            scratch_shapes=[pltpu.VMEM((ROW_BLK, 1, w), U32),
                            pltpu.SemaphoreType.DMA(()), pltpu.SemaphoreType.DMA(())]),
        compiler_params=_cparams(("arbitrary",)),
        name="dispatch",
    )(dest, zero_start, nsb, hp)


def _row_block(r):
    return pl.ds(pl.multiple_of(r * ROW_BLK, ROW_BLK), ROW_BLK)


def _moe_up_kernel(sbe_s, sbn_s, x_ref, win_ref, bin_ref, h_ref, winb, zbuf, xstage):
    nb = sbn_s[pl.program_id(0)]
    half = x_ref.shape[2]
    nsub = win_ref.shape[1] // (2 * LANES)
    lane = lax.broadcasted_iota(I32, (ROW_BLK, LANES), 1)
    even = (2 * lane) & (LANES - 1)
    odd = even + 1
    first_half = lane < LANES // 2

    def unpack(r):
        xstage[...] = x_ref[_row_block(r), 0, :]
        xu = xstage[...]
        xa = lax.bitcast_convert_type(lax.shift_left(xu, U32(16)), F32).astype(BF16)
        xb = lax.bitcast_convert_type(xu & U32(0xFFFF0000), F32).astype(BF16)
        return xa, xb

    def matmul(r):
        xa, xb = unpack(r)
        zbuf[...] = (jnp.dot(xa, winb[0:half, :], preferred_element_type=F32)
                     + jnp.dot(xb, winb[half:2 * half, :], preferred_element_type=F32) + bin_ref[...])

    def matmul_casting(r):
        xa, xb = unpack(r)
        kq = half // 2
        acc = bin_ref[...]
        for ks in range(4):
            wb = win_ref[ks * kq:(ks + 1) * kq, :].astype(BF16)
            winb[ks * kq:(ks + 1) * kq, :] = wb
            xpart = (xa, xb)[ks // 2][:, (ks % 2) * kq:(ks % 2 + 1) * kq]
            acc = acc + jnp.dot(xpart, wb, preferred_element_type=F32)
        zbuf[...] = acc

    def activation():
        hs = []
        for u in range(nsub):
            za = zbuf[:, (2 * u) * LANES:(2 * u + 1) * LANES]
            zb = zbuf[:, (2 * u + 1) * LANES:(2 * u + 2) * LANES]
            gate = jnp.where(first_half, jnp.take_along_axis(za, even, axis=1),
                             jnp.take_along_axis(zb, even, axis=1))
            up = jnp.where(first_half, jnp.take_along_axis(za, odd, axis=1),
                           jnp.take_along_axis(zb, odd, axis=1))
            gate = jnp.minimum(gate, SWIGLU_LIMIT)
            up = jnp.clip(up, -SWIGLU_LIMIT, SWIGLU_LIMIT)
            glu = gate * jax.nn.sigmoid(SWIGLU_ALPHA * gate)
            hs.append(((up + 1.0) * glu).astype(BF16))
        return jnp.concatenate(hs, axis=1)

    matmul_casting(0)

    def step(r, carry):
        h = activation()
        matmul(r + 1)
        h_ref[_row_block(r), :] = h
        return carry

    lax.fori_loop(0, nb - 1, step, 0)
    h_ref[_row_block(nb - 1), :] = activation()


def _moe_down_kernel(sbe_s, sbn_s, h_ref, wout_ref, bout_ref, o_ref, woutb):
    nb = sbn_s[pl.program_id(0)]
    kq = h_ref.shape[1] // 4
    h0 = h_ref[_row_block(0), :]
    acc = bout_ref[...]
    for ks in range(4):
        wb = wout_ref[ks * kq:(ks + 1) * kq, :].astype(BF16)
        woutb[ks * kq:(ks + 1) * kq, :] = wb
        acc = acc + jnp.dot(h0[:, ks * kq:(ks + 1) * kq], wb, preferred_element_type=F32)
    o_ref[_row_block(0), 0, :] = acc

    def project(r):
        return jnp.dot(h_ref[_row_block(r), :], woutb[...], preferred_element_type=F32) + bout_ref[...]

    def pair(p, carry):
        r = 1 + 2 * p
        y0, y1 = project(r), project(r + 1)
        o_ref[_row_block(r), 0, :] = y0
        o_ref[_row_block(r + 1), 0, :] = y1
        return carry

    lax.fori_loop(0, (nb - 1) // 2, pair, 0)

    @pl.when((nb - 1) % 2 == 1)
    def _():
        o_ref[_row_block(nb - 1), 0, :] = project(nb - 1)


def _moe_ffn(sb_e, sb_nb, nsb, xs, w_in, b_in, w_out, b_out, max_sb):
    ne, d, f2 = w_in.shape
    ff = w_out.shape[1]
    half = xs.shape[2]
    rows = max_sb * SUPER_ROWS
    sem = ("arbitrary", "arbitrary")

    h = pl.pallas_call(
        _moe_up_kernel,
        out_shape=jax.ShapeDtypeStruct((rows, ff), BF16),
        grid_spec=pltpu.PrefetchScalarGridSpec(
            num_scalar_prefetch=2,
            grid=(nsb[0], ff // UP_CHUNK),
            in_specs=[pl.BlockSpec((SUPER_ROWS, 1, half), lambda s, j, sbe, sbn: (s, 0, 0)),
                      pl.BlockSpec((None, d, 2 * UP_CHUNK), lambda s, j, sbe, sbn: (sbe[s], 0, j)),
                      pl.BlockSpec((None, 1, 2 * UP_CHUNK), lambda s, j, sbe, sbn: (sbe[s], 0, j))],
            out_specs=pl.BlockSpec((SUPER_ROWS, UP_CHUNK), lambda s, j, sbe, sbn: (s, j)),
            scratch_shapes=[pltpu.VMEM((d, 2 * UP_CHUNK), BF16), pltpu.VMEM((ROW_BLK, 2 * UP_CHUNK), F32),
                            pltpu.VMEM((ROW_BLK, half), U32)]),
        compiler_params=_cparams(sem),
        name="moe_up",
    )(sb_e, sb_nb, xs, w_in, b_in.reshape(ne, 1, f2))

    return pl.pallas_call(
        _moe_down_kernel,
        out_shape=jax.ShapeDtypeStruct((rows, 1, d), F32),
        grid_spec=pltpu.PrefetchScalarGridSpec(
            num_scalar_prefetch=2,
            grid=(nsb[0], d // DOWN_CHUNK),
            in_specs=[pl.BlockSpec((SUPER_ROWS, ff), lambda s, n, sbe, sbn: (s, 0)),
                      pl.BlockSpec((None, ff, DOWN_CHUNK), lambda s, n, sbe, sbn: (sbe[s], 0, n)),
                      pl.BlockSpec((None, 1, DOWN_CHUNK), lambda s, n, sbe, sbn: (sbe[s], 0, n))],
            out_specs=pl.BlockSpec((SUPER_ROWS, 1, DOWN_CHUNK), lambda s, n, sbe, sbn: (s, 0, n)),
            scratch_shapes=[pltpu.VMEM((ff, DOWN_CHUNK), BF16)]),
        compiler_params=_cparams(sem),
        name="moe_down",
    )(sb_e, sb_nb, h, w_out, b_out.reshape(ne, 1, d))


COMBINE_GROUP = 32


def _combine_kernel(dest_s, gate_ref, x1_ref, gt2_ref, g_ref, ys_hbm, o_ref, buf, stage, sem):
    step = pl.program_id(0)
    nstep = pl.num_programs(0)
    tt = x1_ref.shape[0]
    slot = step % 2

    def issue(tile, slot_, jj):
        base = tile * tt * TOP_K
        for u in range(COMBINE_GROUP):
            j = jj * COMBINE_GROUP + u
            for k in range(TOP_K):
                d = dest_s[base + j * TOP_K + k]
                pltpu.make_async_copy(ys_hbm.at[pl.ds(d, 1)], buf.at[slot_, k, pl.ds(j, 1)],
                                      sem.at[slot_]).start(priority=k % 2)

    def finish(jj):
        rows = pl.ds(pl.multiple_of(jj * COMBINE_GROUP, COMBINE_GROUP), COMBINE_GROUP)
        g = gate_ref[rows, :]
        y = None
        for k in range(TOP_K):
            stage[k] = buf[slot, k, rows, 0, :]
            term = stage[k] * g[:, k:k + 1]
            y = term if y is None else y + term
        o_ref[rows, :] = x1_ref[rows, :] + gt2_ref[...] * _rms(y, g_ref[...])

    @pl.when(step == 0)
    def _():
        def first(jj, c):
            issue(0, 0, jj)
            return c

        lax.fori_loop(0, tt // COMBINE_GROUP, first, 0)

    for k in range(TOP_K):
        pltpu.make_async_copy(ys_hbm.at[pl.ds(0, tt)], buf.at[slot, k], sem.at[slot]).wait()

    @pl.when(step + 1 < nstep)
    def _():
        def both(jj, c):
            finish(jj)
            issue(step + 1, 1 - slot, jj)
            return c

        lax.fori_loop(0, tt // COMBINE_GROUP, both, 0)

    @pl.when(step + 1 == nstep)
    def _():
        def last(jj, c):
            finish(jj)
            return c

        lax.fori_loop(0, tt // COMBINE_GROUP, last, 0)


def _combine(dest, gates, x1, gt2, g_post, ys, seq):
    t, d = x1.shape
    tt = 128
    return pl.pallas_call(
        _combine_kernel,
        out_shape=jax.ShapeDtypeStruct((t, d), F32),
        grid_spec=pltpu.PrefetchScalarGridSpec(
            num_scalar_prefetch=1,
            grid=(t // tt,),
            in_specs=[pl.BlockSpec((tt, LANES), lambda i, *_: (i, 0)),
                      pl.BlockSpec((tt, d), lambda i, *_: (i, 0)),
                      pl.BlockSpec((None, 1, d), lambda i, *_: (i * tt // seq, 0, 0)),
                      pl.BlockSpec((1, d), lambda i, *_: (0, 0)),
                      pl.BlockSpec(memory_space=pl.ANY)],
            out_specs=pl.BlockSpec((tt, d), lambda i, *_: (i, 0)),
            scratch_shapes=[pltpu.VMEM((2, TOP_K, tt, 1, d), F32), pltpu.VMEM((TOP_K, COMBINE_GROUP, d), F32),
                            pltpu.SemaphoreType.DMA((2,))]),
        compiler_params=_cparams(("arbitrary",)),
        name="combine",
    )(dest, gates, x1, gt2, g_post.reshape(1, d), ys)


def kernel(x, c, w_ada, b_ada, g_pre_mix, g_post_mix, w_in, rpb_na, t5_table, g_out_na, g_out_dil, w_o,
           g_pre_ffn, g_post_ffn, w_router, b_router, w_e_in, b_e_in, w_e_out, b_e_out):
    bsz, seq, d = x.shape
    t = bsz * seq
    for l in range(w_ada.shape[0]):
        mod = _ada_mod(c, w_ada[l], b_ada[l])
        sh1, sc1, gt1, sh2, sc2, gt2 = (m.reshape(bsz, 1, d) for m in jnp.split(mod, 6, axis=-1))
        x2 = x.reshape(t, d)

        proj = _qkv_proj(x2, g_pre_mix[l], sc1, sh1, w_in[l].astype(BF16), seq)
        out_na = _na_attn(proj, _na_bias_table(rpb_na[l]), bsz, seq)
        out_dil = _dil_attn(proj, t5_table, bsz, seq)

        ne = w_router.shape[-1]
        wr = jnp.zeros((d, LANES), BF16).at[:, :ne].set(w_router[l].astype(BF16))
        br = jnp.full((1, LANES), NEG, F32).at[0, :ne].set(b_router[l])
        x1, hp, logits = _mix_out(out_na, out_dil, g_out_na[l], g_out_dil[l], w_o[l].astype(BF16), x2,
                                  gt1, sc2, sh2, g_post_mix[l], g_pre_ffn[l], wr, br, seq)

        idx, rank, gates, cnt = _route(logits)
        max_sb, row_start, nsb, sb_e, sb_nb, zero_start = _schedule(cnt[0, :ne], t * TOP_K)
        eidx = idx[:, :TOP_K].reshape(-1)
        onehot = eidx[:, None] == jnp.arange(ne, dtype=I32)[None, :]
        dest = rank[:, :TOP_K].reshape(-1) + jnp.sum(jnp.where(onehot, row_start[None, :], 0), axis=1)
        xs = _dispatch(dest, zero_start, nsb, hp, max_sb * SUPER_ROWS)
        ys = _moe_ffn(sb_e, sb_nb, nsb, xs, w_e_in[l], b_e_in[l], w_e_out[l], b_e_out[l], max_sb)
        x = _combine(dest, gates, x1, gt2, g_post_ffn[l], ys, seq).reshape(bsz, seq, d)
    return x
```

```python
import functools

import numpy as np
import jax
import jax.numpy as jnp
from jax import lax
from jax.experimental import pallas as pl
from jax.experimental.pallas import tpu as pltpu

F32 = jnp.float32
BF16 = jnp.bfloat16
U32 = jnp.uint32
I32 = jnp.int32

HEAD_DIM = 128
N_HEADS_NA = 8
N_HEADS_DIL = 8
GRID_W = 64
NA_ROWS = 8
NA_COLS = 16
DIL_PATTERNS = ((128, 1), (512, 4), (2048, 16))
T5_BUCKETS = 32
T5_MAX_DIST = 1024
N_EXPERTS = 32
TOP_K = 4
SWIGLU_LIMIT = 7.0
SWIGLU_ALPHA = 1.702
EPS = 1e-6
NEG = -1e30
SCALE = HEAD_DIM ** -0.5

LANES = 128
QBLK = 128
ROW_BLK = 256
SUPER_BLKS = 7
SUPER_ROWS = ROW_BLK * SUPER_BLKS
UP_CHUNK = 512
DOWN_CHUNK = 1024
VMEM_LIMIT = 56 * 1024 * 1024


def _cparams(sem, vmem=VMEM_LIMIT):
    return pltpu.CompilerParams(dimension_semantics=sem, vmem_limit_bytes=vmem)


def _rms(x, g):
    return x * lax.rsqrt(jnp.mean(x * x, axis=-1, keepdims=True) + EPS) * g


def _ada_kernel(c_ref, w_ref, b_ref, o_ref):
    c = c_ref[...]
    s = c * jax.nn.sigmoid(c)
    o_ref[...] = jnp.dot(s.astype(BF16), w_ref[...].astype(BF16),
                         preferred_element_type=F32) + b_ref[...]


def _ada_mod(c, w, b):
    bsz, d = c.shape
    n = w.shape[1]
    tn = 1024
    cp = jnp.zeros((8, d), F32).at[:bsz].set(c)
    out = pl.pallas_call(
        _ada_kernel,
        out_shape=jax.ShapeDtypeStruct((8, n), F32),
        grid=(n // tn,),
        in_specs=[pl.BlockSpec((8, d), lambda j: (0, 0)),
                  pl.BlockSpec((d, tn), lambda j: (0, j)),
                  pl.BlockSpec((1, tn), lambda j: (0, j))],
        out_specs=pl.BlockSpec((8, tn), lambda j: (0, j)),
        compiler_params=_cparams(("arbitrary",)),
        name="ada_mod",
    )(cp, w, b.reshape(1, n))
    return out[:bsz]


QKV_SUB = 256


def _qkv_kernel(x_ref, g_ref, sc_ref, sh_ref, w_ref, o_ref, h_scr):
    def emit(acc, rows):
        for u in range(o_ref.shape[0]):
            o_ref[u, rows, :] = acc[:, u * LANES:(u + 1) * LANES].astype(BF16)

    @pl.when(pl.program_id(1) == 0)
    def _():
        subs = [slice(i * QKV_SUB, (i + 1) * QKV_SUB) for i in range(x_ref.shape[0] // QKV_SUB)]
        hs = []
        for r in subs:
            h = (_rms(x_ref[r, :], g_ref[...]) * (1.0 + sc_ref[...]) + sh_ref[...]).astype(BF16)
            h_scr[r, :] = h
            hs.append(h)
        accs = [jnp.dot(h, w_ref[...], preferred_element_type=F32) for h in hs]
        for r, acc in zip(subs, accs):
            emit(acc, r)

    @pl.when(pl.program_id(1) > 0)
    def _():
        emit(jnp.dot(h_scr[...], w_ref[...], preferred_element_type=F32), slice(None))


def _qkv_proj(x2, g, sc, sh, w_bf, seq):
    t, d = x2.shape
    n = w_bf.shape[1]
    tm, tn = 1024, 1024
    return pl.pallas_call(
        _qkv_kernel,
        out_shape=jax.ShapeDtypeStruct((n // LANES, t, LANES), BF16),
        grid=(t // tm, n // tn),
        in_specs=[pl.BlockSpec((tm, d), lambda i, j: (i, 0)),
                  pl.BlockSpec((1, d), lambda i, j: (0, 0)),
                  pl.BlockSpec((None, 1, d), lambda i, j: (i * tm // seq, 0, 0)),
                  pl.BlockSpec((None, 1, d), lambda i, j: (i * tm // seq, 0, 0)),
                  pl.BlockSpec((d, tn), lambda i, j: (0, j))],
        out_specs=pl.BlockSpec((tn // LANES, tm, LANES), lambda i, j: (j, i, 0)),
        scratch_shapes=[pltpu.VMEM((tm, d), BF16)],
        compiler_params=_cparams(("arbitrary", "arbitrary")),
        name="qkv_proj",
    )(x2, g.reshape(1, d), sc, sh, w_bf)


def _toeplitz(vec, rows, cols):
    n = rows + cols - 1
    assert vec.shape[-1] == n
    lead = vec.shape[:-1]
    ext = jnp.concatenate([vec, jnp.zeros(lead + (1,), vec.dtype)], axis=-1)
    flat = jnp.broadcast_to(ext[..., None, :], lead + (rows, n + 1)).reshape(lead + (rows * (n + 1),))
    skew = flat[..., :rows * n].reshape(lead + (rows, n))
    return skew[..., rows - 1:rows - 1 + cols]


def _na_bias_table(rpb):
    cidx = np.arange(GRID_W)
    col_start = np.clip(cidx - NA_COLS // 2, 0, GRID_W - NA_COLS)
    col_ok = (cidx[None, :] >= col_start[:, None]) & (cidx[None, :] < col_start[:, None] + NA_COLS)
    pad = GRID_W - NA_COLS
    vec = jnp.pad(rpb.astype(F32), ((0, 0), (0, 0), (pad, pad)))
    return jnp.where(col_ok, _toeplitz(vec, GRID_W, GRID_W), NEG)


def _attn_group(qs, ks, vs, biases):
    ss = [lax.dot_general(q, k, (((1,), (1,)), ((), ())), preferred_element_type=F32) * SCALE + b
          for q, k, b in zip(qs, ks, biases)]
    ms = [jnp.max(s, axis=-1, keepdims=True) for s in ss]
    ps = [jnp.exp(s - m) for s, m in zip(ss, ms)]
    ls = [jnp.sum(p, axis=-1, keepdims=True) for p in ps]
    os = [jnp.dot(p.astype(BF16), v, preferred_element_type=F32) / l for p, v, l in zip(ps, vs, ls)]
    return os, [m + jnp.log(l) for m, l in zip(ms, ls)]


NA_GROUP = 8


def _na_kernel(q_ref, k_ref, v_ref, tab_ref, o_ref, bias_ref, *, rows):
    nkeys = NA_ROWS * GRID_W

    @pl.when(pl.program_id(1) == 0)
    def _():
        for var in range(NA_ROWS):
            bias_ref[var] = jnp.concatenate([tab_ref[var + kr] for kr in range(NA_ROWS)], axis=1)

    def body(g, carry):
        qs, ks, vs, bs, q0s = [], [], [], [], []
        for i in range(NA_GROUP):
            r = g * NA_GROUP + i
            rs = jnp.clip(r - NA_ROWS // 2, 0, rows - NA_ROWS)
            q0 = pl.multiple_of(r * GRID_W, GRID_W)
            k0 = pl.multiple_of(rs * GRID_W, GRID_W)
            q0s.append(q0)
            qs.append(q_ref[pl.ds(q0, GRID_W), :])
            ks.append(k_ref[pl.ds(k0, nkeys), :])
            vs.append(v_ref[pl.ds(k0, nkeys), :])
            bs.append(bias_ref[rs - r + (NA_ROWS - 1)])
        os, _ = _attn_group(qs, ks, vs, bs)
        for q0, o in zip(q0s, os):
            o_ref[pl.ds(q0, GRID_W), :] = o
        return carry

    lax.fori_loop(0, rows // NA_GROUP, body, 0)


def _na_attn(proj, bias, bsz, seq):
    nh = N_HEADS_NA
    t = bsz * seq
    blk = lambda off: pl.BlockSpec((None, seq, LANES), lambda h, b: (h + off, b, 0))
    return pl.pallas_call(
        functools.partial(_na_kernel, rows=seq // GRID_W),
        out_shape=jax.ShapeDtypeStruct((nh, t, LANES), F32),
        grid=(nh, bsz),
        in_specs=[blk(0), blk(nh), blk(2 * nh),
                  pl.BlockSpec((None, 2 * NA_ROWS - 1, GRID_W, GRID_W), lambda h, b: (h, 0, 0, 0))],
        out_specs=pl.BlockSpec((None, seq, LANES), lambda h, b: (h, b, 0)),
        scratch_shapes=[pltpu.VMEM((NA_ROWS, GRID_W, NA_ROWS * GRID_W), F32)],
        compiler_params=_cparams(("arbitrary", "arbitrary")),
        name="na_attn",
    )(proj, proj, proj, bias)


def _t5_bucket(rel):
    nb = T5_BUCKETS // 2
    max_exact = nb // 2
    ret = (rel > 0).astype(np.int32) * nb
    n = np.abs(rel)
    large = max_exact + (np.log(np.maximum(n, 1) / max_exact) / np.log(T5_MAX_DIST / max_exact)
                         * (nb - max_exact)).astype(np.int32)
    large = np.minimum(large, nb - 1)
    return (ret + np.where(n < max_exact, n, large)).astype(np.int32)


def _dil_geometry(sub_len):
    half = DIL_PATTERNS[0][0] // 2
    width = min(sub_len, QBLK + 2 * half)
    nblk = sub_len // QBLK
    starts = [min(max(QBLK * n - half, 0), sub_len - width) for n in range(nblk)]
    offs = sorted({ws - QBLK * n for n, ws in enumerate(starts)}, reverse=True)
    var = [offs.index(ws - QBLK * n) for n, ws in enumerate(starts)]
    return width, starts, offs, var


def _dil_bias_vecs(t5_table, seq):
    half = DIL_PATTERNS[0][0] // 2
    deltas, dils, counts, lens = [], [], [], []
    for _, dil in DIL_PATTERNS:
        width, _, offs, _ = _dil_geometry(seq // dil)
        counts.append(len(offs))
        lens.append(QBLK + width)
        for off in offs:
            d = np.full(2 * QBLK + 2 * half, 4 * half, np.int64)
            d[:QBLK + width - 1] = np.arange(QBLK + width - 1) - (QBLK - 1) + off
            deltas.append(d)
            dils.append(dil)
    delta = np.stack(deltas)
    onehot = np.eye(T5_BUCKETS, dtype=np.float32)[_t5_bucket(delta * np.asarray(dils)[:, None])]
    vals = jnp.einsum("vkb,bh->hvk", jnp.asarray(onehot), t5_table.astype(F32),
                      precision=lax.Precision.HIGHEST)
    vals = jnp.where(np.abs(delta) <= half, vals, NEG)
    out, v0 = [], 0
    for n, ln in zip(counts, lens):
        out.append(vals[:, v0:v0 + n, :ln])
        v0 += n
    return out


DIL_GROUP = 8


def _dil_kernel(q_ref, k_ref, v_ref, v1_ref, v4_ref, v16_ref, o_ref, qf, kf, vf, o4, l4, o16, l16,
                b1_ref, b4_ref, b16_ref, *, seq):
    @pl.when(pl.program_id(1) == 0)
    def _():
        for vec_ref, tab_ref in ((v1_ref, b1_ref), (v4_ref, b4_ref), (v16_ref, b16_ref)):
            n = vec_ref.shape[1]
            for v in range(vec_ref.shape[0]):
                full = jnp.broadcast_to(vec_ref[v:v + 1, :], (QBLK, n))
                tab_ref[v] = pltpu.roll(full, n - (QBLK - 1), 1, stride=1, stride_axis=0)[:, :n - QBLK]

    qf[...] = q_ref[...].astype(F32)
    kf[...] = k_ref[...].astype(F32)
    vf[...] = v_ref[...].astype(F32)

    for dil, bias_ref, o_s, l_s in ((DIL_PATTERNS[2][1], b16_ref, o16, l16), (DIL_PATTERNS[1][1], b4_ref, o4, l4)):
        width, starts, _, var = _dil_geometry(seq // dil)
        blocks = [(rho, n, ws) for rho in range(dil) for n, ws in enumerate(starts)]
        for g in range(0, len(blocks), DIL_GROUP):
            grp = blocks[g:g + DIL_GROUP]
            qrows = [pl.ds(rho + dil * QBLK * n, QBLK, stride=dil) for rho, n, _ in grp]
            krows = [pl.ds(rho + dil * ws, width, stride=dil) for rho, _, ws in grp]
            os, lses = _attn_group([qf[r, :].astype(BF16) for r in qrows],
                                   [kf[r, :].astype(BF16) for r in krows],
                                   [vf[r, :].astype(BF16) for r in krows],
                                   [bias_ref[var[n]] for _, n, _ in grp])
            for r, o, lse in zip(qrows, os, lses):
                o_s[r, :] = o
                l_s[r, :] = jnp.broadcast_to(lse, (QBLK, LANES))

    width, starts, _, _ = _dil_geometry(seq)
    nblk = len(starts)
    half = DIL_PATTERNS[0][0] // 2

    def body(g, carry):
        rows, krows, bs = [], [], []
        for i in range(DIL_GROUP):
            n = g * DIL_GROUP + i
            ws = pl.multiple_of(jnp.clip(n * QBLK - half, 0, seq - width), half)
            rows.append(pl.ds(pl.multiple_of(n * QBLK, QBLK), QBLK))
            krows.append(pl.ds(ws, width))
            bs.append(b1_ref[jnp.where(n == 0, 0, jnp.where(n == nblk - 1, 2, 1))])
        os, lses = _attn_group([q_ref[r, :] for r in rows], [k_ref[r, :] for r in krows],
                               [v_ref[r, :] for r in krows], bs)
        for r, o1, lse1 in zip(rows, os, lses):
            lse4, lse16 = l4[r, :], l16[r, :]
            mx = jnp.maximum(jnp.maximum(lse4, lse16), lse1)
            e1 = jnp.exp(lse1 - mx)
            e4 = jnp.exp(lse4 - mx)
            e16 = jnp.exp(lse16 - mx)
            o_ref[r, :] = (e1 * o1 + e4 * o4[r, :] + e16 * o16[r, :]) / (e1 + e4 + e16)
        return carry

    lax.fori_loop(0, nblk // DIL_GROUP, body, 0)


def _dil_attn(proj, t5_table, bsz, seq):
    nh = N_HEADS_DIL
    vecs = _dil_bias_vecs(t5_table, seq)
    width, _, offs, var = _dil_geometry(seq)
    assert offs == [0, -(DIL_PATTERNS[0][0] // 2), -DIL_PATTERNS[0][0]] and var[0] == 0 and var[-1] == 2
    first = 3 * N_HEADS_NA
    blk = lambda off: pl.BlockSpec((None, seq, LANES), lambda h, b: (first + off + h, b, 0))
    vec = lambda t: pl.BlockSpec((None,) + t.shape[1:], lambda h, b: (h, 0, 0))
    return pl.pallas_call(
        functools.partial(_dil_kernel, seq=seq),
        out_shape=jax.ShapeDtypeStruct((nh, bsz * seq, LANES), F32),
        grid=(nh, bsz),
        in_specs=[blk(0), blk(nh), blk(2 * nh)] + [vec(t) for t in vecs],
        out_specs=pl.BlockSpec((None, seq, LANES), lambda h, b: (h, b, 0)),
        scratch_shapes=[pltpu.VMEM((seq, LANES), F32)] * 7
        + [pltpu.VMEM((t.shape[1], QBLK, t.shape[2] - QBLK), F32) for t in vecs],
        compiler_params=_cparams(("arbitrary", "arbitrary")),
        name="dil_attn",
    )(proj, proj, proj, *vecs)


MIX_SUB = 256


def _mix_kernel(na_ref, dl_ref, gna_ref, gdl_ref, wo_ref, x_ref, gt1_ref, sc2_ref, sh2_ref,
                gpost_ref, gpre_ref, wr_ref, br_ref, x1_ref, hp_ref, lg_ref):
    nh = na_ref.shape[0]
    tm = x_ref.shape[0]
    subs = [slice(i * MIX_SUB, (i + 1) * MIX_SUB) for i in range(tm // MIX_SUB)]
    lhs = []
    for r in subs:
        na = jnp.concatenate([na_ref[h, r, :] for h in range(nh)], axis=1)
        dl = jnp.concatenate([dl_ref[h, r, :] for h in range(nh)], axis=1)
        lhs.append(jnp.concatenate([_rms(na, gna_ref[...]), _rms(dl, gdl_ref[...])], axis=1).astype(BF16))
    mixed = [jnp.dot(a, wo_ref[...], preferred_element_type=F32) for a in lhs]
    for r, m in zip(subs, mixed):
        x1 = x_ref[r, :] + gt1_ref[...] * _rms(m, gpost_ref[...])
        x1_ref[r, :] = x1
        hf = _rms(x1, gpre_ref[...]) * (1.0 + sc2_ref[...]) + sh2_ref[...]
        hb = hf.astype(BF16)
        lg_ref[r, :] = jnp.dot(hb, wr_ref[...], preferred_element_type=F32) + br_ref[...]
        half = hb.shape[1] // 2
        lo = lax.bitcast_convert_type(hb[:, :half].astype(F32), U32)
        hi = lax.bitcast_convert_type(hb[:, half:].astype(F32), U32)
        hp_ref[r, 0, :] = (hi & U32(0xFFFF0000)) | lax.shift_right_logical(lo, U32(16))


def _mix_out(out_na, out_dil, g_na, g_dil, wo_bf, x2, gt1, sc2, sh2, g_post, g_pre, wr_bf, br, seq):
    t, d = x2.shape
    nh = out_na.shape[0]
    tm = 512
    wna = g_na.shape[-1]
    row = lambda n: pl.BlockSpec((1, n), lambda i: (0, 0))
    per_b = pl.BlockSpec((None, 1, d), lambda i: (i * tm // seq, 0, 0))
    heads = pl.BlockSpec((nh, tm, LANES), lambda i: (0, i, 0))
    return pl.pallas_call(
        _mix_kernel,
        out_shape=(jax.ShapeDtypeStruct((t, d), F32),
                   jax.ShapeDtypeStruct((t, 1, d // 2), U32),
                   jax.ShapeDtypeStruct((t, LANES), F32)),
        grid=(t // tm,),
        in_specs=[heads, heads, row(wna), row(wna),
                  pl.BlockSpec((d, d), lambda i: (0, 0)),
                  pl.BlockSpec((tm, d), lambda i: (i, 0)),
                  per_b, per_b, per_b, row(d), row(d),
                  pl.BlockSpec((d, LANES), lambda i: (0, 0)), row(LANES)],
        out_specs=(pl.BlockSpec((tm, d), lambda i: (i, 0)),
                   pl.BlockSpec((tm, 1, d // 2), lambda i: (i, 0, 0)),
                   pl.BlockSpec((tm, LANES), lambda i: (i, 0))),
        compiler_params=_cparams(("arbitrary",)),
        name="mix_out",
    )(out_na, out_dil, g_na.reshape(1, wna), g_dil.reshape(1, wna), wo_bf, x2, gt1, sc2, sh2,
      g_post.reshape(1, d), g_pre.reshape(1, d), wr_bf, br)


def _route_kernel(lg_ref, idx_ref, rank_ref, gate_ref, cnt_ref, carry):
    step = pl.program_id(0)

    @pl.when(step == 0)
    def _():
        carry[...] = jnp.zeros_like(carry)

    v = lg_ref[...]
    ch = v.shape[0]
    lane = lax.broadcasted_iota(I32, v.shape, 1).astype(F32)
    vals, idxs = [], []
    for _ in range(TOP_K):
        m = jnp.max(v, axis=1, keepdims=True)
        ik = jnp.min(jnp.where(v == m, lane, float(LANES)), axis=1, keepdims=True)
        vals.append(m)
        idxs.append(ik)
        v = jnp.where(lane == ik, -jnp.inf, v)
    es = [jnp.exp(val - vals[0]) for val in vals]
    den = es[0] + es[1] + es[2] + es[3]
    sel = jnp.zeros(v.shape, F32)
    for ik in idxs:
        sel = jnp.where(lane == ik, 1.0, sel)
    ti = lax.broadcasted_iota(I32, (ch, ch), 0)
    tj = lax.broadcasted_iota(I32, (ch, ch), 1)
    lower = jnp.where(tj < ti, 1.0, 0.0).astype(BF16)
    cum = jnp.dot(lower, sel.astype(BF16), preferred_element_type=F32) + carry[0:1, :]
    idx_o = jnp.zeros(v.shape, F32)
    rank_o = jnp.zeros(v.shape, F32)
    gate_o = jnp.zeros(v.shape, F32)
    for k in range(TOP_K):
        rk = jnp.sum(jnp.where(lane == idxs[k], cum, 0.0), axis=1, keepdims=True)
        idx_o = jnp.where(lane == float(k), idxs[k], idx_o)
        rank_o = jnp.where(lane == float(k), rk, rank_o)
        gate_o = jnp.where(lane == float(k), es[k] / den, gate_o)
    idx_ref[...] = idx_o.astype(I32)
    rank_ref[...] = rank_o.astype(I32)
    gate_ref[...] = gate_o
    total = carry[...] + jnp.sum(sel, axis=0, keepdims=True)
    carry[...] = total
    cnt_ref[...] = total.astype(I32)


def _route(logits):
    t = logits.shape[0]
    ch = 512
    blk = pl.BlockSpec((ch, LANES), lambda i: (i, 0))
    return pl.pallas_call(
        _route_kernel,
        out_shape=(jax.ShapeDtypeStruct((t, LANES), I32),
                   jax.ShapeDtypeStruct((t, LANES), I32),
                   jax.ShapeDtypeStruct((t, LANES), F32),
                   jax.ShapeDtypeStruct((8, LANES), I32)),
        grid=(t // ch,),
        in_specs=[blk],
        out_specs=(blk, blk, blk, pl.BlockSpec((8, LANES), lambda i: (0, 0))),
        scratch_shapes=[pltpu.VMEM((8, LANES), F32)],
        compiler_params=_cparams(("arbitrary",)),
        name="route",
    )(logits)


def _schedule(counts, n_assign):
    max_sb = n_assign // SUPER_ROWS + N_EXPERTS
    nsb_e = (counts + SUPER_ROWS - 1) // SUPER_ROWS
    sb_end = jnp.cumsum(nsb_e)
    sb_start = sb_end - nsb_e
    row_start = (sb_start * SUPER_ROWS).astype(I32)
    nsb = sb_end[-1].astype(I32)
    s = jnp.arange(max_sb, dtype=I32)
    sb_e = jnp.minimum(jnp.searchsorted(sb_end, s, side="right"), N_EXPERTS - 1).astype(I32)
    rem = counts[sb_e] - (s - sb_start[sb_e]) * SUPER_ROWS
    sb_nb = jnp.where(s < nsb, (jnp.clip(rem, 0, SUPER_ROWS) + ROW_BLK - 1) // ROW_BLK, 0).astype(I32)
    zero_start = (s * SUPER_ROWS + jnp.maximum(sb_nb - 1, 0) * ROW_BLK).astype(I32)
    return max_sb, row_start, nsb.reshape(1), sb_e, sb_nb, zero_start


ROW_UNROLL = 8


def _dispatch_kernel(dest_s, zs_s, nsb_s, hp_ref, xs_hbm, zbuf, zsem, sem):
    step = pl.program_id(0)
    tt = hp_ref.shape[0]

    def zero_copy(s):
        return pltpu.make_async_copy(zbuf, xs_hbm.at[pl.ds(pl.multiple_of(zs_s[s], ROW_BLK), ROW_BLK)], zsem)

    @pl.when(step == 0)
    def _():
        zbuf[...] = jnp.zeros_like(zbuf)

        def zstart(s, c):
            zero_copy(s).start()
            return c

        def zwait(s, c):
            zero_copy(s).wait()
            return c

        lax.fori_loop(0, nsb_s[0], zstart, 0)
        lax.fori_loop(0, nsb_s[0], zwait, 0)

    base = step * tt * TOP_K

    def issue(jj, c):
        for u in range(ROW_UNROLL):
            j = jj * ROW_UNROLL + u
            for k in range(TOP_K):
                d = dest_s[base + j * TOP_K + k]
                pltpu.make_async_copy(hp_ref.at[pl.ds(j, 1)], xs_hbm.at[pl.ds(d, 1)], sem).start(
                    priority=k % 2)
        return c

    lax.fori_loop(0, tt // ROW_UNROLL, issue, 0)
    for k in range(TOP_K):
        pltpu.make_async_copy(hp_ref, xs_hbm.at[pl.ds(0, tt)], sem).wait()


def _dispatch(dest, zero_start, nsb, hp, n_rows):
    t, _, w = hp.shape
    tt = 256
    return pl.pallas_call(
        _dispatch_kernel,
        out_shape=jax.ShapeDtypeStruct((n_rows, 1, w), U32),
        grid_spec=pltpu.PrefetchScalarGridSpec(
            num_scalar_prefetch=3,
            grid=(t // tt,),
            in_specs=[pl.BlockSpec((tt, 1, w), lambda i, *_: (i, 0, 0))],
            out_specs=pl.BlockSpec(memory_space=pl.ANY),
            scratch_shapes=[pltpu.VMEM((ROW_BLK, 1, w), U32),
                            pltpu.SemaphoreType.DMA(()), pltpu.SemaphoreType.DMA(())]),
        compiler_params=_cparams(("arbitrary",)),
        name="dispatch",
    )(dest, zero_start, nsb, hp)


def _row_block(r):
    return pl.ds(pl.multiple_of(r * ROW_BLK, ROW_BLK), ROW_BLK)


def _moe_up_kernel(sbe_s, sbn_s, x_ref, win_ref, bin_ref, h_ref, winb, zbuf, xstage):
    nb = sbn_s[pl.program_id(0)]
    half = x_ref.shape[2]
    nsub = win_ref.shape[1] // (2 * LANES)
    lane = lax.broadcasted_iota(I32, (ROW_BLK, LANES), 1)
    even = (2 * lane) & (LANES - 1)
    odd = even + 1
    first_half = lane < LANES // 2

    def unpack(r):
        xstage[...] = x_ref[_row_block(r), 0, :]
        xu = xstage[...]
        xa = lax.bitcast_convert_type(lax.shift_left(xu, U32(16)), F32).astype(BF16)
        xb = lax.bitcast_convert_type(xu & U32(0xFFFF0000), F32).astype(BF16)
        return xa, xb

    def matmul(r):
        xa, xb = unpack(r)
        zbuf[...] = (jnp.dot(xa, winb[0:half, :], preferred_element_type=F32)
                     + jnp.dot(xb, winb[half:2 * half, :], preferred_element_type=F32) + bin_ref[...])

    def matmul_casting(r):
        xa, xb = unpack(r)
        kq = half // 2
        acc = bin_ref[...]
        for ks in range(4):
            wb = win_ref[ks * kq:(ks + 1) * kq, :].astype(BF16)
            winb[ks * kq:(ks + 1) * kq, :] = wb
            xpart = (xa, xb)[ks // 2][:, (ks % 2) * kq:(ks % 2 + 1) * kq]
            acc = acc + jnp.dot(xpart, wb, preferred_element_type=F32)
        zbuf[...] = acc

    def activation():
        hs = []
        for u in range(nsub):
            za = zbuf[:, (2 * u) * LANES:(2 * u + 1) * LANES]
            zb = zbuf[:, (2 * u + 1) * LANES:(2 * u + 2) * LANES]
            gate = jnp.where(first_half, jnp.take_along_axis(za, even, axis=1),
                             jnp.take_along_axis(zb, even, axis=1))
            up = jnp.where(first_half, jnp.take_along_axis(za, odd, axis=1),
                           jnp.take_along_axis(zb, odd, axis=1))
            gate = jnp.minimum(gate, SWIGLU_LIMIT)
            up = jnp.clip(up, -SWIGLU_LIMIT, SWIGLU_LIMIT)
            glu = gate * jax.nn.sigmoid(SWIGLU_ALPHA * gate)
            hs.append(((up + 1.0) * glu).astype(BF16))
        return jnp.concatenate(hs, axis=1)

    matmul_casting(0)

    def step(r, carry):
        h = activation()
        matmul(r + 1)
        h_ref[_row_block(r), :] = h
        return carry

    lax.fori_loop(0, nb - 1, step, 0)
    h_ref[_row_block(nb - 1), :] = activation()


def _moe_down_kernel(sbe_s, sbn_s, nsb_s, h_ref, wout_ref, bout_ref, ys_hbm, woutb, obuf, pend, sem):
    s = pl.program_id(0)
    n = pl.program_id(1)
    nn = pl.num_programs(1)
    nb = sbn_s[s]
    chunk = obuf.shape[3]
    par = (s * nn + n) % 2

    def out_copy(r, par_):
        rows = pl.ds(pl.multiple_of(s * SUPER_ROWS + r * ROW_BLK, ROW_BLK), ROW_BLK)
        cols = pl.ds(pl.multiple_of(n * chunk, chunk), chunk)
        return pltpu.make_async_copy(obuf.at[par_, _row_block(r)], ys_hbm.at[rows, :, cols], sem.at[par_])

    def drain(par_):
        def one(i, carry):
            out_copy(0, par_).wait()
            return carry

        lax.fori_loop(0, pend[par_], one, 0)

    @pl.when((s == 0) & (n == 0))
    def _():
        pend[0] = 0
        pend[1] = 0

    drain(par)

    kq = h_ref.shape[1] // 4
    h0 = h_ref[_row_block(0), :]
    acc = bout_ref[...]
    for ks in range(4):
        wb = wout_ref[ks * kq:(ks + 1) * kq, :].astype(BF16)
        woutb[ks * kq:(ks + 1) * kq, :] = wb
        acc = acc + jnp.dot(h0[:, ks * kq:(ks + 1) * kq], wb, preferred_element_type=F32)
    obuf[par, _row_block(0), 0, :] = acc

    def project(r):
        return jnp.dot(h_ref[_row_block(r), :], woutb[...], preferred_element_type=F32) + bout_ref[...]

    def pair(p, carry):
        r = 1 + 2 * p
        y0, y1 = project(r), project(r + 1)
        obuf[par, _row_block(r), 0, :] = y0
        obuf[par, _row_block(r + 1), 0, :] = y1
        return carry

    lax.fori_loop(0, (nb - 1) // 2, pair, 0)

    @pl.when((nb - 1) % 2 == 1)
    def _():
        obuf[par, _row_block(nb - 1), 0, :] = project(nb - 1)

    def start(r, carry):
        out_copy(r, par).start()
        return carry

    lax.fori_loop(0, nb, start, 0)
    pend[par] = nb

    @pl.when((s == nsb_s[0] - 1) & (n == nn - 1))
    def _():
        drain(0)
        drain(1)


def _moe_ffn(sb_e, sb_nb, nsb, xs, w_in, b_in, w_out, b_out, max_sb):
    ne, d, f2 = w_in.shape
    ff = w_out.shape[1]
    half = xs.shape[2]
    rows = max_sb * SUPER_ROWS
    sem = ("arbitrary", "arbitrary")

    h = pl.pallas_call(
        _moe_up_kernel,
        out_shape=jax.ShapeDtypeStruct((rows, ff), BF16),
        grid_spec=pltpu.PrefetchScalarGridSpec(
            num_scalar_prefetch=2,
            grid=(nsb[0], ff // UP_CHUNK),
            in_specs=[pl.BlockSpec((SUPER_ROWS, 1, half), lambda s, j, sbe, sbn: (s, 0, 0)),
                      pl.BlockSpec((None, d, 2 * UP_CHUNK), lambda s, j, sbe, sbn: (sbe[s], 0, j)),
                      pl.BlockSpec((None, 1, 2 * UP_CHUNK), lambda s, j, sbe, sbn: (sbe[s], 0, j))],
            out_specs=pl.BlockSpec((SUPER_ROWS, UP_CHUNK), lambda s, j, sbe, sbn: (s, j)),
            scratch_shapes=[pltpu.VMEM((d, 2 * UP_CHUNK), BF16), pltpu.VMEM((ROW_BLK, 2 * UP_CHUNK), F32),
                            pltpu.VMEM((ROW_BLK, half), U32)]),
        compiler_params=_cparams(sem),
        name="moe_up",
    )(sb_e, sb_nb, xs, w_in, b_in.reshape(ne, 1, f2))

    return pl.pallas_call(
        _moe_down_kernel,
        out_shape=jax.ShapeDtypeStruct((rows, 1, d), F32),
        grid_spec=pltpu.PrefetchScalarGridSpec(
            num_scalar_prefetch=3,
            grid=(nsb[0], d // DOWN_CHUNK),
            in_specs=[pl.BlockSpec((SUPER_ROWS, ff), lambda s, n, sbe, sbn, nsb_: (s, 0)),
                      pl.BlockSpec((None, ff, DOWN_CHUNK), lambda s, n, sbe, sbn, nsb_: (sbe[s], 0, n)),
                      pl.BlockSpec((None, 1, DOWN_CHUNK), lambda s, n, sbe, sbn, nsb_: (sbe[s], 0, n))],
            out_specs=pl.BlockSpec(memory_space=pl.ANY),
            scratch_shapes=[pltpu.VMEM((ff, DOWN_CHUNK), BF16),
                            pltpu.VMEM((2, SUPER_ROWS, 1, DOWN_CHUNK), F32),
                            pltpu.SMEM((2,), I32), pltpu.SemaphoreType.DMA((2,))]),
        compiler_params=_cparams(sem),
        name="moe_down",
    )(sb_e, sb_nb, nsb, h, w_out, b_out.reshape(ne, 1, d))


COMBINE_GROUP = 32


def _combine_kernel(dest_s, gate_ref, x1_ref, gt2_ref, g_ref, ys_hbm, o_ref, buf, stage, sem):
    step = pl.program_id(0)
    nstep = pl.num_programs(0)
    tt = x1_ref.shape[0]
    slot = step % 2

    def issue(tile, slot_, jj):
        base = tile * tt * TOP_K
        for u in range(COMBINE_GROUP):
            j = jj * COMBINE_GROUP + u
            for k in range(TOP_K):
                d = dest_s[base + j * TOP_K + k]
                pltpu.make_async_copy(ys_hbm.at[pl.ds(d, 1)], buf.at[slot_, k, pl.ds(j, 1)],
                                      sem.at[slot_]).start(priority=k % 2)

    def finish(jj):
        rows = pl.ds(pl.multiple_of(jj * COMBINE_GROUP, COMBINE_GROUP), COMBINE_GROUP)
        g = gate_ref[rows, :]
        y = None
        for k in range(TOP_K):
            stage[k] = buf[slot, k, rows, 0, :]
            term = stage[k] * g[:, k:k + 1]
            y = term if y is None else y + term
        o_ref[rows, :] = x1_ref[rows, :] + gt2_ref[...] * _rms(y, g_ref[...])

    @pl.when(step == 0)
    def _():
        def first(jj, c):
            issue(0, 0, jj)
            return c

        lax.fori_loop(0, tt // COMBINE_GROUP, first, 0)

    for k in range(TOP_K):
        pltpu.make_async_copy(ys_hbm.at[pl.ds(0, tt)], buf.at[slot, k], sem.at[slot]).wait()

    @pl.when(step + 1 < nstep)
    def _():
        def both(jj, c):
            finish(jj)
            issue(step + 1, 1 - slot, jj)
            return c

        lax.fori_loop(0, tt // COMBINE_GROUP, both, 0)

    @pl.when(step + 1 == nstep)
    def _():
        def last(jj, c):
            finish(jj)
            return c

        lax.fori_loop(0, tt // COMBINE_GROUP, last, 0)


def _combine(dest, gates, x1, gt2, g_post, ys, seq):
    t, d = x1.shape
    tt = 128
    return pl.pallas_call(
        _combine_kernel,
        out_shape=jax.ShapeDtypeStruct((t, d), F32),
        grid_spec=pltpu.PrefetchScalarGridSpec(
            num_scalar_prefetch=1,
            grid=(t // tt,),
            in_specs=[pl.BlockSpec((tt, LANES), lambda i, *_: (i, 0)),
                      pl.BlockSpec((tt, d), lambda i, *_: (i, 0)),
                      pl.BlockSpec((None, 1, d), lambda i, *_: (i * tt // seq, 0, 0)),
                      pl.BlockSpec((1, d), lambda i, *_: (0, 0)),
                      pl.BlockSpec(memory_space=pl.ANY)],
            out_specs=pl.BlockSpec((tt, d), lambda i, *_: (i, 0)),
            scratch_shapes=[pltpu.VMEM((2, TOP_K, tt, 1, d), F32), pltpu.VMEM((TOP_K, COMBINE_GROUP, d), F32),
                            pltpu.SemaphoreType.DMA((2,))]),
        compiler_params=_cparams(("arbitrary",)),
        name="combine",
    )(dest, gates, x1, gt2, g_post.reshape(1, d), ys)


def kernel(x, c, w_ada, b_ada, g_pre_mix, g_post_mix, w_in, rpb_na, t5_table, g_out_na, g_out_dil, w_o,
           g_pre_ffn, g_post_ffn, w_router, b_router, w_e_in, b_e_in, w_e_out, b_e_out):
    bsz, seq, d = x.shape
    t = bsz * seq
    for l in range(w_ada.shape[0]):
        mod = _ada_mod(c, w_ada[l], b_ada[l])
        sh1, sc1, gt1, sh2, sc2, gt2 = (m.reshape(bsz, 1, d) for m in jnp.split(mod, 6, axis=-1))
        x2 = x.reshape(t, d)

        proj = _qkv_proj(x2, g_pre_mix[l], sc1, sh1, w_in[l].astype(BF16), seq)
        out_na = _na_attn(proj, _na_bias_table(rpb_na[l]), bsz, seq)
        out_dil = _dil_attn(proj, t5_table, bsz, seq)

        ne = w_router.shape[-1]
        wr = jnp.zeros((d, LANES), BF16).at[:, :ne].set(w_router[l].astype(BF16))
        br = jnp.full((1, LANES), NEG, F32).at[0, :ne].set(b_router[l])
        x1, hp, logits = _mix_out(out_na, out_dil, g_out_na[l], g_out_dil[l], w_o[l].astype(BF16), x2,
                                  gt1, sc2, sh2, g_post_mix[l], g_pre_ffn[l], wr, br, seq)

        idx, rank, gates, cnt = _route(logits)
        max_sb, row_start, nsb, sb_e, sb_nb, zero_start = _schedule(cnt[0, :ne], t * TOP_K)
        eidx = idx[:, :TOP_K].reshape(-1)
        onehot = eidx[:, None] == jnp.arange(ne, dtype=I32)[None, :]
        dest = rank[:, :TOP_K].reshape(-1) + jnp.sum(jnp.where(onehot, row_start[None, :], 0), axis=1)
        xs = _dispatch(dest, zero_start, nsb, hp, max_sb * SUPER_ROWS)
        ys = _moe_ffn(sb_e, sb_nb, nsb, xs, w_e_in[l], b_e_in[l], w_e_out[l], b_e_out[l], max_sb)
        x = _combine(dest, gates, x1, gt2, g_post_ffn[l], ys, seq).reshape(bsz, seq, d)
    return x
```

```python
import functools

import numpy as np
import jax
import jax.numpy as jnp
from jax import lax
from jax.experimental import pallas as pl
from jax.experimental.pallas import tpu as pltpu

F32 = jnp.float32
BF16 = jnp.bfloat16
U32 = jnp.uint32
I32 = jnp.int32

HEAD_DIM = 128
N_HEADS_NA = 8
N_HEADS_DIL = 8
GRID_W = 64
NA_ROWS = 8
NA_COLS = 16
DIL_PATTERNS = ((128, 1), (512, 4), (2048, 16))
T5_BUCKETS = 32
T5_MAX_DIST = 1024
N_EXPERTS = 32
TOP_K = 4
SWIGLU_LIMIT = 7.0
SWIGLU_ALPHA = 1.702
EPS = 1e-6
NEG = -1e30
SCALE = HEAD_DIM ** -0.5

LANES = 128
QBLK = 128
ROW_BLK = 256
SUPER_BLKS = 7
SUPER_ROWS = ROW_BLK * SUPER_BLKS
UP_CHUNK = 512
DOWN_CHUNK = 1024
VMEM_LIMIT = 56 * 1024 * 1024


def _cparams(sem, vmem=VMEM_LIMIT):
    return pltpu.CompilerParams(dimension_semantics=sem, vmem_limit_bytes=vmem)


def _rms(x, g):
    return x * lax.rsqrt(jnp.mean(x * x, axis=-1, keepdims=True) + EPS) * g


def _ada_kernel(c_ref, w_ref, b_ref, o_ref):
    c = c_ref[...]
    s = c * jax.nn.sigmoid(c)
    o_ref[...] = jnp.dot(s.astype(BF16), w_ref[...].astype(BF16),
                         preferred_element_type=F32) + b_ref[...]


def _ada_mod(c, w, b):
    bsz, d = c.shape
    n = w.shape[1]
    tn = 1024
    cp = jnp.zeros((8, d), F32).at[:bsz].set(c)
    out = pl.pallas_call(
        _ada_kernel,
        out_shape=jax.ShapeDtypeStruct((8, n), F32),
        grid=(n // tn,),
        in_specs=[pl.BlockSpec((8, d), lambda j: (0, 0)),
                  pl.BlockSpec((d, tn), lambda j: (0, j)),
                  pl.BlockSpec((1, tn), lambda j: (0, j))],
        out_specs=pl.BlockSpec((8, tn), lambda j: (0, j)),
        compiler_params=_cparams(("arbitrary",)),
        name="ada_mod",
    )(cp, w, b.reshape(1, n))
    return out[:bsz]


QKV_SUB = 256


def _qkv_kernel(x_ref, g_ref, sc_ref, sh_ref, w_ref, o_ref, h_scr):
    def emit(acc, rows):
        for u in range(o_ref.shape[0]):
            o_ref[u, rows, :] = acc[:, u * LANES:(u + 1) * LANES].astype(BF16)

    @pl.when(pl.program_id(1) == 0)
    def _():
        subs = [slice(i * QKV_SUB, (i + 1) * QKV_SUB) for i in range(x_ref.shape[0] // QKV_SUB)]
        hs = []
        for r in subs:
            h = (_rms(x_ref[r, :], g_ref[...]) * (1.0 + sc_ref[...]) + sh_ref[...]).astype(BF16)
            h_scr[r, :] = h
            hs.append(h)
        accs = [jnp.dot(h, w_ref[...], preferred_element_type=F32) for h in hs]
        for r, acc in zip(subs, accs):
            emit(acc, r)

    @pl.when(pl.program_id(1) > 0)
    def _():
        emit(jnp.dot(h_scr[...], w_ref[...], preferred_element_type=F32), slice(None))


def _qkv_proj(x2, g, sc, sh, w_bf, seq):
    t, d = x2.shape
    n = w_bf.shape[1]
    tm, tn = 1024, 1024
    return pl.pallas_call(
        _qkv_kernel,
        out_shape=jax.ShapeDtypeStruct((n // LANES, t, LANES), BF16),
        grid=(t // tm, n // tn),
        in_specs=[pl.BlockSpec((tm, d), lambda i, j: (i, 0)),
                  pl.BlockSpec((1, d), lambda i, j: (0, 0)),
                  pl.BlockSpec((None, 1, d), lambda i, j: (i * tm // seq, 0, 0)),
                  pl.BlockSpec((None, 1, d), lambda i, j: (i * tm // seq, 0, 0)),
                  pl.BlockSpec((d, tn), lambda i, j: (0, j))],
        out_specs=pl.BlockSpec((tn // LANES, tm, LANES), lambda i, j: (j, i, 0)),
        scratch_shapes=[pltpu.VMEM((tm, d), BF16)],
        compiler_params=_cparams(("arbitrary", "arbitrary")),
        name="qkv_proj",
    )(x2, g.reshape(1, d), sc, sh, w_bf)


def _toeplitz(vec, rows, cols):
    n = rows + cols - 1
    assert vec.shape[-1] == n
    lead = vec.shape[:-1]
    ext = jnp.concatenate([vec, jnp.zeros(lead + (1,), vec.dtype)], axis=-1)
    flat = jnp.broadcast_to(ext[..., None, :], lead + (rows, n + 1)).reshape(lead + (rows * (n + 1),))
    skew = flat[..., :rows * n].reshape(lead + (rows, n))
    return skew[..., rows - 1:rows - 1 + cols]


def _na_bias_table(rpb):
    cidx = np.arange(GRID_W)
    col_start = np.clip(cidx - NA_COLS // 2, 0, GRID_W - NA_COLS)
    col_ok = (cidx[None, :] >= col_start[:, None]) & (cidx[None, :] < col_start[:, None] + NA_COLS)
    pad = GRID_W - NA_COLS
    vec = jnp.pad(rpb.astype(F32), ((0, 0), (0, 0), (pad, pad)))
    return jnp.where(col_ok, _toeplitz(vec, GRID_W, GRID_W), NEG)


def _attn_group(qs, ks, vs, biases):
    ss = [lax.dot_general(q, k, (((1,), (1,)), ((), ())), preferred_element_type=F32) * SCALE + b
          for q, k, b in zip(qs, ks, biases)]
    ms = [jnp.max(s, axis=-1, keepdims=True) for s in ss]
    ps = [jnp.exp(s - m) for s, m in zip(ss, ms)]
    ls = [jnp.sum(p, axis=-1, keepdims=True) for p in ps]
    os = [jnp.dot(p.astype(BF16), v, preferred_element_type=F32) / l for p, v, l in zip(ps, vs, ls)]
    return os, [m + jnp.log(l) for m, l in zip(ms, ls)]


NA_GROUP = 8


def _na_kernel(q_ref, k_ref, v_ref, tab_ref, o_ref, bias_ref, *, rows):
    nkeys = NA_ROWS * GRID_W

    @pl.when(pl.program_id(1) == 0)
    def _():
        for var in range(NA_ROWS):
            bias_ref[var] = jnp.concatenate([tab_ref[var + kr] for kr in range(NA_ROWS)], axis=1)

    def body(g, carry):
        qs, ks, vs, bs, q0s = [], [], [], [], []
        for i in range(NA_GROUP):
            r = g * NA_GROUP + i
            rs = jnp.clip(r - NA_ROWS // 2, 0, rows - NA_ROWS)
            q0 = pl.multiple_of(r * GRID_W, GRID_W)
            k0 = pl.multiple_of(rs * GRID_W, GRID_W)
            q0s.append(q0)
            qs.append(q_ref[pl.ds(q0, GRID_W), :])
            ks.append(k_ref[pl.ds(k0, nkeys), :])
            vs.append(v_ref[pl.ds(k0, nkeys), :])
            bs.append(bias_ref[rs - r + (NA_ROWS - 1)])
        os, _ = _attn_group(qs, ks, vs, bs)
        for q0, o in zip(q0s, os):
            o_ref[pl.ds(q0, GRID_W), :] = o
        return carry

    lax.fori_loop(0, rows // NA_GROUP, body, 0)


def _na_attn(proj, bias, bsz, seq):
    nh = N_HEADS_NA
    t = bsz * seq
    blk = lambda off: pl.BlockSpec((None, seq, LANES), lambda h, b: (h + off, b, 0))
    return pl.pallas_call(
        functools.partial(_na_kernel, rows=seq // GRID_W),
        out_shape=jax.ShapeDtypeStruct((nh, t, LANES), F32),
        grid=(nh, bsz),
        in_specs=[blk(0), blk(nh), blk(2 * nh),
                  pl.BlockSpec((None, 2 * NA_ROWS - 1, GRID_W, GRID_W), lambda h, b: (h, 0, 0, 0))],
        out_specs=pl.BlockSpec((None, seq, LANES), lambda h, b: (h, b, 0)),
        scratch_shapes=[pltpu.VMEM((NA_ROWS, GRID_W, NA_ROWS * GRID_W), F32)],
        compiler_params=_cparams(("arbitrary", "arbitrary")),
        name="na_attn",
    )(proj, proj, proj, bias)


def _t5_bucket(rel):
    nb = T5_BUCKETS // 2
    max_exact = nb // 2
    ret = (rel > 0).astype(np.int32) * nb
    n = np.abs(rel)
    large = max_exact + (np.log(np.maximum(n, 1) / max_exact) / np.log(T5_MAX_DIST / max_exact)
                         * (nb - max_exact)).astype(np.int32)
    large = np.minimum(large, nb - 1)
    return (ret + np.where(n < max_exact, n, large)).astype(np.int32)


def _dil_geometry(sub_len):
    half = DIL_PATTERNS[0][0] // 2
    width = min(sub_len, QBLK + 2 * half)
    nblk = sub_len // QBLK
    starts = [min(max(QBLK * n - half, 0), sub_len - width) for n in range(nblk)]
    offs = sorted({ws - QBLK * n for n, ws in enumerate(starts)}, reverse=True)
    var = [offs.index(ws - QBLK * n) for n, ws in enumerate(starts)]
    return width, starts, offs, var


def _dil_bias_vecs(t5_table, seq):
    half = DIL_PATTERNS[0][0] // 2
    deltas, dils, counts, lens = [], [], [], []
    for _, dil in DIL_PATTERNS:
        width, _, offs, _ = _dil_geometry(seq // dil)
        counts.append(len(offs))
        lens.append(QBLK + width)
        for off in offs:
            d = np.full(2 * QBLK + 2 * half, 4 * half, np.int64)
            d[:QBLK + width - 1] = np.arange(QBLK + width - 1) - (QBLK - 1) + off
            deltas.append(d)
            dils.append(dil)
    delta = np.stack(deltas)
    onehot = np.eye(T5_BUCKETS, dtype=np.float32)[_t5_bucket(delta * np.asarray(dils)[:, None])]
    vals = jnp.einsum("vkb,bh->hvk", jnp.asarray(onehot), t5_table.astype(F32),
                      precision=lax.Precision.HIGHEST)
    vals = jnp.where(np.abs(delta) <= half, vals, NEG)
    out, v0 = [], 0
    for n, ln in zip(counts, lens):
        out.append(vals[:, v0:v0 + n, :ln])
        v0 += n
    return out


DIL_GROUP = 8


def _dil_kernel(q_ref, k_ref, v_ref, v1_ref, v4_ref, v16_ref, o_ref, qf, kf, vf, o4, l4, o16, l16,
                b1_ref, b4_ref, b16_ref, *, seq):
    @pl.when(pl.program_id(1) == 0)
    def _():
        for vec_ref, tab_ref in ((v1_ref, b1_ref), (v4_ref, b4_ref), (v16_ref, b16_ref)):
            n = vec_ref.shape[1]
            for v in range(vec_ref.shape[0]):
                full = jnp.broadcast_to(vec_ref[v:v + 1, :], (QBLK, n))
                tab_ref[v] = pltpu.roll(full, n - (QBLK - 1), 1, stride=1, stride_axis=0)[:, :n - QBLK]

    qf[...] = q_ref[...].astype(F32)
    kf[...] = k_ref[...].astype(F32)
    vf[...] = v_ref[...].astype(F32)

    for dil, bias_ref, o_s, l_s in ((DIL_PATTERNS[2][1], b16_ref, o16, l16), (DIL_PATTERNS[1][1], b4_ref, o4, l4)):
        width, starts, _, var = _dil_geometry(seq // dil)
        blocks = [(rho, n, ws) for rho in range(dil) for n, ws in enumerate(starts)]
        for g in range(0, len(blocks), DIL_GROUP):
            grp = blocks[g:g + DIL_GROUP]
            qrows = [pl.ds(rho + dil * QBLK * n, QBLK, stride=dil) for rho, n, _ in grp]
            krows = [pl.ds(rho + dil * ws, width, stride=dil) for rho, _, ws in grp]
            os, lses = _attn_group([qf[r, :].astype(BF16) for r in qrows],
                                   [kf[r, :].astype(BF16) for r in krows],
                                   [vf[r, :].astype(BF16) for r in krows],
                                   [bias_ref[var[n]] for _, n, _ in grp])
            for r, o, lse in zip(qrows, os, lses):
                o_s[r, :] = o
                l_s[r, :] = jnp.broadcast_to(lse, (QBLK, LANES))

    width, starts, _, _ = _dil_geometry(seq)
    nblk = len(starts)
    half = DIL_PATTERNS[0][0] // 2

    def body(g, carry):
        rows, krows, bs = [], [], []
        for i in range(DIL_GROUP):
            n = g * DIL_GROUP + i
            ws = pl.multiple_of(jnp.clip(n * QBLK - half, 0, seq - width), half)
            rows.append(pl.ds(pl.multiple_of(n * QBLK, QBLK), QBLK))
            krows.append(pl.ds(ws, width))
            bs.append(b1_ref[jnp.where(n == 0, 0, jnp.where(n == nblk - 1, 2, 1))])
        os, lses = _attn_group([q_ref[r, :] for r in rows], [k_ref[r, :] for r in krows],
                               [v_ref[r, :] for r in krows], bs)
        for r, o1, lse1 in zip(rows, os, lses):
            lse4, lse16 = l4[r, :], l16[r, :]
            mx = jnp.maximum(jnp.maximum(lse4, lse16), lse1)
            e1 = jnp.exp(lse1 - mx)
            e4 = jnp.exp(lse4 - mx)
            e16 = jnp.exp(lse16 - mx)
            o_ref[r, :] = (e1 * o1 + e4 * o4[r, :] + e16 * o16[r, :]) / (e1 + e4 + e16)
        return carry

    lax.fori_loop(0, nblk // DIL_GROUP, body, 0)


def _dil_attn(proj, t5_table, bsz, seq):
    nh = N_HEADS_DIL
    vecs = _dil_bias_vecs(t5_table, seq)
    width, _, offs, var = _dil_geometry(seq)
    assert offs == [0, -(DIL_PATTERNS[0][0] // 2), -DIL_PATTERNS[0][0]] and var[0] == 0 and var[-1] == 2
    first = 3 * N_HEADS_NA
    blk = lambda off: pl.BlockSpec((None, seq, LANES), lambda h, b: (first + off + h, b, 0))
    vec = lambda t: pl.BlockSpec((None,) + t.shape[1:], lambda h, b: (h, 0, 0))
    return pl.pallas_call(
        functools.partial(_dil_kernel, seq=seq),
        out_shape=jax.ShapeDtypeStruct((nh, bsz * seq, LANES), F32),
        grid=(nh, bsz),
        in_specs=[blk(0), blk(nh), blk(2 * nh)] + [vec(t) for t in vecs],
        out_specs=pl.BlockSpec((None, seq, LANES), lambda h, b: (h, b, 0)),
        scratch_shapes=[pltpu.VMEM((seq, LANES), F32)] * 7
        + [pltpu.VMEM((t.shape[1], QBLK, t.shape[2] - QBLK), F32) for t in vecs],
        compiler_params=_cparams(("arbitrary", "arbitrary")),
        name="dil_attn",
    )(proj, proj, proj, *vecs)


MIX_SUB = 256


def _mix_kernel(na_ref, dl_ref, gna_ref, gdl_ref, wo_ref, x_ref, gt1_ref, sc2_ref, sh2_ref,
                gpost_ref, gpre_ref, wr_ref, br_ref, x1_ref, hp_ref, lg_ref):
    nh = na_ref.shape[0]
    tm = x_ref.shape[0]
    subs = [slice(i * MIX_SUB, (i + 1) * MIX_SUB) for i in range(tm // MIX_SUB)]
    lhs = []
    for r in subs:
        na = jnp.concatenate([na_ref[h, r, :] for h in range(nh)], axis=1)
        dl = jnp.concatenate([dl_ref[h, r, :] for h in range(nh)], axis=1)
        lhs.append(jnp.concatenate([_rms(na, gna_ref[...]), _rms(dl, gdl_ref[...])], axis=1).astype(BF16))
    mixed = [jnp.dot(a, wo_ref[...], preferred_element_type=F32) for a in lhs]
    for r, m in zip(subs, mixed):
        x1 = x_ref[r, :] + gt1_ref[...] * _rms(m, gpost_ref[...])
        x1_ref[r, :] = x1
        hf = _rms(x1, gpre_ref[...]) * (1.0 + sc2_ref[...]) + sh2_ref[...]
        hb = hf.astype(BF16)
        lg_ref[r, :] = jnp.dot(hb, wr_ref[...], preferred_element_type=F32) + br_ref[...]
        half = hb.shape[1] // 2
        lo = lax.bitcast_convert_type(hb[:, :half].astype(F32), U32)
        hi = lax.bitcast_convert_type(hb[:, half:].astype(F32), U32)
        hp_ref[r, 0, :] = (hi & U32(0xFFFF0000)) | lax.shift_right_logical(lo, U32(16))


def _mix_out(out_na, out_dil, g_na, g_dil, wo_bf, x2, gt1, sc2, sh2, g_post, g_pre, wr_bf, br, seq):
    t, d = x2.shape
    nh = out_na.shape[0]
    tm = 512
    wna = g_na.shape[-1]
    row = lambda n: pl.BlockSpec((1, n), lambda i: (0, 0))
    per_b = pl.BlockSpec((None, 1, d), lambda i: (i * tm // seq, 0, 0))
    heads = pl.BlockSpec((nh, tm, LANES), lambda i: (0, i, 0))
    return pl.pallas_call(
        _mix_kernel,
        out_shape=(jax.ShapeDtypeStruct((t, d), F32),
                   jax.ShapeDtypeStruct((t, 1, d // 2), U32),
                   jax.ShapeDtypeStruct((t, LANES), F32)),
        grid=(t // tm,),
        in_specs=[heads, heads, row(wna), row(wna),
                  pl.BlockSpec((d, d), lambda i: (0, 0)),
                  pl.BlockSpec((tm, d), lambda i: (i, 0)),
                  per_b, per_b, per_b, row(d), row(d),
                  pl.BlockSpec((d, LANES), lambda i: (0, 0)), row(LANES)],
        out_specs=(pl.BlockSpec((tm, d), lambda i: (i, 0)),
                   pl.BlockSpec((tm, 1, d // 2), lambda i: (i, 0, 0)),
                   pl.BlockSpec((tm, LANES), lambda i: (i, 0))),
        compiler_params=_cparams(("arbitrary",)),
        name="mix_out",
    )(out_na, out_dil, g_na.reshape(1, wna), g_dil.reshape(1, wna), wo_bf, x2, gt1, sc2, sh2,
      g_post.reshape(1, d), g_pre.reshape(1, d), wr_bf, br)


def _route_kernel(lg_ref, idx_ref, rank_ref, gate_ref, cnt_ref, carry):
    step = pl.program_id(0)

    @pl.when(step == 0)
    def _():
        carry[...] = jnp.zeros_like(carry)

    v = lg_ref[...]
    ch = v.shape[0]
    lane = lax.broadcasted_iota(I32, v.shape, 1).astype(F32)
    vals, idxs = [], []
    for _ in range(TOP_K):
        m = jnp.max(v, axis=1, keepdims=True)
        ik = jnp.min(jnp.where(v == m, lane, float(LANES)), axis=1, keepdims=True)
        vals.append(m)
        idxs.append(ik)
        v = jnp.where(lane == ik, -jnp.inf, v)
    es = [jnp.exp(val - vals[0]) for val in vals]
    den = es[0] + es[1] + es[2] + es[3]
    sel = jnp.zeros(v.shape, F32)
    for ik in idxs:
        sel = jnp.where(lane == ik, 1.0, sel)
    ti = lax.broadcasted_iota(I32, (ch, ch), 0)
    tj = lax.broadcasted_iota(I32, (ch, ch), 1)
    lower = jnp.where(tj < ti, 1.0, 0.0).astype(BF16)
    cum = jnp.dot(lower, sel.astype(BF16), preferred_element_type=F32) + carry[0:1, :]
    idx_o = jnp.zeros(v.shape, F32)
    rank_o = jnp.zeros(v.shape, F32)
    gate_o = jnp.zeros(v.shape, F32)
    for k in range(TOP_K):
        rk = jnp.sum(jnp.where(lane == idxs[k], cum, 0.0), axis=1, keepdims=True)
        idx_o = jnp.where(lane == float(k), idxs[k], idx_o)
        rank_o = jnp.where(lane == float(k), rk, rank_o)
        gate_o = jnp.where(lane == float(k), es[k] / den, gate_o)
    idx_ref[...] = idx_o.astype(I32)
    rank_ref[...] = rank_o.astype(I32)
    gate_ref[...] = gate_o
    total = carry[...] + jnp.sum(sel, axis=0, keepdims=True)
    carry[...] = total
    cnt_ref[...] = total.astype(I32)


def _route(logits):
    t = logits.shape[0]
    ch = 512
    blk = pl.BlockSpec((ch, LANES), lambda i: (i, 0))
    return pl.pallas_call(
        _route_kernel,
        out_shape=(jax.ShapeDtypeStruct((t, LANES), I32),
                   jax.ShapeDtypeStruct((t, LANES), I32),
                   jax.ShapeDtypeStruct((t, LANES), F32),
                   jax.ShapeDtypeStruct((8, LANES), I32)),
        grid=(t // ch,),
        in_specs=[blk],
        out_specs=(blk, blk, blk, pl.BlockSpec((8, LANES), lambda i: (0, 0))),
        scratch_shapes=[pltpu.VMEM((8, LANES), F32)],
        compiler_params=_cparams(("arbitrary",)),
        name="route",
    )(logits)


def _schedule(counts, n_assign):
    max_sb = n_assign // SUPER_ROWS + N_EXPERTS
    nsb_e = (counts + SUPER_ROWS - 1) // SUPER_ROWS
    sb_end = jnp.cumsum(nsb_e)
    sb_start = sb_end - nsb_e
    row_start = (sb_start * SUPER_ROWS).astype(I32)
    nsb = sb_end[-1].astype(I32)
    s = jnp.arange(max_sb, dtype=I32)
    sb_e = jnp.minimum(jnp.searchsorted(sb_end, s, side="right"), N_EXPERTS - 1).astype(I32)
    rem = counts[sb_e] - (s - sb_start[sb_e]) * SUPER_ROWS
    sb_nb = jnp.where(s < nsb, (jnp.clip(rem, 0, SUPER_ROWS) + ROW_BLK - 1) // ROW_BLK, 0).astype(I32)
    zero_start = (s * SUPER_ROWS + jnp.maximum(sb_nb - 1, 0) * ROW_BLK).astype(I32)
    return max_sb, row_start, nsb.reshape(1), sb_e, sb_nb, zero_start


ROW_UNROLL = 8


def _dispatch_kernel(dest_s, zs_s, nsb_s, hp_ref, xs_hbm, zbuf, zsem, sem):
    step = pl.program_id(0)
    tt = hp_ref.shape[0]

    def zero_copy(s):
        return pltpu.make_async_copy(zbuf, xs_hbm.at[pl.ds(pl.multiple_of(zs_s[s], ROW_BLK), ROW_BLK)], zsem)

    @pl.when(step == 0)
    def _():
        zbuf[...] = jnp.zeros_like(zbuf)

        def zstart(s, c):
            zero_copy(s).start()
            return c

        def zwait(s, c):
            zero_copy(s).wait()
            return c

        lax.fori_loop(0, nsb_s[0], zstart, 0)
        lax.fori_loop(0, nsb_s[0], zwait, 0)

    base = step * tt * TOP_K

    def issue(jj, c):
        for u in range(ROW_UNROLL):
            j = jj * ROW_UNROLL + u
            for k in range(TOP_K):
                d = dest_s[base + j * TOP_K + k]
                pltpu.make_async_copy(hp_ref.at[pl.ds(j, 1)], xs_hbm.at[pl.ds(d, 1)], sem).start(
                    priority=k % 2)
        return c

    lax.fori_loop(0, tt // ROW_UNROLL, issue, 0)
    for k in range(TOP_K):
        pltpu.make_async_copy(hp_ref, xs_hbm.at[pl.ds(0, tt)], sem).wait()


def _dispatch(dest, zero_start, nsb, hp, n_rows):
    t, _, w = hp.shape
    tt = 256
    return pl.pallas_call(
        _dispatch_kernel,
        out_shape=jax.ShapeDtypeStruct((n_rows, 1, w), U32),
        grid_spec=pltpu.PrefetchScalarGridSpec(
            num_scalar_prefetch=3,
            grid=(t // tt,),
            in_specs=[pl.BlockSpec((tt, 1, w), lambda i, *_: (i, 0, 0))],
            out_specs=pl.BlockSpec(memory_space=pl.ANY),
            scratch_shapes=[pltpu.VMEM((ROW_BLK, 1, w), U32),
                            pltpu.SemaphoreType.DMA(()), pltpu.SemaphoreType.DMA(())]),
        compiler_params=_cparams(("arbitrary",)),
        name="dispatch",
    )(dest, zero_start, nsb, hp)


def _row_block(r):
    return pl.ds(pl.multiple_of(r * ROW_BLK, ROW_BLK), ROW_BLK)


def _moe_up_kernel(sbe_s, sbn_s, x_ref, win_ref, bin_ref, h_ref, winb, zbuf, xstage):
    nb = sbn_s[pl.program_id(0)]
    half = x_ref.shape[2]
    nsub = win_ref.shape[1] // (2 * LANES)
    lane = lax.broadcasted_iota(I32, (ROW_BLK, LANES), 1)
    even = (2 * lane) & (LANES - 1)
    odd = even + 1
    first_half = lane < LANES // 2

    def unpack(r):
        xstage[...] = x_ref[_row_block(r), 0, :]
        xu = xstage[...]
        xa = lax.bitcast_convert_type(lax.shift_left(xu, U32(16)), F32).astype(BF16)
        xb = lax.bitcast_convert_type(xu & U32(0xFFFF0000), F32).astype(BF16)
        return xa, xb

    def matmul(r):
        xa, xb = unpack(r)
        zbuf[...] = (jnp.dot(xa, winb[0:half, :], preferred_element_type=F32)
                     + jnp.dot(xb, winb[half:2 * half, :], preferred_element_type=F32) + bin_ref[...])

    def matmul_casting(r):
        xa, xb = unpack(r)
        kq = half // 2
        acc = bin_ref[...]
        for ks in range(4):
            wb = win_ref[ks * kq:(ks + 1) * kq, :].astype(BF16)
            winb[ks * kq:(ks + 1) * kq, :] = wb
            xpart = (xa, xb)[ks // 2][:, (ks % 2) * kq:(ks % 2 + 1) * kq]
            acc = acc + jnp.dot(xpart, wb, preferred_element_type=F32)
        zbuf[...] = acc

    def activation():
        hs = []
        for u in range(nsub):
            za = zbuf[:, (2 * u) * LANES:(2 * u + 1) * LANES]
            zb = zbuf[:, (2 * u + 1) * LANES:(2 * u + 2) * LANES]
            gate = jnp.where(first_half, jnp.take_along_axis(za, even, axis=1),
                             jnp.take_along_axis(zb, even, axis=1))
            up = jnp.where(first_half, jnp.take_along_axis(za, odd, axis=1),
                           jnp.take_along_axis(zb, odd, axis=1))
            gate = jnp.minimum(gate, SWIGLU_LIMIT)
            up = jnp.clip(up, -SWIGLU_LIMIT, SWIGLU_LIMIT)
            glu = gate * jax.nn.sigmoid(SWIGLU_ALPHA * gate)
            hs.append(((up + 1.0) * glu).astype(BF16))
        return jnp.concatenate(hs, axis=1)

    matmul_casting(0)

    def step(r, carry):
        h = activation()
        matmul(r + 1)
        h_ref[_row_block(r), :] = h
        return carry

    lax.fori_loop(0, nb - 1, step, 0)
    h_ref[_row_block(nb - 1), :] = activation()


def _moe_down_kernel(sbe_s, sbn_s, nsb_s, h_hbm, wout_ref, bout_ref, ys_hbm, woutb, hbuf, obuf, pend,
                     hsem, sem):
    s = pl.program_id(0)
    n = pl.program_id(1)
    nn = pl.num_programs(1)
    nb = sbn_s[s]
    chunk = obuf.shape[3]
    par = (s * nn + n) % 2
    hs = s % 2

    def h_copy(sb, slot, r):
        rows = pl.ds(pl.multiple_of(sb * SUPER_ROWS + r * ROW_BLK, ROW_BLK), ROW_BLK)
        return pltpu.make_async_copy(h_hbm.at[rows], hbuf.at[slot, _row_block(r)], hsem.at[slot])

    def fetch(sb, slot):
        def one(r, carry):
            h_copy(sb, slot, r).start()
            return carry

        lax.fori_loop(0, sbn_s[sb], one, 0)

    @pl.when(n == 0)
    def _():
        @pl.when(s == 0)
        def _():
            fetch(0, 0)

        @pl.when(s + 1 < nsb_s[0])
        def _():
            fetch(s + 1, 1 - hs)

        def one(r, carry):
            h_copy(s, hs, r).wait()
            return carry

        lax.fori_loop(0, nb, one, 0)

    def out_copy(r, par_):
        rows = pl.ds(pl.multiple_of(s * SUPER_ROWS + r * ROW_BLK, ROW_BLK), ROW_BLK)
        cols = pl.ds(pl.multiple_of(n * chunk, chunk), chunk)
        return pltpu.make_async_copy(obuf.at[par_, _row_block(r)], ys_hbm.at[rows, :, cols], sem.at[par_])

    def drain(par_):
        def one(i, carry):
            out_copy(0, par_).wait()
            return carry

        lax.fori_loop(0, pend[par_], one, 0)

    @pl.when((s == 0) & (n == 0))
    def _():
        pend[0] = 0
        pend[1] = 0

    drain(par)

    kq = hbuf.shape[2] // 4
    h0 = hbuf[hs, _row_block(0), :]
    acc = bout_ref[...]
    for ks in range(4):
        wb = wout_ref[ks * kq:(ks + 1) * kq, :].astype(BF16)
        woutb[ks * kq:(ks + 1) * kq, :] = wb
        acc = acc + jnp.dot(h0[:, ks * kq:(ks + 1) * kq], wb, preferred_element_type=F32)
    obuf[par, _row_block(0), 0, :] = acc

    def project(r, nrows):
        rows = pl.ds(pl.multiple_of(r * ROW_BLK, ROW_BLK), nrows)
        obuf[par, rows, 0, :] = jnp.dot(hbuf[hs, rows, :], woutb[...],
                                        preferred_element_type=F32) + bout_ref[...]

    def pair(p, carry):
        project(1 + 2 * p, 2 * ROW_BLK)
        return carry

    lax.fori_loop(0, (nb - 1) // 2, pair, 0)

    @pl.when((nb - 1) % 2 == 1)
    def _():
        project(nb - 1, ROW_BLK)

    def start(r, carry):
        out_copy(r, par).start()
        return carry

    lax.fori_loop(0, nb, start, 0)
    pend[par] = nb

    @pl.when((s == nsb_s[0] - 1) & (n == nn - 1))
    def _():
        drain(0)
        drain(1)


def _moe_ffn(sb_e, sb_nb, nsb, xs, w_in, b_in, w_out, b_out, max_sb):
    ne, d, f2 = w_in.shape
    ff = w_out.shape[1]
    half = xs.shape[2]
    rows = max_sb * SUPER_ROWS
    sem = ("arbitrary", "arbitrary")

    h = pl.pallas_call(
        _moe_up_kernel,
        out_shape=jax.ShapeDtypeStruct((rows, ff), BF16),
        grid_spec=pltpu.PrefetchScalarGridSpec(
            num_scalar_prefetch=2,
            grid=(nsb[0], ff // UP_CHUNK),
            in_specs=[pl.BlockSpec((SUPER_ROWS, 1, half), lambda s, j, sbe, sbn: (s, 0, 0)),
                      pl.BlockSpec((None, d, 2 * UP_CHUNK), lambda s, j, sbe, sbn: (sbe[s], 0, j)),
                      pl.BlockSpec((None, 1, 2 * UP_CHUNK), lambda s, j, sbe, sbn: (sbe[s], 0, j))],
            out_specs=pl.BlockSpec((SUPER_ROWS, UP_CHUNK), lambda s, j, sbe, sbn: (s, j)),
            scratch_shapes=[pltpu.VMEM((d, 2 * UP_CHUNK), BF16), pltpu.VMEM((ROW_BLK, 2 * UP_CHUNK), F32),
                            pltpu.VMEM((ROW_BLK, half), U32)]),
        compiler_params=_cparams(sem),
        name="moe_up",
    )(sb_e, sb_nb, xs, w_in, b_in.reshape(ne, 1, f2))

    return pl.pallas_call(
        _moe_down_kernel,
        out_shape=jax.ShapeDtypeStruct((rows, 1, d), F32),
        grid_spec=pltpu.PrefetchScalarGridSpec(
            num_scalar_prefetch=3,
            grid=(nsb[0], d // DOWN_CHUNK),
            in_specs=[pl.BlockSpec(memory_space=pl.ANY),
                      pl.BlockSpec((None, ff, DOWN_CHUNK), lambda s, n, sbe, sbn, nsb_: (sbe[s], 0, n)),
                      pl.BlockSpec((None, 1, DOWN_CHUNK), lambda s, n, sbe, sbn, nsb_: (sbe[s], 0, n))],
            out_specs=pl.BlockSpec(memory_space=pl.ANY),
            scratch_shapes=[pltpu.VMEM((ff, DOWN_CHUNK), BF16),
                            pltpu.VMEM((2, SUPER_ROWS, ff), BF16),
                            pltpu.VMEM((2, SUPER_ROWS, 1, DOWN_CHUNK), F32),
                            pltpu.SMEM((2,), I32), pltpu.SemaphoreType.DMA((2,)),
                            pltpu.SemaphoreType.DMA((2,))]),
        compiler_params=_cparams(sem),
        name="moe_down",
    )(sb_e, sb_nb, nsb, h, w_out, b_out.reshape(ne, 1, d))


COMBINE_GROUP = 32


def _combine_kernel(dest_s, gate_ref, x1_ref, gt2_ref, g_ref, ys_hbm, o_ref, buf, stage, sem):
    step = pl.program_id(0)
    nstep = pl.num_programs(0)
    tt = x1_ref.shape[0]
    slot = step % 2

    def issue(tile, slot_, jj):
        base = tile * tt * TOP_K
        for u in range(COMBINE_GROUP):
            j = jj * COMBINE_GROUP + u
            for k in range(TOP_K):
                d = dest_s[base + j * TOP_K + k]
                pltpu.make_async_copy(ys_hbm.at[pl.ds(d, 1)], buf.at[slot_, k, pl.ds(j, 1)],
                                      sem.at[slot_]).start(priority=k % 2)

    def finish(jj):
        rows = pl.ds(pl.multiple_of(jj * COMBINE_GROUP, COMBINE_GROUP), COMBINE_GROUP)
        g = gate_ref[rows, :]
        y = None
        for k in range(TOP_K):
            stage[k] = buf[slot, k, rows, 0, :]
            term = stage[k] * g[:, k:k + 1]
            y = term if y is None else y + term
        o_ref[rows, :] = x1_ref[rows, :] + gt2_ref[...] * _rms(y, g_ref[...])

    @pl.when(step == 0)
    def _():
        def first(jj, c):
            issue(0, 0, jj)
            return c

        lax.fori_loop(0, tt // COMBINE_GROUP, first, 0)

    for k in range(TOP_K):
        pltpu.make_async_copy(ys_hbm.at[pl.ds(0, tt)], buf.at[slot, k], sem.at[slot]).wait()

    @pl.when(step + 1 < nstep)
    def _():
        def both(jj, c):
            finish(jj)
            issue(step + 1, 1 - slot, jj)
            return c

        lax.fori_loop(0, tt // COMBINE_GROUP, both, 0)

    @pl.when(step + 1 == nstep)
    def _():
        def last(jj, c):
            finish(jj)
            return c

        lax.fori_loop(0, tt // COMBINE_GROUP, last, 0)


def _combine(dest, gates, x1, gt2, g_post, ys, seq):
    t, d = x1.shape
    tt = 128
    return pl.pallas_call(
        _combine_kernel,
        out_shape=jax.ShapeDtypeStruct((t, d), F32),
        grid_spec=pltpu.PrefetchScalarGridSpec(
            num_scalar_prefetch=1,
            grid=(t // tt,),
            in_specs=[pl.BlockSpec((tt, LANES), lambda i, *_: (i, 0)),
                      pl.BlockSpec((tt, d), lambda i, *_: (i, 0)),
                      pl.BlockSpec((None, 1, d), lambda i, *_: (i * tt // seq, 0, 0)),
                      pl.BlockSpec((1, d), lambda i, *_: (0, 0)),
                      pl.BlockSpec(memory_space=pl.ANY)],
            out_specs=pl.BlockSpec((tt, d), lambda i, *_: (i, 0)),
            scratch_shapes=[pltpu.VMEM((2, TOP_K, tt, 1, d), F32), pltpu.VMEM((TOP_K, COMBINE_GROUP, d), F32),
                            pltpu.SemaphoreType.DMA((2,))]),
        compiler_params=_cparams(("arbitrary",)),
        name="combine",
    )(dest, gates, x1, gt2, g_post.reshape(1, d), ys)


def kernel(x, c, w_ada, b_ada, g_pre_mix, g_post_mix, w_in, rpb_na, t5_table, g_out_na, g_out_dil, w_o,
           g_pre_ffn, g_post_ffn, w_router, b_router, w_e_in, b_e_in, w_e_out, b_e_out):
    bsz, seq, d = x.shape
    t = bsz * seq
    for l in range(w_ada.shape[0]):
        mod = _ada_mod(c, w_ada[l], b_ada[l])
        sh1, sc1, gt1, sh2, sc2, gt2 = (m.reshape(bsz, 1, d) for m in jnp.split(mod, 6, axis=-1))
        x2 = x.reshape(t, d)

        proj = _qkv_proj(x2, g_pre_mix[l], sc1, sh1, w_in[l].astype(BF16), seq)
        out_na = _na_attn(proj, _na_bias_table(rpb_na[l]), bsz, seq)
        out_dil = _dil_attn(proj, t5_table, bsz, seq)

        ne = w_router.shape[-1]
        wr = jnp.zeros((d, LANES), BF16).at[:, :ne].set(w_router[l].astype(BF16))
        br = jnp.full((1, LANES), NEG, F32).at[0, :ne].set(b_router[l])
        x1, hp, logits = _mix_out(out_na, out_dil, g_out_na[l], g_out_dil[l], w_o[l].astype(BF16), x2,
                                  gt1, sc2, sh2, g_post_mix[l], g_pre_ffn[l], wr, br, seq)

        idx, rank, gates, cnt = _route(logits)
        max_sb, row_start, nsb, sb_e, sb_nb, zero_start = _schedule(cnt[0, :ne], t * TOP_K)
        eidx = idx[:, :TOP_K].reshape(-1)
        onehot = eidx[:, None] == jnp.arange(ne, dtype=I32)[None, :]
        dest = rank[:, :TOP_K].reshape(-1) + jnp.sum(jnp.where(onehot, row_start[None, :], 0), axis=1)
        xs = _dispatch(dest, zero_start, nsb, hp, max_sb * SUPER_ROWS)
        ys = _moe_ffn(sb_e, sb_nb, nsb, xs, w_e_in[l], b_e_in[l], w_e_out[l], b_e_out[l], max_sb)
        x = _combine(dest, gates, x1, gt2, g_post_ffn[l], ys, seq).reshape(bsz, seq, d)
    return x
```

```python
import functools

import numpy as np
import jax
import jax.numpy as jnp
from jax import lax
from jax.experimental import pallas as pl
from jax.experimental.pallas import tpu as pltpu

F32 = jnp.float32
BF16 = jnp.bfloat16
U32 = jnp.uint32
I32 = jnp.int32

HEAD_DIM = 128
N_HEADS_NA = 8
N_HEADS_DIL = 8
GRID_W = 64
NA_ROWS = 8
NA_COLS = 16
DIL_PATTERNS = ((128, 1), (512, 4), (2048, 16))
T5_BUCKETS = 32
T5_MAX_DIST = 1024
N_EXPERTS = 32
TOP_K = 4
SWIGLU_LIMIT = 7.0
SWIGLU_ALPHA = 1.702
EPS = 1e-6
NEG = -1e30
SCALE = HEAD_DIM ** -0.5

LANES = 128
QBLK = 128
ROW_BLK = 256
SUPER_BLKS = 7
SUPER_ROWS = ROW_BLK * SUPER_BLKS
UP_CHUNK = 512
DOWN_CHUNK = 1024
VMEM_LIMIT = 56 * 1024 * 1024


def _cparams(sem, vmem=VMEM_LIMIT):
    return pltpu.CompilerParams(dimension_semantics=sem, vmem_limit_bytes=vmem)


def _rms(x, g):
    return x * lax.rsqrt(jnp.mean(x * x, axis=-1, keepdims=True) + EPS) * g


def _ada_kernel(c_ref, w_ref, b_ref, o_ref):
    c = c_ref[...]
    s = c * jax.nn.sigmoid(c)
    o_ref[...] = jnp.dot(s.astype(BF16), w_ref[...].astype(BF16),
                         preferred_element_type=F32) + b_ref[...]


def _ada_mod(c, w, b):
    bsz, d = c.shape
    n = w.shape[1]
    tn = 1024
    cp = jnp.zeros((8, d), F32).at[:bsz].set(c)
    out = pl.pallas_call(
        _ada_kernel,
        out_shape=jax.ShapeDtypeStruct((8, n), F32),
        grid=(n // tn,),
        in_specs=[pl.BlockSpec((8, d), lambda j: (0, 0)),
                  pl.BlockSpec((d, tn), lambda j: (0, j)),
                  pl.BlockSpec((1, tn), lambda j: (0, j))],
        out_specs=pl.BlockSpec((8, tn), lambda j: (0, j)),
        compiler_params=_cparams(("arbitrary",)),
        name="ada_mod",
    )(cp, w, b.reshape(1, n))
    return out[:bsz]


QKV_SUB = 256


def _qkv_kernel(x_ref, g_ref, sc_ref, sh_ref, w_ref, o_ref, h_scr):
    def emit(acc, rows):
        for u in range(o_ref.shape[0]):
            o_ref[u, rows, :] = acc[:, u * LANES:(u + 1) * LANES].astype(BF16)

    @pl.when(pl.program_id(1) == 0)
    def _():
        subs = [slice(i * QKV_SUB, (i + 1) * QKV_SUB) for i in range(x_ref.shape[0] // QKV_SUB)]
        hs = []
        for r in subs:
            h = (_rms(x_ref[r, :], g_ref[...]) * (1.0 + sc_ref[...]) + sh_ref[...]).astype(BF16)
            h_scr[r, :] = h
            hs.append(h)
        accs = [jnp.dot(h, w_ref[...], preferred_element_type=F32) for h in hs]
        for r, acc in zip(subs, accs):
            emit(acc, r)

    @pl.when(pl.program_id(1) > 0)
    def _():
        emit(jnp.dot(h_scr[...], w_ref[...], preferred_element_type=F32), slice(None))


def _qkv_proj(x2, g, sc, sh, w_bf, seq):
    t, d = x2.shape
    n = w_bf.shape[1]
    tm, tn = 1024, 1024
    return pl.pallas_call(
        _qkv_kernel,
        out_shape=jax.ShapeDtypeStruct((n // LANES, t, LANES), BF16),
        grid=(t // tm, n // tn),
        in_specs=[pl.BlockSpec((tm, d), lambda i, j: (i, 0)),
                  pl.BlockSpec((1, d), lambda i, j: (0, 0)),
                  pl.BlockSpec((None, 1, d), lambda i, j: (i * tm // seq, 0, 0)),
                  pl.BlockSpec((None, 1, d), lambda i, j: (i * tm // seq, 0, 0)),
                  pl.BlockSpec((d, tn), lambda i, j: (0, j))],
        out_specs=pl.BlockSpec((tn // LANES, tm, LANES), lambda i, j: (j, i, 0)),
        scratch_shapes=[pltpu.VMEM((tm, d), BF16)],
        compiler_params=_cparams(("arbitrary", "arbitrary")),
        name="qkv_proj",
    )(x2, g.reshape(1, d), sc, sh, w_bf)


def _toeplitz(vec, rows, cols):
    n = rows + cols - 1
    assert vec.shape[-1] == n
    lead = vec.shape[:-1]
    ext = jnp.concatenate([vec, jnp.zeros(lead + (1,), vec.dtype)], axis=-1)
    flat = jnp.broadcast_to(ext[..., None, :], lead + (rows, n + 1)).reshape(lead + (rows * (n + 1),))
    skew = flat[..., :rows * n].reshape(lead + (rows, n))
    return skew[..., rows - 1:rows - 1 + cols]


def _na_bias_table(rpb):
    cidx = np.arange(GRID_W)
    col_start = np.clip(cidx - NA_COLS // 2, 0, GRID_W - NA_COLS)
    col_ok = (cidx[None, :] >= col_start[:, None]) & (cidx[None, :] < col_start[:, None] + NA_COLS)
    pad = GRID_W - NA_COLS
    vec = jnp.pad(rpb.astype(F32), ((0, 0), (0, 0), (pad, pad)))
    return jnp.where(col_ok, _toeplitz(vec, GRID_W, GRID_W), NEG)


def _attn_group(qs, ks, vs, biases):
    ss = [lax.dot_general(q, k, (((1,), (1,)), ((), ())), preferred_element_type=F32) * SCALE + b
          for q, k, b in zip(qs, ks, biases)]
    ms = [jnp.max(s, axis=-1, keepdims=True) for s in ss]
    ps = [jnp.exp(s - m) for s, m in zip(ss, ms)]
    ls = [jnp.sum(p, axis=-1, keepdims=True) for p in ps]
    os = [jnp.dot(p.astype(BF16), v, preferred_element_type=F32) / l for p, v, l in zip(ps, vs, ls)]
    return os, [m + jnp.log(l) for m, l in zip(ms, ls)]


NA_GROUP = 16


def _na_kernel(q_ref, k_ref, v_ref, tab_ref, o_ref, bias_ref, *, rows):
    nkeys = NA_ROWS * GRID_W

    @pl.when(pl.program_id(1) == 0)
    def _():
        for var in range(NA_ROWS):
            bias_ref[var] = jnp.concatenate([tab_ref[var + kr] for kr in range(NA_ROWS)], axis=1)

    def body(g, carry):
        qs, ks, vs, bs, q0s = [], [], [], [], []
        for i in range(NA_GROUP):
            r = g * NA_GROUP + i
            rs = jnp.clip(r - NA_ROWS // 2, 0, rows - NA_ROWS)
            q0 = pl.multiple_of(r * GRID_W, GRID_W)
            k0 = pl.multiple_of(rs * GRID_W, GRID_W)
            q0s.append(q0)
            qs.append(q_ref[pl.ds(q0, GRID_W), :])
            ks.append(k_ref[pl.ds(k0, nkeys), :])
            vs.append(v_ref[pl.ds(k0, nkeys), :])
            bs.append(bias_ref[rs - r + (NA_ROWS - 1)])
        os, _ = _attn_group(qs, ks, vs, bs)
        for q0, o in zip(q0s, os):
            o_ref[pl.ds(q0, GRID_W), :] = o
        return carry

    lax.fori_loop(0, rows // NA_GROUP, body, 0)


def _na_attn(proj, bias, bsz, seq):
    nh = N_HEADS_NA
    t = bsz * seq
    blk = lambda off: pl.BlockSpec((None, seq, LANES), lambda h, b: (h + off, b, 0))
    return pl.pallas_call(
        functools.partial(_na_kernel, rows=seq // GRID_W),
        out_shape=jax.ShapeDtypeStruct((nh, t, LANES), F32),
        grid=(nh, bsz),
        in_specs=[blk(0), blk(nh), blk(2 * nh),
                  pl.BlockSpec((None, 2 * NA_ROWS - 1, GRID_W, GRID_W), lambda h, b: (h, 0, 0, 0))],
        out_specs=pl.BlockSpec((None, seq, LANES), lambda h, b: (h, b, 0)),
        scratch_shapes=[pltpu.VMEM((NA_ROWS, GRID_W, NA_ROWS * GRID_W), F32)],
        compiler_params=_cparams(("arbitrary", "arbitrary")),
        name="na_attn",
    )(proj, proj, proj, bias)


def _t5_bucket(rel):
    nb = T5_BUCKETS // 2
    max_exact = nb // 2
    ret = (rel > 0).astype(np.int32) * nb
    n = np.abs(rel)
    large = max_exact + (np.log(np.maximum(n, 1) / max_exact) / np.log(T5_MAX_DIST / max_exact)
                         * (nb - max_exact)).astype(np.int32)
    large = np.minimum(large, nb - 1)
    return (ret + np.where(n < max_exact, n, large)).astype(np.int32)


def _dil_geometry(sub_len):
    half = DIL_PATTERNS[0][0] // 2
    width = min(sub_len, QBLK + 2 * half)
    nblk = sub_len // QBLK
    starts = [min(max(QBLK * n - half, 0), sub_len - width) for n in range(nblk)]
    offs = sorted({ws - QBLK * n for n, ws in enumerate(starts)}, reverse=True)
    var = [offs.index(ws - QBLK * n) for n, ws in enumerate(starts)]
    return width, starts, offs, var


def _dil_bias_vecs(t5_table, seq):
    half = DIL_PATTERNS[0][0] // 2
    deltas, dils, counts, lens = [], [], [], []
    for _, dil in DIL_PATTERNS:
        width, _, offs, _ = _dil_geometry(seq // dil)
        counts.append(len(offs))
        lens.append(QBLK + width)
        for off in offs:
            d = np.full(2 * QBLK + 2 * half, 4 * half, np.int64)
            d[:QBLK + width - 1] = np.arange(QBLK + width - 1) - (QBLK - 1) + off
            deltas.append(d)
            dils.append(dil)
    delta = np.stack(deltas)
    onehot = np.eye(T5_BUCKETS, dtype=np.float32)[_t5_bucket(delta * np.asarray(dils)[:, None])]
    vals = jnp.einsum("vkb,bh->hvk", jnp.asarray(onehot), t5_table.astype(F32),
                      precision=lax.Precision.HIGHEST)
    vals = jnp.where(np.abs(delta) <= half, vals, NEG)
    out, v0 = [], 0
    for n, ln in zip(counts, lens):
        out.append(vals[:, v0:v0 + n, :ln])
        v0 += n
    return out


DIL_GROUP = 8


def _dil_kernel(q_ref, k_ref, v_ref, v1_ref, v4_ref, v16_ref, o_ref, qf, kf, vf, o4, l4, o16, l16,
                b1_ref, b4_ref, b16_ref, *, seq):
    @pl.when(pl.program_id(1) == 0)
    def _():
        for vec_ref, tab_ref in ((v1_ref, b1_ref), (v4_ref, b4_ref), (v16_ref, b16_ref)):
            n = vec_ref.shape[1]
            for v in range(vec_ref.shape[0]):
                full = jnp.broadcast_to(vec_ref[v:v + 1, :], (QBLK, n))
                tab_ref[v] = pltpu.roll(full, n - (QBLK - 1), 1, stride=1, stride_axis=0)[:, :n - QBLK]

    qf[...] = q_ref[...].astype(F32)
    kf[...] = k_ref[...].astype(F32)
    vf[...] = v_ref[...].astype(F32)

    for dil, bias_ref, o_s, l_s in ((DIL_PATTERNS[2][1], b16_ref, o16, l16), (DIL_PATTERNS[1][1], b4_ref, o4, l4)):
        width, starts, _, var = _dil_geometry(seq // dil)
        blocks = [(rho, n, ws) for rho in range(dil) for n, ws in enumerate(starts)]
        for g in range(0, len(blocks), DIL_GROUP):
            grp = blocks[g:g + DIL_GROUP]
            qrows = [pl.ds(rho + dil * QBLK * n, QBLK, stride=dil) for rho, n, _ in grp]
            krows = [pl.ds(rho + dil * ws, width, stride=dil) for rho, _, ws in grp]
            os, lses = _attn_group([qf[r, :].astype(BF16) for r in qrows],
                                   [kf[r, :].astype(BF16) for r in krows],
                                   [vf[r, :].astype(BF16) for r in krows],
                                   [bias_ref[var[n]] for _, n, _ in grp])
            for r, o, lse in zip(qrows, os, lses):
                o_s[r, :] = o
                l_s[r, :] = jnp.broadcast_to(lse, (QBLK, LANES))

    width, starts, _, _ = _dil_geometry(seq)
    nblk = len(starts)
    half = DIL_PATTERNS[0][0] // 2

    def body(g, carry):
        rows, krows, bs = [], [], []
        for i in range(DIL_GROUP):
            n = g * DIL_GROUP + i
            ws = pl.multiple_of(jnp.clip(n * QBLK - half, 0, seq - width), half)
            rows.append(pl.ds(pl.multiple_of(n * QBLK, QBLK), QBLK))
            krows.append(pl.ds(ws, width))
            bs.append(b1_ref[jnp.where(n == 0, 0, jnp.where(n == nblk - 1, 2, 1))])
        os, lses = _attn_group([q_ref[r, :] for r in rows], [k_ref[r, :] for r in krows],
                               [v_ref[r, :] for r in krows], bs)
        for r, o1, lse1 in zip(rows, os, lses):
            lse4, lse16 = l4[r, :], l16[r, :]
            mx = jnp.maximum(jnp.maximum(lse4, lse16), lse1)
            e1 = jnp.exp(lse1 - mx)
            e4 = jnp.exp(lse4 - mx)
            e16 = jnp.exp(lse16 - mx)
            o_ref[r, :] = (e1 * o1 + e4 * o4[r, :] + e16 * o16[r, :]) / (e1 + e4 + e16)
        return carry

    lax.fori_loop(0, nblk // DIL_GROUP, body, 0)


def _dil_attn(proj, t5_table, bsz, seq):
    nh = N_HEADS_DIL
    vecs = _dil_bias_vecs(t5_table, seq)
    width, _, offs, var = _dil_geometry(seq)
    assert offs == [0, -(DIL_PATTERNS[0][0] // 2), -DIL_PATTERNS[0][0]] and var[0] == 0 and var[-1] == 2
    first = 3 * N_HEADS_NA
    blk = lambda off: pl.BlockSpec((None, seq, LANES), lambda h, b: (first + off + h, b, 0))
    vec = lambda t: pl.BlockSpec((None,) + t.shape[1:], lambda h, b: (h, 0, 0))
    return pl.pallas_call(
        functools.partial(_dil_kernel, seq=seq),
        out_shape=jax.ShapeDtypeStruct((nh, bsz * seq, LANES), F32),
        grid=(nh, bsz),
        in_specs=[blk(0), blk(nh), blk(2 * nh)] + [vec(t) for t in vecs],
        out_specs=pl.BlockSpec((None, seq, LANES), lambda h, b: (h, b, 0)),
        scratch_shapes=[pltpu.VMEM((seq, LANES), F32)] * 7
        + [pltpu.VMEM((t.shape[1], QBLK, t.shape[2] - QBLK), F32) for t in vecs],
        compiler_params=_cparams(("arbitrary", "arbitrary")),
        name="dil_attn",
    )(proj, proj, proj, *vecs)


MIX_SUB = 256


def _mix_kernel(na_ref, dl_ref, gna_ref, gdl_ref, wo_ref, x_ref, gt1_ref, sc2_ref, sh2_ref,
                gpost_ref, gpre_ref, wr_ref, br_ref, x1_ref, hp_ref, lg_ref):
    nh = na_ref.shape[0]
    tm = x_ref.shape[0]
    subs = [slice(i * MIX_SUB, (i + 1) * MIX_SUB) for i in range(tm // MIX_SUB)]
    lhs = []
    for r in subs:
        na = jnp.concatenate([na_ref[h, r, :] for h in range(nh)], axis=1)
        dl = jnp.concatenate([dl_ref[h, r, :] for h in range(nh)], axis=1)
        lhs.append(jnp.concatenate([_rms(na, gna_ref[...]), _rms(dl, gdl_ref[...])], axis=1).astype(BF16))
    mixed = [jnp.dot(a, wo_ref[...], preferred_element_type=F32) for a in lhs]
    for r, m in zip(subs, mixed):
        x1 = x_ref[r, :] + gt1_ref[...] * _rms(m, gpost_ref[...])
        x1_ref[r, :] = x1
        hf = _rms(x1, gpre_ref[...]) * (1.0 + sc2_ref[...]) + sh2_ref[...]
        hb = hf.astype(BF16)
        lg_ref[r, :] = jnp.dot(hb, wr_ref[...], preferred_element_type=F32) + br_ref[...]
        half = hb.shape[1] // 2
        lo = lax.bitcast_convert_type(hb[:, :half].astype(F32), U32)
        hi = lax.bitcast_convert_type(hb[:, half:].astype(F32), U32)
        hp_ref[r, 0, :] = (hi & U32(0xFFFF0000)) | lax.shift_right_logical(lo, U32(16))


def _mix_out(out_na, out_dil, g_na, g_dil, wo_bf, x2, gt1, sc2, sh2, g_post, g_pre, wr_bf, br, seq):
    t, d = x2.shape
    nh = out_na.shape[0]
    tm = 512
    wna = g_na.shape[-1]
    row = lambda n: pl.BlockSpec((1, n), lambda i: (0, 0))
    per_b = pl.BlockSpec((None, 1, d), lambda i: (i * tm // seq, 0, 0))
    heads = pl.BlockSpec((nh, tm, LANES), lambda i: (0, i, 0))
    return pl.pallas_call(
        _mix_kernel,
        out_shape=(jax.ShapeDtypeStruct((t, d), F32),
                   jax.ShapeDtypeStruct((t, 1, d // 2), U32),
                   jax.ShapeDtypeStruct((t, LANES), F32)),
        grid=(t // tm,),
        in_specs=[heads, heads, row(wna), row(wna),
                  pl.BlockSpec((d, d), lambda i: (0, 0)),
                  pl.BlockSpec((tm, d), lambda i: (i, 0)),
                  per_b, per_b, per_b, row(d), row(d),
                  pl.BlockSpec((d, LANES), lambda i: (0, 0)), row(LANES)],
        out_specs=(pl.BlockSpec((tm, d), lambda i: (i, 0)),
                   pl.BlockSpec((tm, 1, d // 2), lambda i: (i, 0, 0)),
                   pl.BlockSpec((tm, LANES), lambda i: (i, 0))),
        compiler_params=_cparams(("arbitrary",)),
        name="mix_out",
    )(out_na, out_dil, g_na.reshape(1, wna), g_dil.reshape(1, wna), wo_bf, x2, gt1, sc2, sh2,
      g_post.reshape(1, d), g_pre.reshape(1, d), wr_bf, br)


def _route_kernel(lg_ref, idx_ref, rank_ref, gate_ref, cnt_ref, carry):
    step = pl.program_id(0)

    @pl.when(step == 0)
    def _():
        carry[...] = jnp.zeros_like(carry)

    v = lg_ref[...]
    ch = v.shape[0]
    lane = lax.broadcasted_iota(I32, v.shape, 1).astype(F32)
    vals, idxs = [], []
    for _ in range(TOP_K):
        m = jnp.max(v, axis=1, keepdims=True)
        ik = jnp.min(jnp.where(v == m, lane, float(LANES)), axis=1, keepdims=True)
        vals.append(m)
        idxs.append(ik)
        v = jnp.where(lane == ik, -jnp.inf, v)
    es = [jnp.exp(val - vals[0]) for val in vals]
    den = es[0] + es[1] + es[2] + es[3]
    sel = jnp.zeros(v.shape, F32)
    for ik in idxs:
        sel = jnp.where(lane == ik, 1.0, sel)
    ti = lax.broadcasted_iota(I32, (ch, ch), 0)
    tj = lax.broadcasted_iota(I32, (ch, ch), 1)
    lower = jnp.where(tj < ti, 1.0, 0.0).astype(BF16)
    cum = jnp.dot(lower, sel.astype(BF16), preferred_element_type=F32) + carry[0:1, :]
    idx_o = jnp.zeros(v.shape, F32)
    rank_o = jnp.zeros(v.shape, F32)
    gate_o = jnp.zeros(v.shape, F32)
    for k in range(TOP_K):
        rk = jnp.sum(jnp.where(lane == idxs[k], cum, 0.0), axis=1, keepdims=True)
        idx_o = jnp.where(lane == float(k), idxs[k], idx_o)
        rank_o = jnp.where(lane == float(k), rk, rank_o)
        gate_o = jnp.where(lane == float(k), es[k] / den, gate_o)
    idx_ref[...] = idx_o.astype(I32)
    rank_ref[...] = rank_o.astype(I32)
    gate_ref[...] = gate_o
    total = carry[...] + jnp.sum(sel, axis=0, keepdims=True)
    carry[...] = total
    cnt_ref[...] = total.astype(I32)


def _route(logits):
    t = logits.shape[0]
    ch = 512
    blk = pl.BlockSpec((ch, LANES), lambda i: (i, 0))
    return pl.pallas_call(
        _route_kernel,
        out_shape=(jax.ShapeDtypeStruct((t, LANES), I32),
                   jax.ShapeDtypeStruct((t, LANES), I32),
                   jax.ShapeDtypeStruct((t, LANES), F32),
                   jax.ShapeDtypeStruct((8, LANES), I32)),
        grid=(t // ch,),
        in_specs=[blk],
        out_specs=(blk, blk, blk, pl.BlockSpec((8, LANES), lambda i: (0, 0))),
        scratch_shapes=[pltpu.VMEM((8, LANES), F32)],
        compiler_params=_cparams(("arbitrary",)),
        name="route",
    )(logits)


def _schedule(counts, n_assign):
    max_sb = n_assign // SUPER_ROWS + N_EXPERTS
    nsb_e = (counts + SUPER_ROWS - 1) // SUPER_ROWS
    sb_end = jnp.cumsum(nsb_e)
    sb_start = sb_end - nsb_e
    row_start = (sb_start * SUPER_ROWS).astype(I32)
    nsb = sb_end[-1].astype(I32)
    s = jnp.arange(max_sb, dtype=I32)
    sb_e = jnp.minimum(jnp.searchsorted(sb_end, s, side="right"), N_EXPERTS - 1).astype(I32)
    rem = counts[sb_e] - (s - sb_start[sb_e]) * SUPER_ROWS
    sb_nb = jnp.where(s < nsb, (jnp.clip(rem, 0, SUPER_ROWS) + ROW_BLK - 1) // ROW_BLK, 0).astype(I32)
    zero_start = (s * SUPER_ROWS + jnp.maximum(sb_nb - 1, 0) * ROW_BLK).astype(I32)
    return max_sb, row_start, nsb.reshape(1), sb_e, sb_nb, zero_start


ROW_UNROLL = 8


def _dispatch_kernel(dest_s, zs_s, nsb_s, hp_ref, xs_hbm, zbuf, zsem, sem):
    step = pl.program_id(0)
    tt = hp_ref.shape[0]

    def zero_copy(s):
        return pltpu.make_async_copy(zbuf, xs_hbm.at[pl.ds(pl.multiple_of(zs_s[s], ROW_BLK), ROW_BLK)], zsem)

    @pl.when(step == 0)
    def _():
        zbuf[...] = jnp.zeros_like(zbuf)

        def zstart(s, c):
            zero_copy(s).start()
            return c

        def zwait(s, c):
            zero_copy(s).wait()
            return c

        lax.fori_loop(0, nsb_s[0], zstart, 0)
        lax.fori_loop(0, nsb_s[0], zwait, 0)

    base = step * tt * TOP_K

    def issue(jj, c):
        for u in range(ROW_UNROLL):
            j = jj * ROW_UNROLL + u
            for k in range(TOP_K):
                d = dest_s[base + j * TOP_K + k]
                pltpu.make_async_copy(hp_ref.at[pl.ds(j, 1)], xs_hbm.at[pl.ds(d, 1)], sem).start(
                    priority=k % 2)
        return c

    lax.fori_loop(0, tt // ROW_UNROLL, issue, 0)
    for k in range(TOP_K):
        pltpu.make_async_copy(hp_ref, xs_hbm.at[pl.ds(0, tt)], sem).wait()


def _dispatch(dest, zero_start, nsb, hp, n_rows):
    t, _, w = hp.shape
    tt = 256
    return pl.pallas_call(
        _dispatch_kernel,
        out_shape=jax.ShapeDtypeStruct((n_rows, 1, w), U32),
        grid_spec=pltpu.PrefetchScalarGridSpec(
            num_scalar_prefetch=3,
            grid=(t // tt,),
            in_specs=[pl.BlockSpec((tt, 1, w), lambda i, *_: (i, 0, 0))],
            out_specs=pl.BlockSpec(memory_space=pl.ANY),
            scratch_shapes=[pltpu.VMEM((ROW_BLK, 1, w), U32),
                            pltpu.SemaphoreType.DMA(()), pltpu.SemaphoreType.DMA(())]),
        compiler_params=_cparams(("arbitrary",)),
        name="dispatch",
    )(dest, zero_start, nsb, hp)


def _row_block(r):
    return pl.ds(pl.multiple_of(r * ROW_BLK, ROW_BLK), ROW_BLK)


def _moe_up_kernel(sbe_s, sbn_s, x_ref, win_ref, bin_ref, h_ref, winb, zbuf, xstage):
    nb = sbn_s[pl.program_id(0)]
    half = x_ref.shape[2]
    nsub = win_ref.shape[1] // (2 * LANES)
    lane = lax.broadcasted_iota(I32, (ROW_BLK, LANES), 1)
    even = (2 * lane) & (LANES - 1)
    odd = even + 1
    first_half = lane < LANES // 2

    def unpack(r):
        xstage[...] = x_ref[_row_block(r), 0, :]
        xu = xstage[...]
        xa = lax.bitcast_convert_type(lax.shift_left(xu, U32(16)), F32).astype(BF16)
        xb = lax.bitcast_convert_type(xu & U32(0xFFFF0000), F32).astype(BF16)
        return xa, xb

    def matmul(r):
        xa, xb = unpack(r)
        zbuf[...] = (jnp.dot(xa, winb[0:half, :], preferred_element_type=F32)
                     + jnp.dot(xb, winb[half:2 * half, :], preferred_element_type=F32) + bin_ref[...])

    def matmul_casting(r):
        xa, xb = unpack(r)
        kq = half // 2
        acc = bin_ref[...]
        for ks in range(4):
            wb = win_ref[ks * kq:(ks + 1) * kq, :].astype(BF16)
            winb[ks * kq:(ks + 1) * kq, :] = wb
            xpart = (xa, xb)[ks // 2][:, (ks % 2) * kq:(ks % 2 + 1) * kq]
            acc = acc + jnp.dot(xpart, wb, preferred_element_type=F32)
        zbuf[...] = acc

    def activation():
        hs = []
        for u in range(nsub):
            za = zbuf[:, (2 * u) * LANES:(2 * u + 1) * LANES]
            zb = zbuf[:, (2 * u + 1) * LANES:(2 * u + 2) * LANES]
            gate = jnp.where(first_half, jnp.take_along_axis(za, even, axis=1),
                             jnp.take_along_axis(zb, even, axis=1))
            up = jnp.where(first_half, jnp.take_along_axis(za, odd, axis=1),
                           jnp.take_along_axis(zb, odd, axis=1))
            gate = jnp.minimum(gate, SWIGLU_LIMIT)
            up = jnp.clip(up, -SWIGLU_LIMIT, SWIGLU_LIMIT)
            glu = gate * jax.nn.sigmoid(SWIGLU_ALPHA * gate)
            hs.append(((up + 1.0) * glu).astype(BF16))
        return jnp.concatenate(hs, axis=1)

    matmul_casting(0)

    def step(r, carry):
        h = activation()
        matmul(r + 1)
        h_ref[_row_block(r), :] = h
        return carry

    lax.fori_loop(0, nb - 1, step, 0)
    h_ref[_row_block(nb - 1), :] = activation()


def _moe_down_kernel(sbe_s, sbn_s, nsb_s, h_hbm, wout_ref, bout_ref, ys_hbm, woutb, hbuf, obuf, pend,
                     hsem, sem):
    s = pl.program_id(0)
    n = pl.program_id(1)
    nn = pl.num_programs(1)
    nb = sbn_s[s]
    chunk = obuf.shape[3]
    par = (s * nn + n) % 2
    hs = s % 2

    def h_copy(sb, slot, r):
        rows = pl.ds(pl.multiple_of(sb * SUPER_ROWS + r * ROW_BLK, ROW_BLK), ROW_BLK)
        return pltpu.make_async_copy(h_hbm.at[rows], hbuf.at[slot, _row_block(r)], hsem.at[slot])

    def fetch(sb, slot):
        def one(r, carry):
            h_copy(sb, slot, r).start()
            return carry

        lax.fori_loop(0, sbn_s[sb], one, 0)

    @pl.when(n == 0)
    def _():
        @pl.when(s == 0)
        def _():
            fetch(0, 0)

        @pl.when(s + 1 < nsb_s[0])
        def _():
            fetch(s + 1, 1 - hs)

        def one(r, carry):
            h_copy(s, hs, r).wait()
            return carry

        lax.fori_loop(0, nb, one, 0)

    def out_copy(r, par_):
        rows = pl.ds(pl.multiple_of(s * SUPER_ROWS + r * ROW_BLK, ROW_BLK), ROW_BLK)
        cols = pl.ds(pl.multiple_of(n * chunk, chunk), chunk)
        return pltpu.make_async_copy(obuf.at[par_, _row_block(r)], ys_hbm.at[rows, :, cols], sem.at[par_])

    def drain(par_):
        def one(i, carry):
            out_copy(0, par_).wait()
            return carry

        lax.fori_loop(0, pend[par_], one, 0)

    @pl.when((s == 0) & (n == 0))
    def _():
        pend[0] = 0
        pend[1] = 0

    drain(par)

    kq = hbuf.shape[2] // 4
    h0 = hbuf[hs, _row_block(0), :]
    acc = bout_ref[...]
    for ks in range(4):
        wb = wout_ref[ks * kq:(ks + 1) * kq, :].astype(BF16)
        woutb[ks * kq:(ks + 1) * kq, :] = wb
        acc = acc + jnp.dot(h0[:, ks * kq:(ks + 1) * kq], wb, preferred_element_type=F32)
    obuf[par, _row_block(0), 0, :] = acc

    def project(r, nrows):
        rows = pl.ds(pl.multiple_of(r * ROW_BLK, ROW_BLK), nrows)
        obuf[par, rows, 0, :] = jnp.dot(hbuf[hs, rows, :], woutb[...],
                                        preferred_element_type=F32) + bout_ref[...]

    def pair(p, carry):
        project(1 + 2 * p, 2 * ROW_BLK)
        return carry

    lax.fori_loop(0, (nb - 1) // 2, pair, 0)

    @pl.when((nb - 1) % 2 == 1)
    def _():
        project(nb - 1, ROW_BLK)

    def start(r, carry):
        out_copy(r, par).start()
        return carry

    lax.fori_loop(0, nb, start, 0)
    pend[par] = nb

    @pl.when((s == nsb_s[0] - 1) & (n == nn - 1))
    def _():
        drain(0)
        drain(1)


def _moe_ffn(sb_e, sb_nb, nsb, xs, w_in, b_in, w_out, b_out, max_sb):
    ne, d, f2 = w_in.shape
    ff = w_out.shape[1]
    half = xs.shape[2]
    rows = max_sb * SUPER_ROWS
    sem = ("arbitrary", "arbitrary")

    h = pl.pallas_call(
        _moe_up_kernel,
        out_shape=jax.ShapeDtypeStruct((rows, ff), BF16),
        grid_spec=pltpu.PrefetchScalarGridSpec(
            num_scalar_prefetch=2,
            grid=(nsb[0], ff // UP_CHUNK),
            in_specs=[pl.BlockSpec((SUPER_ROWS, 1, half), lambda s, j, sbe, sbn: (s, 0, 0)),
                      pl.BlockSpec((None, d, 2 * UP_CHUNK), lambda s, j, sbe, sbn: (sbe[s], 0, j)),
                      pl.BlockSpec((None, 1, 2 * UP_CHUNK), lambda s, j, sbe, sbn: (sbe[s], 0, j))],
            out_specs=pl.BlockSpec((SUPER_ROWS, UP_CHUNK), lambda s, j, sbe, sbn: (s, j)),
            scratch_shapes=[pltpu.VMEM((d, 2 * UP_CHUNK), BF16), pltpu.VMEM((ROW_BLK, 2 * UP_CHUNK), F32),
                            pltpu.VMEM((ROW_BLK, half), U32)]),
        compiler_params=_cparams(sem),
        name="moe_up",
    )(sb_e, sb_nb, xs, w_in, b_in.reshape(ne, 1, f2))

    return pl.pallas_call(
        _moe_down_kernel,
        out_shape=jax.ShapeDtypeStruct((rows, 1, d), F32),
        grid_spec=pltpu.PrefetchScalarGridSpec(
            num_scalar_prefetch=3,
            grid=(nsb[0], d // DOWN_CHUNK),
            in_specs=[pl.BlockSpec(memory_space=pl.ANY),
                      pl.BlockSpec((None, ff, DOWN_CHUNK), lambda s, n, sbe, sbn, nsb_: (sbe[s], 0, n)),
                      pl.BlockSpec((None, 1, DOWN_CHUNK), lambda s, n, sbe, sbn, nsb_: (sbe[s], 0, n))],
            out_specs=pl.BlockSpec(memory_space=pl.ANY),
            scratch_shapes=[pltpu.VMEM((ff, DOWN_CHUNK), BF16),
                            pltpu.VMEM((2, SUPER_ROWS, ff), BF16),
                            pltpu.VMEM((2, SUPER_ROWS, 1, DOWN_CHUNK), F32),
                            pltpu.SMEM((2,), I32), pltpu.SemaphoreType.DMA((2,)),
                            pltpu.SemaphoreType.DMA((2,))]),
        compiler_params=_cparams(sem),
        name="moe_down",
    )(sb_e, sb_nb, nsb, h, w_out, b_out.reshape(ne, 1, d))


COMBINE_GROUP = 32


def _combine_kernel(dest_s, gate_ref, x1_ref, gt2_ref, g_ref, ys_hbm, o_ref, buf, stage, sem):
    step = pl.program_id(0)
    nstep = pl.num_programs(0)
    tt = x1_ref.shape[0]
    slot = step % 2

    def issue(tile, slot_, jj):
        base = tile * tt * TOP_K
        for u in range(COMBINE_GROUP):
            j = jj * COMBINE_GROUP + u
            for k in range(TOP_K):
                d = dest_s[base + j * TOP_K + k]
                pltpu.make_async_copy(ys_hbm.at[pl.ds(d, 1)], buf.at[slot_, k, pl.ds(j, 1)],
                                      sem.at[slot_]).start(priority=k % 2)

    def finish(jj):
        rows = pl.ds(pl.multiple_of(jj * COMBINE_GROUP, COMBINE_GROUP), COMBINE_GROUP)
        g = gate_ref[rows, :]
        y = None
        for k in range(TOP_K):
            stage[k] = buf[slot, k, rows, 0, :]
            term = stage[k] * g[:, k:k + 1]
            y = term if y is None else y + term
        o_ref[rows, :] = x1_ref[rows, :] + gt2_ref[...] * _rms(y, g_ref[...])

    @pl.when(step == 0)
    def _():
        def first(jj, c):
            issue(0, 0, jj)
            return c

        lax.fori_loop(0, tt // COMBINE_GROUP, first, 0)

    for k in range(TOP_K):
        pltpu.make_async_copy(ys_hbm.at[pl.ds(0, tt)], buf.at[slot, k], sem.at[slot]).wait()

    @pl.when(step + 1 < nstep)
    def _():
        def both(jj, c):
            finish(jj)
            issue(step + 1, 1 - slot, jj)
            return c

        lax.fori_loop(0, tt // COMBINE_GROUP, both, 0)

    @pl.when(step + 1 == nstep)
    def _():
        def last(jj, c):
            finish(jj)
            return c

        lax.fori_loop(0, tt // COMBINE_GROUP, last, 0)


def _combine(dest, gates, x1, gt2, g_post, ys, seq):
    t, d = x1.shape
    tt = 128
    return pl.pallas_call(
        _combine_kernel,
        out_shape=jax.ShapeDtypeStruct((t, d), F32),
        grid_spec=pltpu.PrefetchScalarGridSpec(
            num_scalar_prefetch=1,
            grid=(t // tt,),
            in_specs=[pl.BlockSpec((tt, LANES), lambda i, *_: (i, 0)),
                      pl.BlockSpec((tt, d), lambda i, *_: (i, 0)),
                      pl.BlockSpec((None, 1, d), lambda i, *_: (i * tt // seq, 0, 0)),
                      pl.BlockSpec((1, d), lambda i, *_: (0, 0)),
                      pl.BlockSpec(memory_space=pl.ANY)],
            out_specs=pl.BlockSpec((tt, d), lambda i, *_: (i, 0)),
            scratch_shapes=[pltpu.VMEM((2, TOP_K, tt, 1, d), F32), pltpu.VMEM((TOP_K, COMBINE_GROUP, d), F32),
                            pltpu.SemaphoreType.DMA((2,))]),
        compiler_params=_cparams(("arbitrary",)),
        name="combine",
    )(dest, gates, x1, gt2, g_post.reshape(1, d), ys)


def kernel(x, c, w_ada, b_ada, g_pre_mix, g_post_mix, w_in, rpb_na, t5_table, g_out_na, g_out_dil, w_o,
           g_pre_ffn, g_post_ffn, w_router, b_router, w_e_in, b_e_in, w_e_out, b_e_out):
    bsz, seq, d = x.shape
    t = bsz * seq
    for l in range(w_ada.shape[0]):
        mod = _ada_mod(c, w_ada[l], b_ada[l])
        sh1, sc1, gt1, sh2, sc2, gt2 = (m.reshape(bsz, 1, d) for m in jnp.split(mod, 6, axis=-1))
        x2 = x.reshape(t, d)

        proj = _qkv_proj(x2, g_pre_mix[l], sc1, sh1, w_in[l].astype(BF16), seq)
        out_na = _na_attn(proj, _na_bias_table(rpb_na[l]), bsz, seq)
        out_dil = _dil_attn(proj, t5_table, bsz, seq)

        ne = w_router.shape[-1]
        wr = jnp.zeros((d, LANES), BF16).at[:, :ne].set(w_router[l].astype(BF16))
        br = jnp.full((1, LANES), NEG, F32).at[0, :ne].set(b_router[l])
        x1, hp, logits = _mix_out(out_na, out_dil, g_out_na[l], g_out_dil[l], w_o[l].astype(BF16), x2,
                                  gt1, sc2, sh2, g_post_mix[l], g_pre_ffn[l], wr, br, seq)

        idx, rank, gates, cnt = _route(logits)
        max_sb, row_start, nsb, sb_e, sb_nb, zero_start = _schedule(cnt[0, :ne], t * TOP_K)
        eidx = idx[:, :TOP_K].reshape(-1)
        onehot = eidx[:, None] == jnp.arange(ne, dtype=I32)[None, :]
        dest = rank[:, :TOP_K].reshape(-1) + jnp.sum(jnp.where(onehot, row_start[None, :], 0), axis=1)
        xs = _dispatch(dest, zero_start, nsb, hp, max_sb * SUPER_ROWS)
        ys = _moe_ffn(sb_e, sb_nb, nsb, xs, w_e_in[l], b_e_in[l], w_e_out[l], b_e_out[l], max_sb)
        x = _combine(dest, gates, x1, gt2, g_post_ffn[l], ys, seq).reshape(bsz, seq, d)
    return x
```

```python
import functools

import numpy as np
import jax
import jax.numpy as jnp
from jax import lax
from jax.experimental import pallas as pl
from jax.experimental.pallas import tpu as pltpu

F32 = jnp.float32
BF16 = jnp.bfloat16
U32 = jnp.uint32
I32 = jnp.int32

HEAD_DIM = 128
N_HEADS_NA = 8
N_HEADS_DIL = 8
GRID_W = 64
NA_ROWS = 8
NA_COLS = 16
DIL_PATTERNS = ((128, 1), (512, 4), (2048, 16))
T5_BUCKETS = 32
T5_MAX_DIST = 1024
N_EXPERTS = 32
TOP_K = 4
SWIGLU_LIMIT = 7.0
SWIGLU_ALPHA = 1.702
EPS = 1e-6
NEG = -1e30
SCALE = HEAD_DIM ** -0.5

LANES = 128
QBLK = 128
ROW_BLK = 256
SUPER_BLKS = 7
SUPER_ROWS = ROW_BLK * SUPER_BLKS
UP_CHUNK = 512
DOWN_CHUNK = 1024
VMEM_LIMIT = 56 * 1024 * 1024


def _cparams(sem, vmem=VMEM_LIMIT):
    return pltpu.CompilerParams(dimension_semantics=sem, vmem_limit_bytes=vmem)


def _rms(x, g):
    return x * lax.rsqrt(jnp.mean(x * x, axis=-1, keepdims=True) + EPS) * g


def _ada_kernel(c_ref, w_ref, b_ref, o_ref):
    c = c_ref[...]
    s = c * jax.nn.sigmoid(c)
    o_ref[...] = jnp.dot(s.astype(BF16), w_ref[...].astype(BF16),
                         preferred_element_type=F32) + b_ref[...]


def _ada_mod(c, w, b):
    bsz, d = c.shape
    n = w.shape[1]
    tn = 1024
    cp = jnp.zeros((8, d), F32).at[:bsz].set(c)
    out = pl.pallas_call(
        _ada_kernel,
        out_shape=jax.ShapeDtypeStruct((8, n), F32),
        grid=(n // tn,),
        in_specs=[pl.BlockSpec((8, d), lambda j: (0, 0)),
                  pl.BlockSpec((d, tn), lambda j: (0, j)),
                  pl.BlockSpec((1, tn), lambda j: (0, j))],
        out_specs=pl.BlockSpec((8, tn), lambda j: (0, j)),
        compiler_params=_cparams(("arbitrary",)),
        name="ada_mod",
    )(cp, w, b.reshape(1, n))
    return out[:bsz]


QKV_SUB = 256


def _qkv_kernel(x_ref, g_ref, sc_ref, sh_ref, w_ref, o_ref, h_scr, wb):
    def emit(acc, rows):
        for u in range(o_ref.shape[0]):
            o_ref[u, rows, :] = acc[:, u * LANES:(u + 1) * LANES].astype(BF16)

    @pl.when(pl.program_id(1) == 0)
    def _():
        subs = [slice(i * QKV_SUB, (i + 1) * QKV_SUB) for i in range(x_ref.shape[0] // QKV_SUB)]
        hs = []
        for r in subs:
            h = (_rms(x_ref[r, :], g_ref[...]) * (1.0 + sc_ref[...]) + sh_ref[...]).astype(BF16)
            h_scr[r, :] = h
            hs.append(h)
        wb[...] = w_ref[...].astype(BF16)
        accs = [jnp.dot(h, wb[...], preferred_element_type=F32) for h in hs]
        for r, acc in zip(subs, accs):
            emit(acc, r)

    @pl.when(pl.program_id(1) > 0)
    def _():
        wb[...] = w_ref[...].astype(BF16)
        emit(jnp.dot(h_scr[...], wb[...], preferred_element_type=F32), slice(None))


def _qkv_proj(x2, g, sc, sh, w, seq):
    t, d = x2.shape
    n = w.shape[1]
    tm, tn = 1024, 1024
    return pl.pallas_call(
        _qkv_kernel,
        out_shape=jax.ShapeDtypeStruct((n // LANES, t, LANES), BF16),
        grid=(t // tm, n // tn),
        in_specs=[pl.BlockSpec((tm, d), lambda i, j: (i, 0)),
                  pl.BlockSpec((1, d), lambda i, j: (0, 0)),
                  pl.BlockSpec((None, 1, d), lambda i, j: (i * tm // seq, 0, 0)),
                  pl.BlockSpec((None, 1, d), lambda i, j: (i * tm // seq, 0, 0)),
                  pl.BlockSpec((d, tn), lambda i, j: (0, j))],
        out_specs=pl.BlockSpec((tn // LANES, tm, LANES), lambda i, j: (j, i, 0)),
        scratch_shapes=[pltpu.VMEM((tm, d), BF16), pltpu.VMEM((d, tn), BF16)],
        compiler_params=_cparams(("arbitrary", "arbitrary")),
        name="qkv_proj",
    )(x2, g.reshape(1, d), sc, sh, w)


def _toeplitz(vec, rows, cols):
    n = rows + cols - 1
    assert vec.shape[-1] == n
    lead = vec.shape[:-1]
    ext = jnp.concatenate([vec, jnp.zeros(lead + (1,), vec.dtype)], axis=-1)
    flat = jnp.broadcast_to(ext[..., None, :], lead + (rows, n + 1)).reshape(lead + (rows * (n + 1),))
    skew = flat[..., :rows * n].reshape(lead + (rows, n))
    return skew[..., rows - 1:rows - 1 + cols]


def _na_bias_table(rpb):
    cidx = np.arange(GRID_W)
    col_start = np.clip(cidx - NA_COLS // 2, 0, GRID_W - NA_COLS)
    col_ok = (cidx[None, :] >= col_start[:, None]) & (cidx[None, :] < col_start[:, None] + NA_COLS)
    pad = GRID_W - NA_COLS
    vec = jnp.pad(rpb.astype(F32), ((0, 0), (0, 0), (pad, pad)))
    return jnp.where(col_ok, _toeplitz(vec, GRID_W, GRID_W), NEG)


def _attn_group(qs, ks, vs, biases):
    ss = [lax.dot_general(q, k, (((1,), (1,)), ((), ())), preferred_element_type=F32) * SCALE + b
          for q, k, b in zip(qs, ks, biases)]
    ms = [jnp.max(s, axis=-1, keepdims=True) for s in ss]
    ps = [jnp.exp(s - m) for s, m in zip(ss, ms)]
    ls = [jnp.sum(p, axis=-1, keepdims=True) for p in ps]
    os = [jnp.dot(p.astype(BF16), v, preferred_element_type=F32) / l for p, v, l in zip(ps, vs, ls)]
    return os, [m + jnp.log(l) for m, l in zip(ms, ls)]


NA_GROUP = 16


def _na_kernel(q_ref, k_ref, v_ref, tab_ref, o_ref, bias_ref, *, rows):
    nkeys = NA_ROWS * GRID_W

    @pl.when(pl.program_id(1) == 0)
    def _():
        for var in range(NA_ROWS):
            bias_ref[var] = jnp.concatenate([tab_ref[var + kr] for kr in range(NA_ROWS)], axis=1)

    def body(g, carry):
        qs, ks, vs, bs, q0s = [], [], [], [], []
        for i in range(NA_GROUP):
            r = g * NA_GROUP + i
            rs = jnp.clip(r - NA_ROWS // 2, 0, rows - NA_ROWS)
            q0 = pl.multiple_of(r * GRID_W, GRID_W)
            k0 = pl.multiple_of(rs * GRID_W, GRID_W)
            q0s.append(q0)
            qs.append(q_ref[pl.ds(q0, GRID_W), :])
            ks.append(k_ref[pl.ds(k0, nkeys), :])
            vs.append(v_ref[pl.ds(k0, nkeys), :])
            bs.append(bias_ref[rs - r + (NA_ROWS - 1)])
        os, _ = _attn_group(qs, ks, vs, bs)
        for q0, o in zip(q0s, os):
            o_ref[pl.ds(q0, GRID_W), :] = o
        return carry

    lax.fori_loop(0, rows // NA_GROUP, body, 0)


def _na_attn(proj, bias, bsz, seq):
    nh = N_HEADS_NA
    t = bsz * seq
    blk = lambda off: pl.BlockSpec((None, seq, LANES), lambda h, b: (h + off, b, 0))
    return pl.pallas_call(
        functools.partial(_na_kernel, rows=seq // GRID_W),
        out_shape=jax.ShapeDtypeStruct((nh, t, LANES), F32),
        grid=(nh, bsz),
        in_specs=[blk(0), blk(nh), blk(2 * nh),
                  pl.BlockSpec((None, 2 * NA_ROWS - 1, GRID_W, GRID_W), lambda h, b: (h, 0, 0, 0))],
        out_specs=pl.BlockSpec((None, seq, LANES), lambda h, b: (h, b, 0)),
        scratch_shapes=[pltpu.VMEM((NA_ROWS, GRID_W, NA_ROWS * GRID_W), F32)],
        compiler_params=_cparams(("arbitrary", "arbitrary")),
        name="na_attn",
    )(proj, proj, proj, bias)


def _t5_bucket(rel):
    nb = T5_BUCKETS // 2
    max_exact = nb // 2
    ret = (rel > 0).astype(np.int32) * nb
    n = np.abs(rel)
    large = max_exact + (np.log(np.maximum(n, 1) / max_exact) / np.log(T5_MAX_DIST / max_exact)
                         * (nb - max_exact)).astype(np.int32)
    large = np.minimum(large, nb - 1)
    return (ret + np.where(n < max_exact, n, large)).astype(np.int32)


def _dil_geometry(sub_len):
    half = DIL_PATTERNS[0][0] // 2
    width = min(sub_len, QBLK + 2 * half)
    nblk = sub_len // QBLK
    starts = [min(max(QBLK * n - half, 0), sub_len - width) for n in range(nblk)]
    offs = sorted({ws - QBLK * n for n, ws in enumerate(starts)}, reverse=True)
    var = [offs.index(ws - QBLK * n) for n, ws in enumerate(starts)]
    return width, starts, offs, var


def _dil_bias_vecs(t5_table, seq):
    half = DIL_PATTERNS[0][0] // 2
    deltas, dils, counts, lens = [], [], [], []
    for _, dil in DIL_PATTERNS:
        width, _, offs, _ = _dil_geometry(seq // dil)
        counts.append(len(offs))
        lens.append(QBLK + width)
        for off in offs:
            d = np.full(2 * QBLK + 2 * half, 4 * half, np.int64)
            d[:QBLK + width - 1] = np.arange(QBLK + width - 1) - (QBLK - 1) + off
            deltas.append(d)
            dils.append(dil)
    delta = np.stack(deltas)
    onehot = np.eye(T5_BUCKETS, dtype=np.float32)[_t5_bucket(delta * np.asarray(dils)[:, None])]
    vals = jnp.einsum("vkb,bh->hvk", jnp.asarray(onehot), t5_table.astype(F32),
                      precision=lax.Precision.HIGHEST)
    vals = jnp.where(np.abs(delta) <= half, vals, NEG)
    out, v0 = [], 0
    for n, ln in zip(counts, lens):
        out.append(vals[:, v0:v0 + n, :ln])
        v0 += n
    return out


DIL_GROUP = 8


def _dil_kernel(q_ref, k_ref, v_ref, v1_ref, v4_ref, v16_ref, o_ref, qf, kf, vf, o4, l4, o16, l16,
                b1_ref, b4_ref, b16_ref, *, seq):
    @pl.when(pl.program_id(1) == 0)
    def _():
        for vec_ref, tab_ref in ((v1_ref, b1_ref), (v4_ref, b4_ref), (v16_ref, b16_ref)):
            n = vec_ref.shape[1]
            for v in range(vec_ref.shape[0]):
                full = jnp.broadcast_to(vec_ref[v:v + 1, :], (QBLK, n))
                tab_ref[v] = pltpu.roll(full, n - (QBLK - 1), 1, stride=1, stride_axis=0)[:, :n - QBLK]

    qf[...] = q_ref[...].astype(F32)
    kf[...] = k_ref[...].astype(F32)
    vf[...] = v_ref[...].astype(F32)

    for dil, bias_ref, o_s, l_s in ((DIL_PATTERNS[2][1], b16_ref, o16, l16), (DIL_PATTERNS[1][1], b4_ref, o4, l4)):
        width, starts, _, var = _dil_geometry(seq // dil)
        blocks = [(rho, n, ws) for rho in range(dil) for n, ws in enumerate(starts)]
        for g in range(0, len(blocks), DIL_GROUP):
            grp = blocks[g:g + DIL_GROUP]
            qrows = [pl.ds(rho + dil * QBLK * n, QBLK, stride=dil) for rho, n, _ in grp]
            krows = [pl.ds(rho + dil * ws, width, stride=dil) for rho, _, ws in grp]
            os, lses = _attn_group([qf[r, :].astype(BF16) for r in qrows],
                                   [kf[r, :].astype(BF16) for r in krows],
                                   [vf[r, :].astype(BF16) for r in krows],
                                   [bias_ref[var[n]] for _, n, _ in grp])
            for r, o, lse in zip(qrows, os, lses):
                o_s[r, :] = o
                l_s[r, :] = jnp.broadcast_to(lse, (QBLK, LANES))

    width, starts, _, _ = _dil_geometry(seq)
    nblk = len(starts)
    half = DIL_PATTERNS[0][0] // 2

    def body(g, carry):
        rows, krows, bs = [], [], []
        for i in range(DIL_GROUP):
            n = g * DIL_GROUP + i
            ws = pl.multiple_of(jnp.clip(n * QBLK - half, 0, seq - width), half)
            rows.append(pl.ds(pl.multiple_of(n * QBLK, QBLK), QBLK))
            krows.append(pl.ds(ws, width))
            bs.append(b1_ref[jnp.where(n == 0, 0, jnp.where(n == nblk - 1, 2, 1))])
        os, lses = _attn_group([q_ref[r, :] for r in rows], [k_ref[r, :] for r in krows],
                               [v_ref[r, :] for r in krows], bs)
        for r, o1, lse1 in zip(rows, os, lses):
            lse4, lse16 = l4[r, :], l16[r, :]
            mx = jnp.maximum(jnp.maximum(lse4, lse16), lse1)
            e1 = jnp.exp(lse1 - mx)
            e4 = jnp.exp(lse4 - mx)
            e16 = jnp.exp(lse16 - mx)
            o_ref[r, :] = (e1 * o1 + e4 * o4[r, :] + e16 * o16[r, :]) / (e1 + e4 + e16)
        return carry

    lax.fori_loop(0, nblk // DIL_GROUP, body, 0)


def _dil_attn(proj, t5_table, bsz, seq):
    nh = N_HEADS_DIL
    vecs = _dil_bias_vecs(t5_table, seq)
    width, _, offs, var = _dil_geometry(seq)
    assert offs == [0, -(DIL_PATTERNS[0][0] // 2), -DIL_PATTERNS[0][0]] and var[0] == 0 and var[-1] == 2
    first = 3 * N_HEADS_NA
    blk = lambda off: pl.BlockSpec((None, seq, LANES), lambda h, b: (first + off + h, b, 0))
    vec = lambda t: pl.BlockSpec((None,) + t.shape[1:], lambda h, b: (h, 0, 0))
    return pl.pallas_call(
        functools.partial(_dil_kernel, seq=seq),
        out_shape=jax.ShapeDtypeStruct((nh, bsz * seq, LANES), F32),
        grid=(nh, bsz),
        in_specs=[blk(0), blk(nh), blk(2 * nh)] + [vec(t) for t in vecs],
        out_specs=pl.BlockSpec((None, seq, LANES), lambda h, b: (h, b, 0)),
        scratch_shapes=[pltpu.VMEM((seq, LANES), F32)] * 7
        + [pltpu.VMEM((t.shape[1], QBLK, t.shape[2] - QBLK), F32) for t in vecs],
        compiler_params=_cparams(("arbitrary", "arbitrary")),
        name="dil_attn",
    )(proj, proj, proj, *vecs)


MIX_SUB = 256


def _mix_kernel(na_ref, dl_ref, gna_ref, gdl_ref, wo_ref, x_ref, gt1_ref, sc2_ref, sh2_ref,
                gpost_ref, gpre_ref, wr_ref, br_ref, x1_ref, hp_ref, lg_ref):
    nh = na_ref.shape[0]
    tm = x_ref.shape[0]
    subs = [slice(i * MIX_SUB, (i + 1) * MIX_SUB) for i in range(tm // MIX_SUB)]
    lhs = []
    for r in subs:
        na = jnp.concatenate([na_ref[h, r, :] for h in range(nh)], axis=1)
        dl = jnp.concatenate([dl_ref[h, r, :] for h in range(nh)], axis=1)
        lhs.append(jnp.concatenate([_rms(na, gna_ref[...]), _rms(dl, gdl_ref[...])], axis=1).astype(BF16))
    mixed = [jnp.dot(a, wo_ref[...], preferred_element_type=F32) for a in lhs]
    for r, m in zip(subs, mixed):
        x1 = x_ref[r, :] + gt1_ref[...] * _rms(m, gpost_ref[...])
        x1_ref[r, :] = x1
        hf = _rms(x1, gpre_ref[...]) * (1.0 + sc2_ref[...]) + sh2_ref[...]
        hb = hf.astype(BF16)
        lg_ref[r, :] = jnp.dot(hb, wr_ref[...], preferred_element_type=F32) + br_ref[...]
        half = hb.shape[1] // 2
        lo = lax.bitcast_convert_type(hb[:, :half].astype(F32), U32)
        hi = lax.bitcast_convert_type(hb[:, half:].astype(F32), U32)
        hp_ref[r, 0, :] = (hi & U32(0xFFFF0000)) | lax.shift_right_logical(lo, U32(16))


def _mix_out(out_na, out_dil, g_na, g_dil, wo_bf, x2, gt1, sc2, sh2, g_post, g_pre, wr_bf, br, seq):
    t, d = x2.shape
    nh = out_na.shape[0]
    tm = 512
    wna = g_na.shape[-1]
    row = lambda n: pl.BlockSpec((1, n), lambda i: (0, 0))
    per_b = pl.BlockSpec((None, 1, d), lambda i: (i * tm // seq, 0, 0))
    heads = pl.BlockSpec((nh, tm, LANES), lambda i: (0, i, 0))
    return pl.pallas_call(
        _mix_kernel,
        out_shape=(jax.ShapeDtypeStruct((t, d), F32),
                   jax.ShapeDtypeStruct((t, 1, d // 2), U32),
                   jax.ShapeDtypeStruct((t, LANES), F32)),
        grid=(t // tm,),
        in_specs=[heads, heads, row(wna), row(wna),
                  pl.BlockSpec((d, d), lambda i: (0, 0)),
                  pl.BlockSpec((tm, d), lambda i: (i, 0)),
                  per_b, per_b, per_b, row(d), row(d),
                  pl.BlockSpec((d, LANES), lambda i: (0, 0)), row(LANES)],
        out_specs=(pl.BlockSpec((tm, d), lambda i: (i, 0)),
                   pl.BlockSpec((tm, 1, d // 2), lambda i: (i, 0, 0)),
                   pl.BlockSpec((tm, LANES), lambda i: (i, 0))),
        compiler_params=_cparams(("arbitrary",)),
        name="mix_out",
    )(out_na, out_dil, g_na.reshape(1, wna), g_dil.reshape(1, wna), wo_bf, x2, gt1, sc2, sh2,
      g_post.reshape(1, d), g_pre.reshape(1, d), wr_bf, br)


def _route_kernel(lg_ref, idx_ref, rank_ref, gate_ref, cnt_ref, carry):
    step = pl.program_id(0)

    @pl.when(step == 0)
    def _():
        carry[...] = jnp.zeros_like(carry)

    v = lg_ref[...]
    ch = v.shape[0]
    lane = lax.broadcasted_iota(I32, v.shape, 1).astype(F32)
    vals, idxs = [], []
    for _ in range(TOP_K):
        m = jnp.max(v, axis=1, keepdims=True)
        ik = jnp.min(jnp.where(v == m, lane, float(LANES)), axis=1, keepdims=True)
        vals.append(m)
        idxs.append(ik)
        v = jnp.where(lane == ik, -jnp.inf, v)
    es = [jnp.exp(val - vals[0]) for val in vals]
    den = es[0] + es[1] + es[2] + es[3]
    sel = jnp.zeros(v.shape, F32)
    for ik in idxs:
        sel = jnp.where(lane == ik, 1.0, sel)
    ti = lax.broadcasted_iota(I32, (ch, ch), 0)
    tj = lax.broadcasted_iota(I32, (ch, ch), 1)
    lower = jnp.where(tj < ti, 1.0, 0.0).astype(BF16)
    cum = jnp.dot(lower, sel.astype(BF16), preferred_element_type=F32) + carry[0:1, :]
    idx_o = jnp.zeros(v.shape, F32)
    rank_o = jnp.zeros(v.shape, F32)
    gate_o = jnp.zeros(v.shape, F32)
    for k in range(TOP_K):
        rk = jnp.sum(jnp.where(lane == idxs[k], cum, 0.0), axis=1, keepdims=True)
        idx_o = jnp.where(lane == float(k), idxs[k], idx_o)
        rank_o = jnp.where(lane == float(k), rk, rank_o)
        gate_o = jnp.where(lane == float(k), es[k] / den, gate_o)
    idx_ref[...] = idx_o.astype(I32)
    rank_ref[...] = rank_o.astype(I32)
    gate_ref[...] = gate_o
    total = carry[...] + jnp.sum(sel, axis=0, keepdims=True)
    carry[...] = total
    cnt_ref[...] = total.astype(I32)


def _route(logits):
    t = logits.shape[0]
    ch = 512
    blk = pl.BlockSpec((ch, LANES), lambda i: (i, 0))
    return pl.pallas_call(
        _route_kernel,
        out_shape=(jax.ShapeDtypeStruct((t, LANES), I32),
                   jax.ShapeDtypeStruct((t, LANES), I32),
                   jax.ShapeDtypeStruct((t, LANES), F32),
                   jax.ShapeDtypeStruct((8, LANES), I32)),
        grid=(t // ch,),
        in_specs=[blk],
        out_specs=(blk, blk, blk, pl.BlockSpec((8, LANES), lambda i: (0, 0))),
        scratch_shapes=[pltpu.VMEM((8, LANES), F32)],
        compiler_params=_cparams(("arbitrary",)),
        name="route",
    )(logits)


def _schedule(counts, n_assign):
    max_sb = n_assign // SUPER_ROWS + N_EXPERTS
    nsb_e = (counts + SUPER_ROWS - 1) // SUPER_ROWS
    sb_end = jnp.cumsum(nsb_e)
    sb_start = sb_end - nsb_e
    row_start = (sb_start * SUPER_ROWS).astype(I32)
    nsb = sb_end[-1].astype(I32)
    s = jnp.arange(max_sb, dtype=I32)
    sb_e = jnp.minimum(jnp.searchsorted(sb_end, s, side="right"), N_EXPERTS - 1).astype(I32)
    rem = counts[sb_e] - (s - sb_start[sb_e]) * SUPER_ROWS
    sb_nb = jnp.where(s < nsb, (jnp.clip(rem, 0, SUPER_ROWS) + ROW_BLK - 1) // ROW_BLK, 0).astype(I32)
    zero_start = (s * SUPER_ROWS + jnp.maximum(sb_nb - 1, 0) * ROW_BLK).astype(I32)
    return max_sb, row_start, nsb.reshape(1), sb_e, sb_nb, zero_start


ROW_UNROLL = 8


def _dispatch_kernel(dest_s, zs_s, nsb_s, hp_ref, xs_hbm, zbuf, zsem, sem):
    step = pl.program_id(0)
    tt = hp_ref.shape[0]

    def zero_copy(s):
        return pltpu.make_async_copy(zbuf, xs_hbm.at[pl.ds(pl.multiple_of(zs_s[s], ROW_BLK), ROW_BLK)], zsem)

    @pl.when(step == 0)
    def _():
        zbuf[...] = jnp.zeros_like(zbuf)

        def zstart(s, c):
            zero_copy(s).start()
            return c

        def zwait(s, c):
            zero_copy(s).wait()
            return c

        lax.fori_loop(0, nsb_s[0], zstart, 0)
        lax.fori_loop(0, nsb_s[0], zwait, 0)

    base = step * tt * TOP_K

    def issue(jj, c):
        for u in range(ROW_UNROLL):
            j = jj * ROW_UNROLL + u
            for k in range(TOP_K):
                d = dest_s[base + j * TOP_K + k]
                pltpu.make_async_copy(hp_ref.at[pl.ds(j, 1)], xs_hbm.at[pl.ds(d, 1)], sem).start(
                    priority=k % 2)
        return c

    lax.fori_loop(0, tt // ROW_UNROLL, issue, 0)
    for k in range(TOP_K):
        pltpu.make_async_copy(hp_ref, xs_hbm.at[pl.ds(0, tt)], sem).wait()


def _dispatch(dest, zero_start, nsb, hp, n_rows):
    t, _, w = hp.shape
    tt = 256
    return pl.pallas_call(
        _dispatch_kernel,
        out_shape=jax.ShapeDtypeStruct((n_rows, 1, w), U32),
        grid_spec=pltpu.PrefetchScalarGridSpec(
            num_scalar_prefetch=3,
            grid=(t // tt,),
            in_specs=[pl.BlockSpec((tt, 1, w), lambda i, *_: (i, 0, 0))],
            out_specs=pl.BlockSpec(memory_space=pl.ANY),
            scratch_shapes=[pltpu.VMEM((ROW_BLK, 1, w), U32),
                            pltpu.SemaphoreType.DMA(()), pltpu.SemaphoreType.DMA(())]),
        compiler_params=_cparams(("arbitrary",)),
        name="dispatch",
    )(dest, zero_start, nsb, hp)


def _row_block(r):
    return pl.ds(pl.multiple_of(r * ROW_BLK, ROW_BLK), ROW_BLK)


def _moe_up_kernel(sbe_s, sbn_s, x_ref, win_ref, bin_ref, h_ref, winb, zbuf, xstage):
    nb = sbn_s[pl.program_id(0)]
    half = x_ref.shape[2]
    nsub = win_ref.shape[1] // (2 * LANES)
    lane = lax.broadcasted_iota(I32, (ROW_BLK, LANES), 1)
    even = (2 * lane) & (LANES - 1)
    odd = even + 1
    first_half = lane < LANES // 2

    def unpack(r):
        xstage[...] = x_ref[_row_block(r), 0, :]
        xu = xstage[...]
        xa = lax.bitcast_convert_type(lax.shift_left(xu, U32(16)), F32).astype(BF16)
        xb = lax.bitcast_convert_type(xu & U32(0xFFFF0000), F32).astype(BF16)
        return xa, xb

    def matmul(r):
        xa, xb = unpack(r)
        zbuf[...] = (jnp.dot(xa, winb[0:half, :], preferred_element_type=F32)
                     + jnp.dot(xb, winb[half:2 * half, :], preferred_element_type=F32) + bin_ref[...])

    def matmul_casting(r):
        xa, xb = unpack(r)
        kq = half // 2
        acc = bin_ref[...]
        for ks in range(4):
            wb = win_ref[ks * kq:(ks + 1) * kq, :].astype(BF16)
            winb[ks * kq:(ks + 1) * kq, :] = wb
            xpart = (xa, xb)[ks // 2][:, (ks % 2) * kq:(ks % 2 + 1) * kq]
            acc = acc + jnp.dot(xpart, wb, preferred_element_type=F32)
        zbuf[...] = acc

    def activation():
        hs = []
        for u in range(nsub):
            za = zbuf[:, (2 * u) * LANES:(2 * u + 1) * LANES]
            zb = zbuf[:, (2 * u + 1) * LANES:(2 * u + 2) * LANES]
            gate = jnp.where(first_half, jnp.take_along_axis(za, even, axis=1),
                             jnp.take_along_axis(zb, even, axis=1))
            up = jnp.where(first_half, jnp.take_along_axis(za, odd, axis=1),
                           jnp.take_along_axis(zb, odd, axis=1))
            gate = jnp.minimum(gate, SWIGLU_LIMIT)
            up = jnp.clip(up, -SWIGLU_LIMIT, SWIGLU_LIMIT)
            glu = gate * jax.nn.sigmoid(SWIGLU_ALPHA * gate)
            hs.append(((up + 1.0) * glu).astype(BF16))
        return jnp.concatenate(hs, axis=1)

    matmul_casting(0)

    def step(r, carry):
        h = activation()
        matmul(r + 1)
        h_ref[_row_block(r), :] = h
        return carry

    lax.fori_loop(0, nb - 1, step, 0)
    h_ref[_row_block(nb - 1), :] = activation()


def _moe_down_kernel(sbe_s, sbn_s, nsb_s, h_hbm, wout_ref, bout_ref, ys_hbm, woutb, hbuf, obuf, pend,
                     hsem, sem):
    s = pl.program_id(0)
    n = pl.program_id(1)
    nn = pl.num_programs(1)
    nb = sbn_s[s]
    chunk = obuf.shape[3]
    par = (s * nn + n) % 2
    hs = s % 2

    def h_copy(sb, slot, r):
        rows = pl.ds(pl.multiple_of(sb * SUPER_ROWS + r * ROW_BLK, ROW_BLK), ROW_BLK)
        return pltpu.make_async_copy(h_hbm.at[rows], hbuf.at[slot, _row_block(r)], hsem.at[slot])

    def fetch(sb, slot):
        def one(r, carry):
            h_copy(sb, slot, r).start()
            return carry

        lax.fori_loop(0, sbn_s[sb], one, 0)

    @pl.when(n == 0)
    def _():
        @pl.when(s == 0)
        def _():
            fetch(0, 0)

        @pl.when(s + 1 < nsb_s[0])
        def _():
            fetch(s + 1, 1 - hs)

        def one(r, carry):
            h_copy(s, hs, r).wait()
            return carry

        lax.fori_loop(0, nb, one, 0)

    def out_copy(r, par_):
        rows = pl.ds(pl.multiple_of(s * SUPER_ROWS + r * ROW_BLK, ROW_BLK), ROW_BLK)
        cols = pl.ds(pl.multiple_of(n * chunk, chunk), chunk)
        return pltpu.make_async_copy(obuf.at[par_, _row_block(r)], ys_hbm.at[rows, :, cols], sem.at[par_])

    def drain(par_):
        def one(i, carry):
            out_copy(0, par_).wait()
            return carry

        lax.fori_loop(0, pend[par_], one, 0)

    @pl.when((s == 0) & (n == 0))
    def _():
        pend[0] = 0
        pend[1] = 0

    drain(par)

    kq = hbuf.shape[2] // 4
    h0 = hbuf[hs, _row_block(0), :]
    acc = bout_ref[...]
    for ks in range(4):
        wb = wout_ref[ks * kq:(ks + 1) * kq, :].astype(BF16)
        woutb[ks * kq:(ks + 1) * kq, :] = wb
        acc = acc + jnp.dot(h0[:, ks * kq:(ks + 1) * kq], wb, preferred_element_type=F32)
    obuf[par, _row_block(0), 0, :] = acc

    def project(r, nrows):
        rows = pl.ds(pl.multiple_of(r * ROW_BLK, ROW_BLK), nrows)
        obuf[par, rows, 0, :] = jnp.dot(hbuf[hs, rows, :], woutb[...],
                                        preferred_element_type=F32) + bout_ref[...]

    def pair(p, carry):
        project(1 + 2 * p, 2 * ROW_BLK)
        return carry

    lax.fori_loop(0, (nb - 1) // 2, pair, 0)

    @pl.when((nb - 1) % 2 == 1)
    def _():
        project(nb - 1, ROW_BLK)

    def start(r, carry):
        out_copy(r, par).start()
        return carry

    lax.fori_loop(0, nb, start, 0)
    pend[par] = nb

    @pl.when((s == nsb_s[0] - 1) & (n == nn - 1))
    def _():
        drain(0)
        drain(1)


def _moe_ffn(sb_e, sb_nb, nsb, xs, w_in, b_in, w_out, b_out, max_sb):
    ne, d, f2 = w_in.shape
    ff = w_out.shape[1]
    half = xs.shape[2]
    rows = max_sb * SUPER_ROWS
    sem = ("arbitrary", "arbitrary")

    h = pl.pallas_call(
        _moe_up_kernel,
        out_shape=jax.ShapeDtypeStruct((rows, ff), BF16),
        grid_spec=pltpu.PrefetchScalarGridSpec(
            num_scalar_prefetch=2,
            grid=(nsb[0], ff // UP_CHUNK),
            in_specs=[pl.BlockSpec((SUPER_ROWS, 1, half), lambda s, j, sbe, sbn: (s, 0, 0)),
                      pl.BlockSpec((None, d, 2 * UP_CHUNK), lambda s, j, sbe, sbn: (sbe[s], 0, j)),
                      pl.BlockSpec((None, 1, 2 * UP_CHUNK), lambda s, j, sbe, sbn: (sbe[s], 0, j))],
            out_specs=pl.BlockSpec((SUPER_ROWS, UP_CHUNK), lambda s, j, sbe, sbn: (s, j)),
            scratch_shapes=[pltpu.VMEM((d, 2 * UP_CHUNK), BF16), pltpu.VMEM((ROW_BLK, 2 * UP_CHUNK), F32),
                            pltpu.VMEM((ROW_BLK, half), U32)]),
        compiler_params=_cparams(sem),
        name="moe_up",
    )(sb_e, sb_nb, xs, w_in, b_in.reshape(ne, 1, f2))

    return pl.pallas_call(
        _moe_down_kernel,
        out_shape=jax.ShapeDtypeStruct((rows, 1, d), F32),
        grid_spec=pltpu.PrefetchScalarGridSpec(
            num_scalar_prefetch=3,
            grid=(nsb[0], d // DOWN_CHUNK),
            in_specs=[pl.BlockSpec(memory_space=pl.ANY),
                      pl.BlockSpec((None, ff, DOWN_CHUNK), lambda s, n, sbe, sbn, nsb_: (sbe[s], 0, n)),
                      pl.BlockSpec((None, 1, DOWN_CHUNK), lambda s, n, sbe, sbn, nsb_: (sbe[s], 0, n))],
            out_specs=pl.BlockSpec(memory_space=pl.ANY),
            scratch_shapes=[pltpu.VMEM((ff, DOWN_CHUNK), BF16),
                            pltpu.VMEM((2, SUPER_ROWS, ff), BF16),
                            pltpu.VMEM((2, SUPER_ROWS, 1, DOWN_CHUNK), F32),
                            pltpu.SMEM((2,), I32), pltpu.SemaphoreType.DMA((2,)),
                            pltpu.SemaphoreType.DMA((2,))]),
        compiler_params=_cparams(sem),
        name="moe_down",
    )(sb_e, sb_nb, nsb, h, w_out, b_out.reshape(ne, 1, d))


COMBINE_GROUP = 32


def _combine_kernel(dest_s, gate_ref, x1_ref, gt2_ref, g_ref, ys_hbm, o_ref, buf, stage, sem):
    step = pl.program_id(0)
    nstep = pl.num_programs(0)
    tt = x1_ref.shape[0]
    slot = step % 2

    def issue(tile, slot_, jj):
        base = tile * tt * TOP_K
        for u in range(COMBINE_GROUP):
            j = jj * COMBINE_GROUP + u
            for k in range(TOP_K):
                d = dest_s[base + j * TOP_K + k]
                pltpu.make_async_copy(ys_hbm.at[pl.ds(d, 1)], buf.at[slot_, k, pl.ds(j, 1)],
                                      sem.at[slot_]).start(priority=k % 2)

    def finish(jj):
        rows = pl.ds(pl.multiple_of(jj * COMBINE_GROUP, COMBINE_GROUP), COMBINE_GROUP)
        g = gate_ref[rows, :]
        y = None
        for k in range(TOP_K):
            stage[k] = buf[slot, k, rows, 0, :]
            term = stage[k] * g[:, k:k + 1]
            y = term if y is None else y + term
        o_ref[rows, :] = x1_ref[rows, :] + gt2_ref[...] * _rms(y, g_ref[...])

    @pl.when(step == 0)
    def _():
        def first(jj, c):
            issue(0, 0, jj)
            return c

        lax.fori_loop(0, tt // COMBINE_GROUP, first, 0)

    for k in range(TOP_K):
        pltpu.make_async_copy(ys_hbm.at[pl.ds(0, tt)], buf.at[slot, k], sem.at[slot]).wait()

    @pl.when(step + 1 < nstep)
    def _():
        def both(jj, c):
            finish(jj)
            issue(step + 1, 1 - slot, jj)
            return c

        lax.fori_loop(0, tt // COMBINE_GROUP, both, 0)

    @pl.when(step + 1 == nstep)
    def _():
        def last(jj, c):
            finish(jj)
            return c

        lax.fori_loop(0, tt // COMBINE_GROUP, last, 0)


def _combine(dest, gates, x1, gt2, g_post, ys, seq):
    t, d = x1.shape
    tt = 128
    return pl.pallas_call(
        _combine_kernel,
        out_shape=jax.ShapeDtypeStruct((t, d), F32),
        grid_spec=pltpu.PrefetchScalarGridSpec(
            num_scalar_prefetch=1,
            grid=(t // tt,),
            in_specs=[pl.BlockSpec((tt, LANES), lambda i, *_: (i, 0)),
                      pl.BlockSpec((tt, d), lambda i, *_: (i, 0)),
                      pl.BlockSpec((None, 1, d), lambda i, *_: (i * tt // seq, 0, 0)),
                      pl.BlockSpec((1, d), lambda i, *_: (0, 0)),
                      pl.BlockSpec(memory_space=pl.ANY)],
            out_specs=pl.BlockSpec((tt, d), lambda i, *_: (i, 0)),
            scratch_shapes=[pltpu.VMEM((2, TOP_K, tt, 1, d), F32), pltpu.VMEM((TOP_K, COMBINE_GROUP, d), F32),
                            pltpu.SemaphoreType.DMA((2,))]),
        compiler_params=_cparams(("arbitrary",)),
        name="combine",
    )(dest, gates, x1, gt2, g_post.reshape(1, d), ys)


def kernel(x, c, w_ada, b_ada, g_pre_mix, g_post_mix, w_in, rpb_na, t5_table, g_out_na, g_out_dil, w_o,
           g_pre_ffn, g_post_ffn, w_router, b_router, w_e_in, b_e_in, w_e_out, b_e_out):
    bsz, seq, d = x.shape
    t = bsz * seq
    for l in range(w_ada.shape[0]):
        mod = _ada_mod(c, w_ada[l], b_ada[l])
        sh1, sc1, gt1, sh2, sc2, gt2 = (m.reshape(bsz, 1, d) for m in jnp.split(mod, 6, axis=-1))
        x2 = x.reshape(t, d)

        proj = _qkv_proj(x2, g_pre_mix[l], sc1, sh1, w_in[l], seq)
        out_na = _na_attn(proj, _na_bias_table(rpb_na[l]), bsz, seq)
        out_dil = _dil_attn(proj, t5_table, bsz, seq)

        ne = w_router.shape[-1]
        wr = jnp.zeros((d, LANES), BF16).at[:, :ne].set(w_router[l].astype(BF16))
        br = jnp.full((1, LANES), NEG, F32).at[0, :ne].set(b_router[l])
        x1, hp, logits = _mix_out(out_na, out_dil, g_out_na[l], g_out_dil[l], w_o[l].astype(BF16), x2,
                                  gt1, sc2, sh2, g_post_mix[l], g_pre_ffn[l], wr, br, seq)

        idx, rank, gates, cnt = _route(logits)
        max_sb, row_start, nsb, sb_e, sb_nb, zero_start = _schedule(cnt[0, :ne], t * TOP_K)
        eidx = idx[:, :TOP_K].reshape(-1)
        onehot = eidx[:, None] == jnp.arange(ne, dtype=I32)[None, :]
        dest = rank[:, :TOP_K].reshape(-1) + jnp.sum(jnp.where(onehot, row_start[None, :], 0), axis=1)
        xs = _dispatch(dest, zero_start, nsb, hp, max_sb * SUPER_ROWS)
        ys = _moe_ffn(sb_e, sb_nb, nsb, xs, w_e_in[l], b_e_in[l], w_e_out[l], b_e_out[l], max_sb)
        x = _combine(dest, gates, x1, gt2, g_post_ffn[l], ys, seq).reshape(bsz, seq, d)
    return x
```

```python
import functools

import numpy as np
import jax
import jax.numpy as jnp
from jax import lax
from jax.experimental import pallas as pl
from jax.experimental.pallas import tpu as pltpu

F32 = jnp.float32
BF16 = jnp.bfloat16
U32 = jnp.uint32
I32 = jnp.int32

HEAD_DIM = 128
N_HEADS_NA = 8
N_HEADS_DIL = 8
GRID_W = 64
NA_ROWS = 8
NA_COLS = 16
DIL_PATTERNS = ((128, 1), (512, 4), (2048, 16))
T5_BUCKETS = 32
T5_MAX_DIST = 1024
N_EXPERTS = 32
TOP_K = 4
SWIGLU_LIMIT = 7.0
SWIGLU_ALPHA = 1.702
EPS = 1e-6
NEG = -1e30
SCALE = HEAD_DIM ** -0.5

LANES = 128
QBLK = 128
ROW_BLK = 256
SUPER_BLKS = 7
SUPER_ROWS = ROW_BLK * SUPER_BLKS
UP_CHUNK = 512
DOWN_CHUNK = 1024
VMEM_LIMIT = 56 * 1024 * 1024


def _cparams(sem, vmem=VMEM_LIMIT):
    return pltpu.CompilerParams(dimension_semantics=sem, vmem_limit_bytes=vmem)


def _rms(x, g):
    return x * lax.rsqrt(jnp.mean(x * x, axis=-1, keepdims=True) + EPS) * g


def _ada_kernel(c_ref, w_ref, b_ref, o_ref):
    c = c_ref[...]
    s = c * jax.nn.sigmoid(c)
    o_ref[...] = jnp.dot(s.astype(BF16), w_ref[...].astype(BF16),
                         preferred_element_type=F32) + b_ref[...]


def _ada_mod(c, w, b):
    bsz, d = c.shape
    n = w.shape[1]
    tn = 1024
    cp = jnp.zeros((8, d), F32).at[:bsz].set(c)
    out = pl.pallas_call(
        _ada_kernel,
        out_shape=jax.ShapeDtypeStruct((8, n), F32),
        grid=(n // tn,),
        in_specs=[pl.BlockSpec((8, d), lambda j: (0, 0)),
                  pl.BlockSpec((d, tn), lambda j: (0, j)),
                  pl.BlockSpec((1, tn), lambda j: (0, j))],
        out_specs=pl.BlockSpec((8, tn), lambda j: (0, j)),
        compiler_params=_cparams(("arbitrary",)),
        name="ada_mod",
    )(cp, w, b.reshape(1, n))
    return out[:bsz]


QKV_SUB = 256


def _qkv_kernel(x_ref, g_ref, sc_ref, sh_ref, w_ref, o_ref, h_scr, wb):
    def emit(acc, rows):
        for u in range(o_ref.shape[0]):
            o_ref[u, rows, :] = acc[:, u * LANES:(u + 1) * LANES].astype(BF16)

    @pl.when(pl.program_id(1) == 0)
    def _():
        subs = [slice(i * QKV_SUB, (i + 1) * QKV_SUB) for i in range(x_ref.shape[0] // QKV_SUB)]
        hs = []
        for r in subs:
            h = (_rms(x_ref[r, :], g_ref[...]) * (1.0 + sc_ref[...]) + sh_ref[...]).astype(BF16)
            h_scr[r, :] = h
            hs.append(h)
        wb[...] = w_ref[...].astype(BF16)
        accs = [jnp.dot(h, wb[...], preferred_element_type=F32) for h in hs]
        for r, acc in zip(subs, accs):
            emit(acc, r)

    @pl.when(pl.program_id(1) > 0)
    def _():
        wb[...] = w_ref[...].astype(BF16)
        emit(jnp.dot(h_scr[...], wb[...], preferred_element_type=F32), slice(None))


def _qkv_proj(x2, g, sc, sh, w, seq):
    t, d = x2.shape
    n = w.shape[1]
    tm, tn = 1024, 1024
    return pl.pallas_call(
        _qkv_kernel,
        out_shape=jax.ShapeDtypeStruct((n // LANES, t, LANES), BF16),
        grid=(t // tm, n // tn),
        in_specs=[pl.BlockSpec((tm, d), lambda i, j: (i, 0)),
                  pl.BlockSpec((1, d), lambda i, j: (0, 0)),
                  pl.BlockSpec((None, 1, d), lambda i, j: (i * tm // seq, 0, 0)),
                  pl.BlockSpec((None, 1, d), lambda i, j: (i * tm // seq, 0, 0)),
                  pl.BlockSpec((d, tn), lambda i, j: (0, j))],
        out_specs=pl.BlockSpec((tn // LANES, tm, LANES), lambda i, j: (j, i, 0)),
        scratch_shapes=[pltpu.VMEM((tm, d), BF16), pltpu.VMEM((d, tn), BF16)],
        compiler_params=_cparams(("arbitrary", "arbitrary")),
        name="qkv_proj",
    )(x2, g.reshape(1, d), sc, sh, w)


def _toeplitz(vec, rows, cols):
    n = rows + cols - 1
    assert vec.shape[-1] == n
    lead = vec.shape[:-1]
    ext = jnp.concatenate([vec, jnp.zeros(lead + (1,), vec.dtype)], axis=-1)
    flat = jnp.broadcast_to(ext[..., None, :], lead + (rows, n + 1)).reshape(lead + (rows * (n + 1),))
    skew = flat[..., :rows * n].reshape(lead + (rows, n))
    return skew[..., rows - 1:rows - 1 + cols]


def _na_bias_table(rpb):
    cidx = np.arange(GRID_W)
    col_start = np.clip(cidx - NA_COLS // 2, 0, GRID_W - NA_COLS)
    col_ok = (cidx[None, :] >= col_start[:, None]) & (cidx[None, :] < col_start[:, None] + NA_COLS)
    pad = GRID_W - NA_COLS
    vec = jnp.pad(rpb.astype(F32), ((0, 0), (0, 0), (pad, pad)))
    return jnp.where(col_ok, _toeplitz(vec, GRID_W, GRID_W), NEG)


def _attn_group(qs, ks, vs, biases):
    ss = [lax.dot_general(q, k, (((1,), (1,)), ((), ())), preferred_element_type=F32) * SCALE + b
          for q, k, b in zip(qs, ks, biases)]
    ms = [jnp.max(s, axis=-1, keepdims=True) for s in ss]
    ps = [jnp.exp(s - m) for s, m in zip(ss, ms)]
    ls = [jnp.sum(p, axis=-1, keepdims=True) for p in ps]
    os = [jnp.dot(p.astype(BF16), v, preferred_element_type=F32) / l for p, v, l in zip(ps, vs, ls)]
    return os, [m + jnp.log(l) for m, l in zip(ms, ls)]


NA_GROUP = 16


def _na_kernel(q_ref, k_ref, v_ref, tab_ref, o_ref, bias_ref, *, rows):
    nkeys = NA_ROWS * GRID_W

    @pl.when(pl.program_id(1) == 0)
    def _():
        for var in range(NA_ROWS):
            bias_ref[var] = jnp.concatenate([tab_ref[var + kr] for kr in range(NA_ROWS)], axis=1)

    def body(g, carry):
        qs, ks, vs, bs, q0s = [], [], [], [], []
        for i in range(NA_GROUP):
            r = g * NA_GROUP + i
            rs = jnp.clip(r - NA_ROWS // 2, 0, rows - NA_ROWS)
            q0 = pl.multiple_of(r * GRID_W, GRID_W)
            k0 = pl.multiple_of(rs * GRID_W, GRID_W)
            q0s.append(q0)
            qs.append(q_ref[pl.ds(q0, GRID_W), :])
            ks.append(k_ref[pl.ds(k0, nkeys), :])
            vs.append(v_ref[pl.ds(k0, nkeys), :])
            bs.append(bias_ref[rs - r + (NA_ROWS - 1)])
        os, _ = _attn_group(qs, ks, vs, bs)
        for q0, o in zip(q0s, os):
            o_ref[pl.ds(q0, GRID_W), :] = o
        return carry

    lax.fori_loop(0, rows // NA_GROUP, body, 0)


def _na_attn(proj, bias, bsz, seq):
    nh = N_HEADS_NA
    t = bsz * seq
    blk = lambda off: pl.BlockSpec((None, seq, LANES), lambda h, b: (h + off, b, 0))
    return pl.pallas_call(
        functools.partial(_na_kernel, rows=seq // GRID_W),
        out_shape=jax.ShapeDtypeStruct((nh, t, LANES), F32),
        grid=(nh, bsz),
        in_specs=[blk(0), blk(nh), blk(2 * nh),
                  pl.BlockSpec((None, 2 * NA_ROWS - 1, GRID_W, GRID_W), lambda h, b: (h, 0, 0, 0))],
        out_specs=pl.BlockSpec((None, seq, LANES), lambda h, b: (h, b, 0)),
        scratch_shapes=[pltpu.VMEM((NA_ROWS, GRID_W, NA_ROWS * GRID_W), F32)],
        compiler_params=_cparams(("arbitrary", "arbitrary")),
        name="na_attn",
    )(proj, proj, proj, bias)


def _t5_bucket(rel):
    nb = T5_BUCKETS // 2
    max_exact = nb // 2
    ret = (rel > 0).astype(np.int32) * nb
    n = np.abs(rel)
    large = max_exact + (np.log(np.maximum(n, 1) / max_exact) / np.log(T5_MAX_DIST / max_exact)
                         * (nb - max_exact)).astype(np.int32)
    large = np.minimum(large, nb - 1)
    return (ret + np.where(n < max_exact, n, large)).astype(np.int32)


def _dil_geometry(sub_len):
    half = DIL_PATTERNS[0][0] // 2
    width = min(sub_len, QBLK + 2 * half)
    nblk = sub_len // QBLK
    starts = [min(max(QBLK * n - half, 0), sub_len - width) for n in range(nblk)]
    offs = sorted({ws - QBLK * n for n, ws in enumerate(starts)}, reverse=True)
    var = [offs.index(ws - QBLK * n) for n, ws in enumerate(starts)]
    return width, starts, offs, var


def _dil_bias_vecs(t5_table, seq):
    half = DIL_PATTERNS[0][0] // 2
    deltas, dils, counts, lens = [], [], [], []
    for _, dil in DIL_PATTERNS:
        width, _, offs, _ = _dil_geometry(seq // dil)
        counts.append(len(offs))
        lens.append(QBLK + width)
        for off in offs:
            d = np.full(2 * QBLK + 2 * half, 4 * half, np.int64)
            d[:QBLK + width - 1] = np.arange(QBLK + width - 1) - (QBLK - 1) + off
            deltas.append(d)
            dils.append(dil)
    delta = np.stack(deltas)
    onehot = np.eye(T5_BUCKETS, dtype=np.float32)[_t5_bucket(delta * np.asarray(dils)[:, None])]
    vals = jnp.einsum("vkb,bh->hvk", jnp.asarray(onehot), t5_table.astype(F32),
                      precision=lax.Precision.HIGHEST)
    vals = jnp.where(np.abs(delta) <= half, vals, NEG)
    out, v0 = [], 0
    for n, ln in zip(counts, lens):
        out.append(vals[:, v0:v0 + n, :ln])
        v0 += n
    return out


DIL_GROUP = 8


def _dil_kernel(q_ref, k_ref, v_ref, v1_ref, v4_ref, v16_ref, o_ref, qf, kf, vf, o4, l4, o16, l16,
                b1_ref, b4_ref, b16_ref, *, seq):
    @pl.when(pl.program_id(1) == 0)
    def _():
        for vec_ref, tab_ref in ((v1_ref, b1_ref), (v4_ref, b4_ref), (v16_ref, b16_ref)):
            n = vec_ref.shape[1]
            for v in range(vec_ref.shape[0]):
                full = jnp.broadcast_to(vec_ref[v:v + 1, :], (QBLK, n))
                tab_ref[v] = pltpu.roll(full, n - (QBLK - 1), 1, stride=1, stride_axis=0)[:, :n - QBLK]

    qf[...] = q_ref[...].astype(F32)
    kf[...] = k_ref[...].astype(F32)
    vf[...] = v_ref[...].astype(F32)

    for dil, bias_ref, o_s, l_s in ((DIL_PATTERNS[2][1], b16_ref, o16, l16), (DIL_PATTERNS[1][1], b4_ref, o4, l4)):
        width, starts, _, var = _dil_geometry(seq // dil)
        blocks = [(rho, n, ws) for rho in range(dil) for n, ws in enumerate(starts)]
        for g in range(0, len(blocks), DIL_GROUP):
            grp = blocks[g:g + DIL_GROUP]
            qrows = [pl.ds(rho + dil * QBLK * n, QBLK, stride=dil) for rho, n, _ in grp]
            krows = [pl.ds(rho + dil * ws, width, stride=dil) for rho, _, ws in grp]
            os, lses = _attn_group([qf[r, :].astype(BF16) for r in qrows],
                                   [kf[r, :].astype(BF16) for r in krows],
                                   [vf[r, :].astype(BF16) for r in krows],
                                   [bias_ref[var[n]] for _, n, _ in grp])
            for r, o, lse in zip(qrows, os, lses):
                o_s[r, :] = o
                l_s[r, :] = jnp.broadcast_to(lse, (QBLK, LANES))

    width, starts, _, _ = _dil_geometry(seq)
    nblk = len(starts)
    half = DIL_PATTERNS[0][0] // 2

    def body(g, carry):
        rows, krows, bs = [], [], []
        for i in range(DIL_GROUP):
            n = g * DIL_GROUP + i
            ws = pl.multiple_of(jnp.clip(n * QBLK - half, 0, seq - width), half)
            rows.append(pl.ds(pl.multiple_of(n * QBLK, QBLK), QBLK))
            krows.append(pl.ds(ws, width))
            bs.append(b1_ref[jnp.where(n == 0, 0, jnp.where(n == nblk - 1, 2, 1))])
        os, lses = _attn_group([q_ref[r, :] for r in rows], [k_ref[r, :] for r in krows],
                               [v_ref[r, :] for r in krows], bs)
        for r, o1, lse1 in zip(rows, os, lses):
            lse4, lse16 = l4[r, :], l16[r, :]
            mx = jnp.maximum(jnp.maximum(lse4, lse16), lse1)
            e1 = jnp.exp(lse1 - mx)
            e4 = jnp.exp(lse4 - mx)
            e16 = jnp.exp(lse16 - mx)
            o_ref[r, :] = (e1 * o1 + e4 * o4[r, :] + e16 * o16[r, :]) / (e1 + e4 + e16)
        return carry

    lax.fori_loop(0, nblk // DIL_GROUP, body, 0)


def _dil_attn(proj, t5_table, bsz, seq):
    nh = N_HEADS_DIL
    vecs = _dil_bias_vecs(t5_table, seq)
    width, _, offs, var = _dil_geometry(seq)
    assert offs == [0, -(DIL_PATTERNS[0][0] // 2), -DIL_PATTERNS[0][0]] and var[0] == 0 and var[-1] == 2
    first = 3 * N_HEADS_NA
    blk = lambda off: pl.BlockSpec((None, seq, LANES), lambda h, b: (first + off + h, b, 0))
    vec = lambda t: pl.BlockSpec((None,) + t.shape[1:], lambda h, b: (h, 0, 0))
    return pl.pallas_call(
        functools.partial(_dil_kernel, seq=seq),
        out_shape=jax.ShapeDtypeStruct((nh, bsz * seq, LANES), F32),
        grid=(nh, bsz),
        in_specs=[blk(0), blk(nh), blk(2 * nh)] + [vec(t) for t in vecs],
        out_specs=pl.BlockSpec((None, seq, LANES), lambda h, b: (h, b, 0)),
        scratch_shapes=[pltpu.VMEM((seq, LANES), F32)] * 7
        + [pltpu.VMEM((t.shape[1], QBLK, t.shape[2] - QBLK), F32) for t in vecs],
        compiler_params=_cparams(("arbitrary", "arbitrary")),
        name="dil_attn",
    )(proj, proj, proj, *vecs)


MIX_SUB = 256


def _mix_kernel(na_ref, dl_ref, gna_ref, gdl_ref, wo_ref, x_ref, gt1_ref, sc2_ref, sh2_ref,
                gpost_ref, gpre_ref, wr_ref, br_ref, x1_ref, hp_ref, lg_ref):
    nh = na_ref.shape[0]
    tm = x_ref.shape[0]
    subs = [slice(i * MIX_SUB, (i + 1) * MIX_SUB) for i in range(tm // MIX_SUB)]
    lhs = []
    for r in subs:
        na = jnp.concatenate([na_ref[h, r, :] for h in range(nh)], axis=1)
        dl = jnp.concatenate([dl_ref[h, r, :] for h in range(nh)], axis=1)
        lhs.append(jnp.concatenate([_rms(na, gna_ref[...]), _rms(dl, gdl_ref[...])], axis=1).astype(BF16))
    mixed = [jnp.dot(a, wo_ref[...], preferred_element_type=F32) for a in lhs]
    for r, m in zip(subs, mixed):
        x1 = x_ref[r, :] + gt1_ref[...] * _rms(m, gpost_ref[...])
        x1_ref[r, :] = x1
        hf = _rms(x1, gpre_ref[...]) * (1.0 + sc2_ref[...]) + sh2_ref[...]
        hb = hf.astype(BF16)
        lg_ref[r, :] = jnp.dot(hb, wr_ref[...], preferred_element_type=F32) + br_ref[...]
        half = hb.shape[1] // 2
        lo = lax.bitcast_convert_type(hb[:, :half].astype(F32), U32)
        hi = lax.bitcast_convert_type(hb[:, half:].astype(F32), U32)
        hp_ref[r, 0, :] = (hi & U32(0xFFFF0000)) | lax.shift_right_logical(lo, U32(16))


def _mix_out(out_na, out_dil, g_na, g_dil, wo_bf, x2, gt1, sc2, sh2, g_post, g_pre, wr_bf, br, seq):
    t, d = x2.shape
    nh = out_na.shape[0]
    tm = 512
    wna = g_na.shape[-1]
    row = lambda n: pl.BlockSpec((1, n), lambda i: (0, 0))
    per_b = pl.BlockSpec((None, 1, d), lambda i: (i * tm // seq, 0, 0))
    heads = pl.BlockSpec((nh, tm, LANES), lambda i: (0, i, 0))
    return pl.pallas_call(
        _mix_kernel,
        out_shape=(jax.ShapeDtypeStruct((t, d), F32),
                   jax.ShapeDtypeStruct((t, 1, d // 2), U32),
                   jax.ShapeDtypeStruct((t, LANES), F32)),
        grid=(t // tm,),
        in_specs=[heads, heads, row(wna), row(wna),
                  pl.BlockSpec((d, d), lambda i: (0, 0)),
                  pl.BlockSpec((tm, d), lambda i: (i, 0)),
                  per_b, per_b, per_b, row(d), row(d),
                  pl.BlockSpec((d, LANES), lambda i: (0, 0)), row(LANES)],
        out_specs=(pl.BlockSpec((tm, d), lambda i: (i, 0)),
                   pl.BlockSpec((tm, 1, d // 2), lambda i: (i, 0, 0)),
                   pl.BlockSpec((tm, LANES), lambda i: (i, 0))),
        compiler_params=_cparams(("arbitrary",)),
        name="mix_out",
    )(out_na, out_dil, g_na.reshape(1, wna), g_dil.reshape(1, wna), wo_bf, x2, gt1, sc2, sh2,
      g_post.reshape(1, d), g_pre.reshape(1, d), wr_bf, br)


def _route_kernel(lg_ref, idx_ref, rank_ref, gate_ref, cnt_ref, carry):
    step = pl.program_id(0)

    @pl.when(step == 0)
    def _():
        carry[...] = jnp.zeros_like(carry)

    v = lg_ref[...]
    ch = v.shape[0]
    lane = lax.broadcasted_iota(I32, v.shape, 1).astype(F32)
    vals, idxs = [], []
    for _ in range(TOP_K):
        m = jnp.max(v, axis=1, keepdims=True)
        ik = jnp.min(jnp.where(v == m, lane, float(LANES)), axis=1, keepdims=True)
        vals.append(m)
        idxs.append(ik)
        v = jnp.where(lane == ik, -jnp.inf, v)
    es = [jnp.exp(val - vals[0]) for val in vals]
    den = es[0] + es[1] + es[2] + es[3]
    sel = jnp.zeros(v.shape, F32)
    for ik in idxs:
        sel = jnp.where(lane == ik, 1.0, sel)
    ti = lax.broadcasted_iota(I32, (ch, ch), 0)
    tj = lax.broadcasted_iota(I32, (ch, ch), 1)
    lower = jnp.where(tj < ti, 1.0, 0.0).astype(BF16)
    cum = jnp.dot(lower, sel.astype(BF16), preferred_element_type=F32) + carry[0:1, :]
    idx_o = jnp.zeros(v.shape, F32)
    rank_o = jnp.zeros(v.shape, F32)
    gate_o = jnp.zeros(v.shape, F32)
    for k in range(TOP_K):
        rk = jnp.sum(jnp.where(lane == idxs[k], cum, 0.0), axis=1, keepdims=True)
        idx_o = jnp.where(lane == float(k), idxs[k], idx_o)
        rank_o = jnp.where(lane == float(k), rk, rank_o)
        gate_o = jnp.where(lane == float(k), es[k] / den, gate_o)
    idx_ref[...] = idx_o.astype(I32)
    rank_ref[...] = rank_o.astype(I32)
    gate_ref[...] = gate_o
    total = carry[...] + jnp.sum(sel, axis=0, keepdims=True)
    carry[...] = total
    cnt_ref[...] = total.astype(I32)


def _route(logits):
    t = logits.shape[0]
    ch = 512
    blk = pl.BlockSpec((ch, LANES), lambda i: (i, 0))
    return pl.pallas_call(
        _route_kernel,
        out_shape=(jax.ShapeDtypeStruct((t, LANES), I32),
                   jax.ShapeDtypeStruct((t, LANES), I32),
                   jax.ShapeDtypeStruct((t, LANES), F32),
                   jax.ShapeDtypeStruct((8, LANES), I32)),
        grid=(t // ch,),
        in_specs=[blk],
        out_specs=(blk, blk, blk, pl.BlockSpec((8, LANES), lambda i: (0, 0))),
        scratch_shapes=[pltpu.VMEM((8, LANES), F32)],
        compiler_params=_cparams(("arbitrary",)),
        name="route",
    )(logits)


def _schedule(counts, n_assign):
    max_sb = n_assign // SUPER_ROWS + N_EXPERTS
    nsb_e = (counts + SUPER_ROWS - 1) // SUPER_ROWS
    sb_end = jnp.cumsum(nsb_e)
    sb_start = sb_end - nsb_e
    row_start = (sb_start * SUPER_ROWS).astype(I32)
    nsb = sb_end[-1].astype(I32)
    s = jnp.arange(max_sb, dtype=I32)
    sb_e = jnp.minimum(jnp.searchsorted(sb_end, s, side="right"), N_EXPERTS - 1).astype(I32)
    rem = counts[sb_e] - (s - sb_start[sb_e]) * SUPER_ROWS
    sb_nb = jnp.where(s < nsb, (jnp.clip(rem, 0, SUPER_ROWS) + ROW_BLK - 1) // ROW_BLK, 0).astype(I32)
    zero_start = (s * SUPER_ROWS + jnp.maximum(sb_nb - 1, 0) * ROW_BLK).astype(I32)
    return max_sb, row_start, nsb.reshape(1), sb_e, sb_nb, zero_start


ROW_UNROLL = 8


def _dispatch_kernel(dest_s, zs_s, nsb_s, hp_ref, xs_hbm, zbuf, zsem, sem):
    step = pl.program_id(0)
    tt = hp_ref.shape[0]

    def zero_copy(s):
        return pltpu.make_async_copy(zbuf, xs_hbm.at[pl.ds(pl.multiple_of(zs_s[s], ROW_BLK), ROW_BLK)], zsem)

    @pl.when(step == 0)
    def _():
        zbuf[...] = jnp.zeros_like(zbuf)

        def zstart(s, c):
            zero_copy(s).start()
            return c

        def zwait(s, c):
            zero_copy(s).wait()
            return c

        lax.fori_loop(0, nsb_s[0], zstart, 0)
        lax.fori_loop(0, nsb_s[0], zwait, 0)

    base = step * tt * TOP_K

    def issue(jj, c):
        for u in range(ROW_UNROLL):
            j = jj * ROW_UNROLL + u
            for k in range(TOP_K):
                d = dest_s[base + j * TOP_K + k]
                pltpu.make_async_copy(hp_ref.at[pl.ds(j, 1)], xs_hbm.at[pl.ds(d, 1)], sem).start(
                    priority=k % 2)
        return c

    lax.fori_loop(0, tt // ROW_UNROLL, issue, 0)
    for k in range(TOP_K):
        pltpu.make_async_copy(hp_ref, xs_hbm.at[pl.ds(0, tt)], sem).wait()


def _dispatch(dest, zero_start, nsb, hp, n_rows):
    t, _, w = hp.shape
    tt = 256
    return pl.pallas_call(
        _dispatch_kernel,
        out_shape=jax.ShapeDtypeStruct((n_rows, 1, w), U32),
        grid_spec=pltpu.PrefetchScalarGridSpec(
            num_scalar_prefetch=3,
            grid=(t // tt,),
            in_specs=[pl.BlockSpec((tt, 1, w), lambda i, *_: (i, 0, 0))],
            out_specs=pl.BlockSpec(memory_space=pl.ANY),
            scratch_shapes=[pltpu.VMEM((ROW_BLK, 1, w), U32),
                            pltpu.SemaphoreType.DMA(()), pltpu.SemaphoreType.DMA(())]),
        compiler_params=_cparams(("arbitrary",)),
        name="dispatch",
    )(dest, zero_start, nsb, hp)


def _row_block(r):
    return pl.ds(pl.multiple_of(r * ROW_BLK, ROW_BLK), ROW_BLK)


def _moe_up_kernel(sbe_s, sbn_s, x_ref, win_ref, bin_ref, h_ref, winb, zbuf, xstage):
    nb = sbn_s[pl.program_id(0)]
    half = x_ref.shape[2]
    nsub = win_ref.shape[1] // (2 * LANES)
    lane = lax.broadcasted_iota(I32, (ROW_BLK, LANES), 1)
    even = (2 * lane) & (LANES - 1)
    odd = even + 1
    first_half = lane < LANES // 2

    def unpack(r):
        xstage[...] = x_ref[_row_block(r), 0, :]
        xu = xstage[...]
        xa = lax.bitcast_convert_type(lax.shift_left(xu, U32(16)), F32).astype(BF16)
        xb = lax.bitcast_convert_type(xu & U32(0xFFFF0000), F32).astype(BF16)
        return xa, xb

    def matmul(r):
        xa, xb = unpack(r)
        zbuf[...] = (jnp.dot(xa, winb[0:half, :], preferred_element_type=F32)
                     + jnp.dot(xb, winb[half:2 * half, :], preferred_element_type=F32) + bin_ref[...])

    def matmul_casting(r):
        xa, xb = unpack(r)
        kq = half // 2
        acc = bin_ref[...]
        for ks in range(4):
            wb = win_ref[ks * kq:(ks + 1) * kq, :].astype(BF16)
            winb[ks * kq:(ks + 1) * kq, :] = wb
            xpart = (xa, xb)[ks // 2][:, (ks % 2) * kq:(ks % 2 + 1) * kq]
            acc = acc + jnp.dot(xpart, wb, preferred_element_type=F32)
        zbuf[...] = acc

    def activation():
        hs = []
        for u in range(nsub):
            za = zbuf[:, (2 * u) * LANES:(2 * u + 1) * LANES]
            zb = zbuf[:, (2 * u + 1) * LANES:(2 * u + 2) * LANES]
            gate = jnp.where(first_half, jnp.take_along_axis(za, even, axis=1),
                             jnp.take_along_axis(zb, even, axis=1))
            up = jnp.where(first_half, jnp.take_along_axis(za, odd, axis=1),
                           jnp.take_along_axis(zb, odd, axis=1))
            gate = jnp.minimum(gate, SWIGLU_LIMIT)
            up = jnp.clip(up, -SWIGLU_LIMIT, SWIGLU_LIMIT)
            glu = gate * jax.nn.sigmoid(SWIGLU_ALPHA * gate)
            hs.append(((up + 1.0) * glu).astype(BF16))
        return jnp.concatenate(hs, axis=1)

    matmul_casting(0)

    def step(r, carry):
        h = activation()
        matmul(r + 1)
        h_ref[_row_block(r), :] = h
        return carry

    lax.fori_loop(0, nb - 1, step, 0)
    h_ref[_row_block(nb - 1), :] = activation()


def _moe_down_kernel(sbe_s, sbn_s, nsb_s, h_hbm, wout_ref, bout_ref, ys_hbm, woutb, hbuf, obuf, pend,
                     hsem, sem):
    s = pl.program_id(0)
    n = pl.program_id(1)
    nn = pl.num_programs(1)
    nb = sbn_s[s]
    chunk = obuf.shape[3]
    par = (s * nn + n) % 2
    hs = s % 2

    def h_copy(sb, slot, r):
        rows = pl.ds(pl.multiple_of(sb * SUPER_ROWS + r * ROW_BLK, ROW_BLK), ROW_BLK)
        return pltpu.make_async_copy(h_hbm.at[rows], hbuf.at[slot, _row_block(r)], hsem.at[slot])

    def fetch(sb, slot):
        def one(r, carry):
            h_copy(sb, slot, r).start()
            return carry

        lax.fori_loop(0, sbn_s[sb], one, 0)

    @pl.when(n == 0)
    def _():
        @pl.when(s == 0)
        def _():
            fetch(0, 0)

        @pl.when(s + 1 < nsb_s[0])
        def _():
            fetch(s + 1, 1 - hs)

        def one(r, carry):
            h_copy(s, hs, r).wait()
            return carry

        lax.fori_loop(0, nb, one, 0)

    def out_copy(r, par_):
        rows = pl.ds(pl.multiple_of(s * SUPER_ROWS + r * ROW_BLK, ROW_BLK), ROW_BLK)
        cols = pl.ds(pl.multiple_of(n * chunk, chunk), chunk)
        return pltpu.make_async_copy(obuf.at[par_, _row_block(r)], ys_hbm.at[rows, :, cols], sem.at[par_])

    def drain(par_):
        def one(i, carry):
            out_copy(0, par_).wait()
            return carry

        lax.fori_loop(0, pend[par_], one, 0)

    @pl.when((s == 0) & (n == 0))
    def _():
        pend[0] = 0
        pend[1] = 0

    drain(par)

    kq = hbuf.shape[2] // 4
    h0 = hbuf[hs, _row_block(0), :]
    acc = bout_ref[...]
    for ks in range(4):
        wb = wout_ref[ks * kq:(ks + 1) * kq, :].astype(BF16)
        woutb[ks * kq:(ks + 1) * kq, :] = wb
        acc = acc + jnp.dot(h0[:, ks * kq:(ks + 1) * kq], wb, preferred_element_type=F32)
    obuf[par, _row_block(0), 0, :] = acc

    def project(r, nrows):
        rows = pl.ds(pl.multiple_of(r * ROW_BLK, ROW_BLK), nrows)
        obuf[par, rows, 0, :] = jnp.dot(hbuf[hs, rows, :], woutb[...],
                                        preferred_element_type=F32) + bout_ref[...]

    def pair(p, carry):
        project(1 + 2 * p, 2 * ROW_BLK)
        return carry

    lax.fori_loop(0, (nb - 1) // 2, pair, 0)

    @pl.when((nb - 1) % 2 == 1)
    def _():
        project(nb - 1, ROW_BLK)

    def start(r, carry):
        out_copy(r, par).start()
        return carry

    lax.fori_loop(0, nb, start, 0)
    pend[par] = nb

    @pl.when((s == nsb_s[0] - 1) & (n == nn - 1))
    def _():
        drain(0)
        drain(1)


def _moe_ffn(sb_e, sb_nb, nsb, xs, w_in, b_in, w_out, b_out, max_sb):
    ne, d, f2 = w_in.shape
    ff = w_out.shape[1]
    half = xs.shape[2]
    rows = max_sb * SUPER_ROWS
    sem = ("arbitrary", "arbitrary")

    h = pl.pallas_call(
        _moe_up_kernel,
        out_shape=jax.ShapeDtypeStruct((rows, ff), BF16),
        grid_spec=pltpu.PrefetchScalarGridSpec(
            num_scalar_prefetch=2,
            grid=(nsb[0], ff // UP_CHUNK),
            in_specs=[pl.BlockSpec((SUPER_ROWS, 1, half), lambda s, j, sbe, sbn: (s, 0, 0)),
                      pl.BlockSpec((None, d, 2 * UP_CHUNK), lambda s, j, sbe, sbn: (sbe[s], 0, j)),
                      pl.BlockSpec((None, 1, 2 * UP_CHUNK), lambda s, j, sbe, sbn: (sbe[s], 0, j))],
            out_specs=pl.BlockSpec((SUPER_ROWS, UP_CHUNK), lambda s, j, sbe, sbn: (s, j)),
            scratch_shapes=[pltpu.VMEM((d, 2 * UP_CHUNK), BF16), pltpu.VMEM((ROW_BLK, 2 * UP_CHUNK), F32),
                            pltpu.VMEM((ROW_BLK, half), U32)]),
        compiler_params=_cparams(sem),
        name="moe_up",
    )(sb_e, sb_nb, xs, w_in, b_in.reshape(ne, 1, f2))

    return pl.pallas_call(
        _moe_down_kernel,
        out_shape=jax.ShapeDtypeStruct((rows, 1, d), F32),
        grid_spec=pltpu.PrefetchScalarGridSpec(
            num_scalar_prefetch=3,
            grid=(nsb[0], d // DOWN_CHUNK),
            in_specs=[pl.BlockSpec(memory_space=pl.ANY),
                      pl.BlockSpec((None, ff, DOWN_CHUNK), lambda s, n, sbe, sbn, nsb_: (sbe[s], 0, n)),
                      pl.BlockSpec((None, 1, DOWN_CHUNK), lambda s, n, sbe, sbn, nsb_: (sbe[s], 0, n))],
            out_specs=pl.BlockSpec(memory_space=pl.ANY),
            scratch_shapes=[pltpu.VMEM((ff, DOWN_CHUNK), BF16),
                            pltpu.VMEM((2, SUPER_ROWS, ff), BF16),
                            pltpu.VMEM((2, SUPER_ROWS, 1, DOWN_CHUNK), F32),
                            pltpu.SMEM((2,), I32), pltpu.SemaphoreType.DMA((2,)),
                            pltpu.SemaphoreType.DMA((2,))]),
        compiler_params=_cparams(sem),
        name="moe_down",
    )(sb_e, sb_nb, nsb, h, w_out, b_out.reshape(ne, 1, d))


COMBINE_GROUP = 32


def _combine_kernel(dest_s, gate_ref, x1_ref, gt2_ref, g_ref, ys_hbm, o_ref, buf, stage, sem):
    step = pl.program_id(0)
    nstep = pl.num_programs(0)
    tt = x1_ref.shape[0]
    slot = step % 2

    def issue(tile, slot_, jj):
        base = tile * tt * TOP_K
        for u in range(COMBINE_GROUP):
            j = jj * COMBINE_GROUP + u
            for k in range(TOP_K):
                d = dest_s[base + j * TOP_K + k]
                pltpu.make_async_copy(ys_hbm.at[pl.ds(d, 1)], buf.at[slot_, k, pl.ds(j, 1)],
                                      sem.at[slot_]).start(priority=k % 2)

    def finish(jj):
        rows = pl.ds(pl.multiple_of(jj * COMBINE_GROUP, COMBINE_GROUP), COMBINE_GROUP)
        g = gate_ref[rows, :]
        y = None
        for k in range(TOP_K):
            stage[k] = buf[slot, k, rows, 0, :]
            term = stage[k] * g[:, k:k + 1]
            y = term if y is None else y + term
        o_ref[rows, :] = x1_ref[rows, :] + gt2_ref[...] * _rms(y, g_ref[...])

    @pl.when(step == 0)
    def _():
        def first(jj, c):
            issue(0, 0, jj)
            return c

        lax.fori_loop(0, tt // COMBINE_GROUP, first, 0)

    for k in range(TOP_K):
        pltpu.make_async_copy(ys_hbm.at[pl.ds(0, tt)], buf.at[slot, k], sem.at[slot]).wait()

    @pl.when(step + 1 < nstep)
    def _():
        def both(jj, c):
            finish(jj)
            issue(step + 1, 1 - slot, jj)
            return c

        lax.fori_loop(0, tt // COMBINE_GROUP, both, 0)

    @pl.when(step + 1 == nstep)
    def _():
        def last(jj, c):
            finish(jj)
            return c

        lax.fori_loop(0, tt // COMBINE_GROUP, last, 0)


def _combine(dest, gates, x1, gt2, g_post, ys, seq):
    t, d = x1.shape
    tt = 256
    return pl.pallas_call(
        _combine_kernel,
        out_shape=jax.ShapeDtypeStruct((t, d), F32),
        grid_spec=pltpu.PrefetchScalarGridSpec(
            num_scalar_prefetch=1,
            grid=(t // tt,),
            in_specs=[pl.BlockSpec((tt, LANES), lambda i, *_: (i, 0)),
                      pl.BlockSpec((tt, d), lambda i, *_: (i, 0)),
                      pl.BlockSpec((None, 1, d), lambda i, *_: (i * tt // seq, 0, 0)),
                      pl.BlockSpec((1, d), lambda i, *_: (0, 0)),
                      pl.BlockSpec(memory_space=pl.ANY)],
            out_specs=pl.BlockSpec((tt, d), lambda i, *_: (i, 0)),
            scratch_shapes=[pltpu.VMEM((2, TOP_K, tt, 1, d), F32), pltpu.VMEM((TOP_K, COMBINE_GROUP, d), F32),
                            pltpu.SemaphoreType.DMA((2,))]),
        compiler_params=_cparams(("arbitrary",)),
        name="combine",
    )(dest, gates, x1, gt2, g_post.reshape(1, d), ys)


def kernel(x, c, w_ada, b_ada, g_pre_mix, g_post_mix, w_in, rpb_na, t5_table, g_out_na, g_out_dil, w_o,
           g_pre_ffn, g_post_ffn, w_router, b_router, w_e_in, b_e_in, w_e_out, b_e_out):
    bsz, seq, d = x.shape
    t = bsz * seq
    for l in range(w_ada.shape[0]):
        mod = _ada_mod(c, w_ada[l], b_ada[l])
        sh1, sc1, gt1, sh2, sc2, gt2 = (m.reshape(bsz, 1, d) for m in jnp.split(mod, 6, axis=-1))
        x2 = x.reshape(t, d)

        proj = _qkv_proj(x2, g_pre_mix[l], sc1, sh1, w_in[l], seq)
        out_na = _na_attn(proj, _na_bias_table(rpb_na[l]), bsz, seq)
        out_dil = _dil_attn(proj, t5_table, bsz, seq)

        ne = w_router.shape[-1]
        wr = jnp.zeros((d, LANES), BF16).at[:, :ne].set(w_router[l].astype(BF16))
        br = jnp.full((1, LANES), NEG, F32).at[0, :ne].set(b_router[l])
        x1, hp, logits = _mix_out(out_na, out_dil, g_out_na[l], g_out_dil[l], w_o[l].astype(BF16), x2,
                                  gt1, sc2, sh2, g_post_mix[l], g_pre_ffn[l], wr, br, seq)

        idx, rank, gates, cnt = _route(logits)
        max_sb, row_start, nsb, sb_e, sb_nb, zero_start = _schedule(cnt[0, :ne], t * TOP_K)
        eidx = idx[:, :TOP_K].reshape(-1)
        onehot = eidx[:, None] == jnp.arange(ne, dtype=I32)[None, :]
        dest = rank[:, :TOP_K].reshape(-1) + jnp.sum(jnp.where(onehot, row_start[None, :], 0), axis=1)
        xs = _dispatch(dest, zero_start, nsb, hp, max_sb * SUPER_ROWS)
        ys = _moe_ffn(sb_e, sb_nb, nsb, xs, w_e_in[l], b_e_in[l], w_e_out[l], b_e_out[l], max_sb)
        x = _combine(dest, gates, x1, gt2, g_post_ffn[l], ys, seq).reshape(bsz, seq, d)
    return x
```
